```python
import math
import jax, jax.numpy as jnp
from jax import lax
import numpy as np

D_MODEL = 1024
BATCH = 8
SEQ = 2048
DEPTH = 4
DEC_BATCH = 128
DEC_SEQ = 1
PAST_LEN = 16384
PAGE_SIZE = 128

N_MIXERS = 3
N_S5 = (DEPTH + 2) // N_MIXERS
N_RWKV = (DEPTH + 1) // N_MIXERS
N_RGLRU = DEPTH // N_MIXERS
RMS_EPS = 1e-6

S5_GROUP = 16
S5_GROUPS = D_MODEL // S5_GROUP
S5_STATE = 64
S5_DT_MIN = 1e-3
S5_DT_MAX = 1e-1

RWKV_HEAD = 64
RWKV_HEADS = D_MODEL // RWKV_HEAD
RWKV_LORA_W = 64
RWKV_LORA_A = 64
RWKV_LORA_G = 128
RWKV_GN_EPS = 64e-5

D_RNN = D_MODEL
LRU_BLOCKS = 4
LRU_BLOCK = D_RNN // LRU_BLOCKS
LRU_C = 8.0
LRU_CONV = 4

D_FF = 2816
FFN_CONV = 3

kernel_name = 'hybrid_s5_rwkv7_rglru_convffn_step'


def rms_norm(x, g):
    xf = x.astype(jnp.float32)
    xf = xf * lax.rsqrt(jnp.mean(xf * xf, axis=-1, keepdims=True) + RMS_EPS)
    return (xf * g.astype(jnp.float32)).astype(x.dtype)


def causal_dwconv(u, buf, w, b):
    width = w.shape[0]
    length = u.shape[1]
    ext = jnp.concatenate([buf.astype(u.dtype), u], axis=1)
    out = b + ext[:, 0:length] * w[0]
    for k in range(1, width):
        out = out + ext[:, k:k + length] * w[k]
    return out, ext[:, ext.shape[1] - (width - 1):]


def _lin_combine(e1, e2):
    a1, b1 = e1
    a2, b2 = e2
    return a1 * a2, a2 * b1 + b2


def _cplx_combine(e1, e2):
    a1r, a1i, b1r, b1i = e1
    a2r, a2i, b2r, b2i = e2
    return (a2r * a1r - a2i * a1i, a2r * a1i + a2i * a1r,
            a2r * b1r - a2i * b1i + b2r, a2r * b1i + a2i * b1r + b2i)


def s5_mixer(u, h0_re, h0_im, p):
    f32 = jnp.float32
    bsz, length, _ = u.shape
    ug = u.astype(f32).reshape(bsz, length, S5_GROUPS, S5_GROUP)
    lam_re = jnp.minimum(p['s5_a_re'].astype(f32), -1e-4)
    lam_im = p['s5_a_im'].astype(f32)
    dt = jnp.exp(p['s5_log_dt'].astype(f32))[:, None]
    mag = jnp.exp(lam_re * dt)
    ab_re = mag * jnp.cos(lam_im * dt)
    ab_im = mag * jnp.sin(lam_im * dt)
    den = lam_re * lam_re + lam_im * lam_im
    coef_re = ((ab_re - 1.0) * lam_re + ab_im * lam_im) / den
    coef_im = (ab_im * lam_re - (ab_re - 1.0) * lam_im) / den
    bu_re = jnp.einsum('blgc,gpc->blgp', ug, p['s5_b_re'].astype(f32))
    bu_im = jnp.einsum('blgc,gpc->blgp', ug, p['s5_b_im'].astype(f32))
    x_re = coef_re * bu_re - coef_im * bu_im
    x_im = coef_re * bu_im + coef_im * bu_re
    h0r = h0_re.astype(f32)
    h0i = h0_im.astype(f32)
    x_re = x_re.at[:, 0].add(ab_re * h0r - ab_im * h0i)
    x_im = x_im.at[:, 0].add(ab_re * h0i + ab_im * h0r)
    shp = x_re.shape
    _, _, h_re, h_im = lax.associative_scan(
        _cplx_combine,
        (jnp.broadcast_to(ab_re, shp), jnp.broadcast_to(ab_im, shp), x_re, x_im), axis=1)
    y = (jnp.einsum('blgp,gcp->blgc', h_re, p['s5_c_re'].astype(f32))
         - jnp.einsum('blgp,gcp->blgc', h_im, p['s5_c_im'].astype(f32)))
    y = y.reshape(bsz, length, D_MODEL) + p['s5_d'].astype(f32) * u.astype(f32)
    z = jnp.einsum('bld,de->ble', jax.nn.gelu(y).astype(u.dtype), p['s5_w_glu'])
    val, gate = jnp.split(z, 2, axis=-1)
    return val * jax.nn.sigmoid(gate), h_re[:, -1], h_im[:, -1]


def rwkv7_mixer(x, shift0, s0, p):
    f32 = jnp.float32
    bsz, length, _ = x.shape
    prev = jnp.concatenate([shift0[:, None].astype(x.dtype), x[:, :-1]], axis=1)
    xx = prev - x
    xs = x[None] + xx[None] * p['rw_mu'][:, None, None, :]
    rkv = jnp.einsum('nbld,nde->nble', xs[:3], p['rw_w_rkv']).astype(f32)
    r, k, v = rkv[0], rkv[1], rkv[2]
    xw, xa, xg = xs[3], xs[4], xs[5]
    wlog = -jax.nn.softplus(-(p['rw_w0'] + jnp.tanh(xw @ p['rw_w1']) @ p['rw_w2']).astype(f32)) - 0.5
    decay = jnp.exp(-jnp.exp(wlog))
    a = jax.nn.sigmoid((p['rw_a0'] + (xa @ p['rw_a1']) @ p['rw_a2']).astype(f32))
    g = (jax.nn.sigmoid(xg @ p['rw_g1']) @ p['rw_g2']).astype(f32)
    heads = lambda t: t.reshape(bsz, length, RWKV_HEADS, RWKV_HEAD)
    kk = heads(k * p['rw_k_k'].astype(f32))
    kk = kk / jnp.maximum(jnp.sqrt(jnp.sum(kk * kk, axis=-1, keepdims=True)), 1e-12)
    k = k * (1.0 + (a - 1.0) * p['rw_k_a'].astype(f32))
    rh, dh, kh, vh, ah = heads(r), heads(decay), heads(k), heads(v), heads(a)

    def step(S, inp):
        r_t, w_t, k_t, v_t, kk_t, a_t = inp
        sa = jnp.einsum('bhij,bhj->bhi', S, -kk_t)
        S = (S * w_t[:, :, None, :] + sa[..., :, None] * (kk_t * a_t)[..., None, :]
             + v_t[..., :, None] * k_t[..., None, :])
        return S, jnp.einsum('bhij,bhj->bhi', S, r_t)

    seq = tuple(jnp.moveaxis(t, 1, 0) for t in (rh, dh, kh, vh, kk, ah))
    s_fin, o = lax.scan(step, s0.astype(f32), seq)
    o = jnp.moveaxis(o, 0, 1)
    mean = jnp.mean(o, axis=-1, keepdims=True)
    var = jnp.mean(jnp.square(o - mean), axis=-1, keepdims=True)
    o = ((o - mean) * lax.rsqrt(var + RWKV_GN_EPS)).reshape(bsz, length, D_MODEL)
    o = o * p['rw_ln_w'].astype(f32) + p['rw_ln_b'].astype(f32)
    bonus = jnp.sum(rh * kh * p['rw_r_k'].astype(f32), axis=-1, keepdims=True) * vh
    o = (o + bonus.reshape(bsz, length, D_MODEL)) * g
    out = jnp.einsum('bld,de->ble', o.astype(x.dtype), p['rw_w_o'])
    return out, x[:, -1], s_fin


def rglru_mixer(x, conv0, h0, p):
    f32 = jnp.float32
    bsz, length, _ = x.shape
    gy = jnp.einsum('bld,de->ble', x, p['lru_w_in'])
    gate_br, u = jnp.split(gy, 2, axis=-1)
    gate_br = jax.nn.gelu(gate_br.astype(f32))
    u, conv_new = causal_dwconv(u, conv0, p['lru_conv_w'], p['lru_conv_b'])
    ub = u.reshape(bsz, length, LRU_BLOCKS, LRU_BLOCK)
    rg = jax.nn.sigmoid((jnp.einsum('blnc,ncd->blnd', ub, p['lru_w_rg']).reshape(bsz, length, D_RNN)
                         + p['lru_b_rg']).astype(f32))
    ig = jax.nn.sigmoid((jnp.einsum('blnc,ncd->blnd', ub, p['lru_w_ig']).reshape(bsz, length, D_RNN)
                         + p['lru_b_ig']).astype(f32))
    log_a = LRU_C * rg * jax.nn.log_sigmoid(p['lru_lambda'].astype(f32))
    a = jnp.exp(log_a)
    mult = jnp.sqrt(-jnp.expm1(2.0 * log_a))
    bx = mult * ig * u.astype(f32)
    bx = bx.at[:, 0].add(a[:, 0] * h0.astype(f32))
    _, h = lax.associative_scan(_lin_combine, (a, bx), axis=1)
    out = jnp.einsum('ble,ed->bld', (h * gate_br).astype(x.dtype), p['lru_w_out'])
    return out, conv_new, h[:, -1]


def conv_ffn(x, conv0, w_in, conv_w, conv_b, w_out):
    hu = jnp.einsum('bld,df->blf', x, w_in)
    gate, up = jnp.split(hu, 2, axis=-1)
    gate, conv_new = causal_dwconv(gate, conv0, conv_w, conv_b)
    out = jnp.einsum('blf,fd->bld', jax.nn.silu(gate) * up, w_out)
    return out, conv_new


def _layer_slice(w, prefix, j):
    return {name: arr[j] for name, arr in w.items() if name.startswith(prefix)}


def trunk(x, st, w):
    dt = x.dtype
    out = {'s5_re': [], 's5_im': [], 'rw_wkv': [], 'rw_shift': [], 'lru_h': [], 'lru_conv': [], 'ffn_conv': []}
    for i in range(DEPTH):
        kind, j = i % N_MIXERS, i // N_MIXERS
        h = rms_norm(x, w['norm_mix'][i])
        if kind == 0:
            y, hr, hi = s5_mixer(h, st['s5_re'][j], st['s5_im'][j], _layer_slice(w, 's5_', j))
            out['s5_re'].append(hr)
            out['s5_im'].append(hi)
        elif kind == 1:
            y, sh, s = rwkv7_mixer(h, st['rw_shift'][j], st['rw_wkv'][j], _layer_slice(w, 'rw_', j))
            out['rw_shift'].append(sh)
            out['rw_wkv'].append(s)
        else:
            y, cb, hl = rglru_mixer(h, st['lru_conv'][j], st['lru_h'][j], _layer_slice(w, 'lru_', j))
            out['lru_conv'].append(cb)
            out['lru_h'].append(hl)
        x = x + y.astype(dt)
        h = rms_norm(x, w['norm_ffn'][i])
        y, cb = conv_ffn(h, st['ffn_conv'][i], w['ffn_w_in'][i], w['ffn_conv_w'][i],
                         w['ffn_conv_b'][i], w['ffn_w_out'][i])
        out['ffn_conv'].append(cb)
        x = x + y.astype(dt)
    y = rms_norm(x, w['norm_final'])
    new = {name: jnp.stack(v).astype(st[name].dtype) for name, v in out.items()}
    return y, new


def setup_inputs(seed: int = 0) -> dict:
    key = jax.random.key(seed)
    ks = iter(jax.random.split(key, 64))
    f32 = jnp.float32
    nrm = lambda shape, scale: scale * jax.random.normal(next(ks), shape, f32)
    unif = lambda shape, lo, hi: jax.random.uniform(next(ks), shape, f32, lo, hi)
    D = D_MODEL
    inp = {}
    inp['x_prompt'] = nrm((BATCH, SEQ, D), 1.0)
    inp['x_sample'] = nrm((DEC_BATCH, DEC_SEQ, D), 1.0)
    inp['state_s5_re'] = nrm((N_S5, DEC_BATCH, S5_GROUPS, S5_STATE), 0.1)
    inp['state_s5_im'] = nrm((N_S5, DEC_BATCH, S5_GROUPS, S5_STATE), 0.1)
    inp['state_rwkv_wkv'] = nrm((N_RWKV, DEC_BATCH, RWKV_HEADS, RWKV_HEAD, RWKV_HEAD), 0.1)
    inp['state_rwkv_shift'] = nrm((N_RWKV, DEC_BATCH, D), 1.0)
    inp['state_lru_h'] = nrm((N_RGLRU, DEC_BATCH, D_RNN), 0.5)
    inp['state_lru_conv'] = nrm((N_RGLRU, DEC_BATCH, LRU_CONV - 1, D_RNN), 1.0)
    inp['state_ffn_conv'] = nrm((DEPTH, DEC_BATCH, FFN_CONV - 1, D_FF), 1.0)
    inp['norm_mix'] = 1.0 + nrm((DEPTH, D), 0.02)
    inp['norm_ffn'] = 1.0 + nrm((DEPTH, D), 0.02)
    inp['norm_final'] = 1.0 + nrm((D,), 0.02)
    inp['s5_a_re'] = -0.5 + nrm((N_S5, S5_GROUPS, S5_STATE), 0.01)
    inp['s5_a_im'] = math.pi * jnp.arange(S5_STATE, dtype=f32) + nrm((N_S5, S5_GROUPS, S5_STATE), 0.01)
    inp['s5_log_dt'] = unif((N_S5, S5_GROUPS), math.log(S5_DT_MIN), math.log(S5_DT_MAX))
    inp['s5_b_re'] = nrm((N_S5, S5_GROUPS, S5_STATE, S5_GROUP), (2 * S5_GROUP) ** -0.5)
    inp['s5_b_im'] = nrm((N_S5, S5_GROUPS, S5_STATE, S5_GROUP), (2 * S5_GROUP) ** -0.5)
    inp['s5_c_re'] = nrm((N_S5, S5_GROUPS, S5_GROUP, S5_STATE), S5_STATE ** -0.5)
    inp['s5_c_im'] = nrm((N_S5, S5_GROUPS, S5_GROUP, S5_STATE), S5_STATE ** -0.5)
    inp['s5_d'] = nrm((N_S5, D), 1.0)
    inp['s5_w_glu'] = nrm((N_S5, D, 2 * D), D ** -0.5)
    inp['rw_mu'] = unif((N_RWKV, 6, D), 0.0, 1.0)
    inp['rw_w_rkv'] = nrm((N_RWKV, 3, D, D), D ** -0.5)
    inp['rw_w0'] = unif((N_RWKV, D), -6.0, 1.0)
    inp['rw_w1'] = nrm((N_RWKV, D, RWKV_LORA_W), D ** -0.5)
    inp['rw_w2'] = nrm((N_RWKV, RWKV_LORA_W, D), 0.5 * RWKV_LORA_W ** -0.5)
    inp['rw_a0'] = nrm((N_RWKV, D), 0.5)
    inp['rw_a1'] = nrm((N_RWKV, D, RWKV_LORA_A), D ** -0.5)
    inp['rw_a2'] = nrm((N_RWKV, RWKV_LORA_A, D), 0.5 * RWKV_LORA_A ** -0.5)
    inp['rw_g1'] = nrm((N_RWKV, D, RWKV_LORA_G), D ** -0.5)
    inp['rw_g2'] = nrm((N_RWKV, RWKV_LORA_G, D), RWKV_LORA_G ** -0.5)
    inp['rw_k_k'] = 0.85 + nrm((N_RWKV, D), 0.05)
    inp['rw_k_a'] = 1.0 + nrm((N_RWKV, D), 0.05)
    inp['rw_r_k'] = nrm((N_RWKV, RWKV_HEADS, RWKV_HEAD), 0.1)
    inp['rw_ln_w'] = 1.0 + nrm((N_RWKV, D), 0.02)
    inp['rw_ln_b'] = nrm((N_RWKV, D), 0.02)
    inp['rw_w_o'] = nrm((N_RWKV, D, D), D ** -0.5)
    inp['lru_w_in'] = nrm((N_RGLRU, D, 2 * D_RNN), D ** -0.5)
    inp['lru_conv_w'] = nrm((N_RGLRU, LRU_CONV, D_RNN), LRU_CONV ** -0.5)
    inp['lru_conv_b'] = nrm((N_RGLRU, D_RNN), 0.02)
    inp['lru_w_rg'] = nrm((N_RGLRU, LRU_BLOCKS, LRU_BLOCK, LRU_BLOCK), LRU_BLOCK ** -0.5)
    inp['lru_b_rg'] = nrm((N_RGLRU, D_RNN), 0.02)
    inp['lru_w_ig'] = nrm((N_RGLRU, LRU_BLOCKS, LRU_BLOCK, LRU_BLOCK), LRU_BLOCK ** -0.5)
    inp['lru_b_ig'] = nrm((N_RGLRU, D_RNN), 0.02)
    s = unif((N_RGLRU, D_RNN), 0.9, 0.999) ** (1.0 / LRU_C)
    inp['lru_lambda'] = jnp.log(s) - jnp.log1p(-s)
    inp['lru_w_out'] = nrm((N_RGLRU, D_RNN, D), D_RNN ** -0.5)
    inp['ffn_w_in'] = nrm((DEPTH, D, 2 * D_FF), D ** -0.5)
    inp['ffn_conv_w'] = nrm((DEPTH, FFN_CONV, D_FF), FFN_CONV ** -0.5)
    inp['ffn_conv_b'] = nrm((DEPTH, D_FF), 0.02)
    inp['ffn_w_out'] = nrm((DEPTH, D_FF, D), D_FF ** -0.5)
    return inp


def reference(x_prompt, x_sample, state_s5_re, state_s5_im, state_rwkv_wkv, state_rwkv_shift,
              state_lru_h, state_lru_conv, state_ffn_conv, norm_mix, norm_ffn, norm_final,
              s5_a_re, s5_a_im, s5_log_dt, s5_b_re, s5_b_im, s5_c_re, s5_c_im, s5_d, s5_w_glu,
              rw_mu, rw_w_rkv, rw_w0, rw_w1, rw_w2, rw_a0, rw_a1, rw_a2, rw_g1, rw_g2, rw_k_k,
              rw_k_a, rw_r_k, rw_ln_w, rw_ln_b, rw_w_o, lru_w_in, lru_conv_w, lru_conv_b,
              lru_w_rg, lru_b_rg, lru_w_ig, lru_b_ig, lru_lambda, lru_w_out, ffn_w_in,
              ffn_conv_w, ffn_conv_b, ffn_w_out):
    w = {'norm_mix': norm_mix, 'norm_ffn': norm_ffn, 'norm_final': norm_final,
         's5_a_re': s5_a_re, 's5_a_im': s5_a_im, 's5_log_dt': s5_log_dt, 's5_b_re': s5_b_re,
         's5_b_im': s5_b_im, 's5_c_re': s5_c_re, 's5_c_im': s5_c_im, 's5_d': s5_d, 's5_w_glu': s5_w_glu,
         'rw_mu': rw_mu, 'rw_w_rkv': rw_w_rkv, 'rw_w0': rw_w0, 'rw_w1': rw_w1, 'rw_w2': rw_w2,
         'rw_a0': rw_a0, 'rw_a1': rw_a1, 'rw_a2': rw_a2, 'rw_g1': rw_g1, 'rw_g2': rw_g2,
         'rw_k_k': rw_k_k, 'rw_k_a': rw_k_a, 'rw_r_k': rw_r_k, 'rw_ln_w': rw_ln_w, 'rw_ln_b': rw_ln_b,
         'rw_w_o': rw_w_o, 'lru_w_in': lru_w_in, 'lru_conv_w': lru_conv_w, 'lru_conv_b': lru_conv_b,
         'lru_w_rg': lru_w_rg, 'lru_b_rg': lru_b_rg, 'lru_w_ig': lru_w_ig, 'lru_b_ig': lru_b_ig,
         'lru_lambda': lru_lambda, 'lru_w_out': lru_w_out, 'ffn_w_in': ffn_w_in,
         'ffn_conv_w': ffn_conv_w, 'ffn_conv_b': ffn_conv_b, 'ffn_w_out': ffn_w_out}
    bsz, dt = x_prompt.shape[0], x_prompt.dtype
    st_prompt = {'s5_re': jnp.zeros((N_S5, bsz, S5_GROUPS, S5_STATE), dt),
                 's5_im': jnp.zeros((N_S5, bsz, S5_GROUPS, S5_STATE), dt),
                 'rw_wkv': jnp.zeros((N_RWKV, bsz, RWKV_HEADS, RWKV_HEAD, RWKV_HEAD), dt),
                 'rw_shift': jnp.zeros((N_RWKV, bsz, D_MODEL), dt),
                 'lru_h': jnp.zeros((N_RGLRU, bsz, D_RNN), dt),
                 'lru_conv': jnp.zeros((N_RGLRU, bsz, LRU_CONV - 1, D_RNN), dt),
                 'ffn_conv': jnp.zeros((DEPTH, bsz, FFN_CONV - 1, D_FF), dt)}
    st_sample = {'s5_re': state_s5_re, 's5_im': state_s5_im, 'rw_wkv': state_rwkv_wkv,
                 'rw_shift': state_rwkv_shift, 'lru_h': state_lru_h, 'lru_conv': state_lru_conv,
                 'ffn_conv': state_ffn_conv}
    y_prompt, new_p = trunk(x_prompt, st_prompt, w)
    y_sample, new_s = trunk(x_sample, st_sample, w)
    return (y_prompt, y_sample, new_p['s5_re'], new_s['s5_re'], new_p['s5_im'], new_s['s5_im'],
            new_p['rw_wkv'], new_s['rw_wkv'], new_p['rw_shift'], new_s['rw_shift'],
            new_p['lru_h'], new_s['lru_h'], new_p['lru_conv'], new_s['lru_conv'],
            new_p['ffn_conv'], new_s['ffn_conv'])
```

```python
import functools

import jax
import jax.numpy as jnp
from jax import lax
from jax.experimental import pallas as pl
from jax.experimental.pallas import tpu as pltpu

F32 = jnp.float32
BF16 = jnp.bfloat16

D_MODEL = 1024
DEPTH = 4
N_MIXERS = 3
RMS_EPS = 1e-6

S5_GROUP = 16
S5_GROUPS = D_MODEL // S5_GROUP
S5_STATE = 64
S5_SLABS = 8
S5_SLAB_STATE = (S5_GROUPS // S5_SLABS) * S5_STATE
S5_NSTATE = S5_GROUPS * S5_STATE

RWKV_HEAD = 64
RWKV_HEADS = D_MODEL // RWKV_HEAD
RWKV_PAIRS = RWKV_HEADS // 2
RWKV_GN_EPS = 64e-5
LORA_PAD = 128

D_RNN = D_MODEL
LRU_BLOCKS = 4
LRU_BLOCK = D_RNN // LRU_BLOCKS
LRU_C = 8.0
LRU_CONV = 4

D_FF = 2816
FFN_CONV = 3
FFN_CHUNK = 256
FFN_NCHUNK = D_FF // FFN_CHUNK

LANES = 128
WKV_CHUNK = 64
VMEM_LIMIT = 60 * 1024 * 1024


def _cparams():
    return pltpu.CompilerParams(dimension_semantics=("arbitrary",), vmem_limit_bytes=VMEM_LIMIT)


def _const_spec(shape):
    nd = len(shape)
    return pl.BlockSpec(shape, lambda i, _n=nd: (0,) * _n, pipeline_mode=pl.Buffered(1))


def _row_spec(tm, width):
    return pl.BlockSpec((tm, width), lambda i: (i, 0))


def _rms(x, g):
    ms = jnp.mean(x * x, axis=-1, keepdims=True)
    return x * lax.rsqrt(ms + RMS_EPS) * g


def _bdot(a, w):
    return jnp.dot(a.astype(BF16), w, preferred_element_type=F32)


def _softplus(z):
    return jnp.maximum(z, 0.0) + jnp.log1p(jnp.exp(-jnp.abs(z)))


def _expm1(x):
    u = jnp.exp(x)
    near = u == 1.0
    ratio = (u - 1.0) * x / jnp.where(near, 1.0, jnp.log(u))
    return jnp.where(near, x, jnp.where(jnp.abs(x) < 0.5, ratio, u - 1.0))


def _split2(x):
    hi = x.astype(BF16)
    lo = (x - hi.astype(F32)).astype(BF16)
    return hi, lo


def _head_sum(x, ones_bd):
    hi, lo = _split2(x)
    return (jnp.dot(hi, ones_bd, preferred_element_type=F32)
            + jnp.dot(lo, ones_bd, preferred_element_type=F32))


def _ffn_body(x_ref, g_ref, win_ref, cw_ref, cb_ref, wout_ref, c0_ref, gf_ref,
              o_ref, cnew_ref, carry_ref, *, B, TM, final_norm):
    hist = (FFN_CONV - 1) * B

    @pl.when(pl.program_id(0) == 0)
    def _():
        carry_ref[...] = c0_ref[...]

    x = x_ref[...]
    h = _rms(x, g_ref[...]).astype(BF16)
    acc = jnp.zeros((TM, D_MODEL), F32)
    for c in range(FFN_NCHUNK):
        lo = c * FFN_CHUNK
        gate = jnp.dot(h, win_ref[:, lo:lo + FFN_CHUNK], preferred_element_type=F32)
        up = jnp.dot(h, win_ref[:, D_FF + lo:D_FF + lo + FFN_CHUNK], preferred_element_type=F32)
        ext = jnp.concatenate([carry_ref[:, lo:lo + FFN_CHUNK], gate], axis=0)
        conv = cb_ref[:, lo:lo + FFN_CHUNK] + ext[0:TM] * cw_ref[0:1, lo:lo + FFN_CHUNK]
        for k in range(1, FFN_CONV):
            conv = conv + ext[k * B:k * B + TM] * cw_ref[k:k + 1, lo:lo + FFN_CHUNK]
        carry_ref[:, lo:lo + FFN_CHUNK] = ext[TM:TM + hist]
        act = (conv * jax.nn.sigmoid(conv)) * up
        acc = acc + jnp.dot(act.astype(BF16), wout_ref[lo:lo + FFN_CHUNK, :],
                            preferred_element_type=F32)
    y = x + acc
    if final_norm:
        y = _rms(y, gf_ref[...])
    o_ref[...] = y
    cnew_ref[...] = carry_ref[...]


def _ffn(x, g, win, cw, cb, wout, c0, gf, *, B, TM, final_norm):
    n = x.shape[0]
    hist = (FFN_CONV - 1) * B
    body = functools.partial(_ffn_body, B=B, TM=TM, final_norm=final_norm)
    return pl.pallas_call(
        body,
        grid=(n // TM,),
        in_specs=[_row_spec(TM, D_MODEL), _const_spec((1, D_MODEL)), _const_spec(win.shape),
                  _const_spec(cw.shape), _const_spec(cb.shape), _const_spec(wout.shape),
                  _const_spec(c0.shape), _const_spec((1, D_MODEL))],
        out_specs=[_row_spec(TM, D_MODEL), _const_spec((hist, D_FF))],
        out_shape=[jax.ShapeDtypeStruct((n, D_MODEL), F32), jax.ShapeDtypeStruct((hist, D_FF), F32)],
        scratch_shapes=[pltpu.VMEM((hist, D_FF), F32)],
        compiler_params=_cparams(),
        name="conv_ffn",
    )(x, g, win, cw, cb, wout, c0, gf)


def _s5_body(x_ref, g_ref, wbr_ref, wbi_ref, cfr_ref, cfi_ref, abr_ref, abi_ref, wcr_ref, wci_ref,
             d_ref, wglu_ref, h0r_ref, h0i_ref, o_ref, hr_out, hi_out,
             xr_s, xi_s, hr_s, hi_s, y_s, *, B, TM):
    @pl.when(pl.program_id(0) == 0)
    def _():
        hr_s[...] = h0r_ref[...]
        hi_s[...] = h0i_ref[...]

    x = x_ref[...]
    u = _rms(x, g_ref[...])
    ub = u.astype(BF16)
    steps = TM // B
    for s in range(S5_SLABS):
        sl = slice(s * S5_SLAB_STATE, (s + 1) * S5_SLAB_STATE)
        us = ub[:, s * LANES:(s + 1) * LANES]
        bur = jnp.dot(us, wbr_ref[s], preferred_element_type=F32)
        bui = jnp.dot(us, wbi_ref[s], preferred_element_type=F32)
        cr = cfr_ref[:, sl]
        ci = cfi_ref[:, sl]
        xr_s[...] = cr * bur - ci * bui
        xi_s[...] = cr * bui + ci * bur
        ar = jnp.broadcast_to(abr_ref[:, sl], (B, S5_SLAB_STATE))
        ai = jnp.broadcast_to(abi_ref[:, sl], (B, S5_SLAB_STATE))

        def step(t, carry, ar=ar, ai=ai):
            hr, hi = carry
            rows = pl.ds(pl.multiple_of(t * B, B), B)
            nr = (ar * hr - ai * hi) + xr_s[rows, :]
            ni = (ar * hi + ai * hr) + xi_s[rows, :]
            xr_s[rows, :] = nr
            xi_s[rows, :] = ni
            return nr, ni

        hr, hi = lax.fori_loop(0, steps, step, (hr_s[:, sl], hi_s[:, sl]))
        hr_s[:, sl] = hr
        hi_s[:, sl] = hi
        y_s[:, s * LANES:(s + 1) * LANES] = (
            jnp.dot(xr_s[...].astype(BF16), wcr_ref[s], preferred_element_type=F32)
            - jnp.dot(xi_s[...].astype(BF16), wci_ref[s], preferred_element_type=F32))
    y = y_s[...] + d_ref[...] * u
    z = _bdot(jax.nn.gelu(y), wglu_ref[...])
    o_ref[...] = x + z[:, :D_MODEL] * jax.nn.sigmoid(z[:, D_MODEL:])
    hr_out[...] = hr_s[...]
    hi_out[...] = hi_s[...]


def _s5(x, g, prm, h0r, h0i, *, B, TM):
    n = x.shape[0]
    body = functools.partial(_s5_body, B=B, TM=TM)
    consts = [g, prm['wbr'], prm['wbi'], prm['cfr'], prm['cfi'], prm['abr'], prm['abi'],
              prm['wcr'], prm['wci'], prm['d'], prm['wglu'], h0r, h0i]
    return pl.pallas_call(
        body,
        grid=(n // TM,),
        in_specs=[_row_spec(TM, D_MODEL)] + [_const_spec(a.shape) for a in consts],
        out_specs=[_row_spec(TM, D_MODEL), _const_spec((B, S5_NSTATE)), _const_spec((B, S5_NSTATE))],
        out_shape=[jax.ShapeDtypeStruct((n, D_MODEL), F32),
                   jax.ShapeDtypeStruct((B, S5_NSTATE), F32),
                   jax.ShapeDtypeStruct((B, S5_NSTATE), F32)],
        scratch_shapes=[pltpu.VMEM((TM, S5_SLAB_STATE), F32), pltpu.VMEM((TM, S5_SLAB_STATE), F32),
                        pltpu.VMEM((B, S5_NSTATE), F32), pltpu.VMEM((B, S5_NSTATE), F32),
                        pltpu.VMEM((TM, D_MODEL), F32)],
        compiler_params=_cparams(),
        name="s5_mixer",
    )(x, *consts)


def _s5_params(p):
    lam_re = jnp.minimum(p['s5_a_re'], -1e-4)
    lam_im = p['s5_a_im']
    dt = jnp.exp(p['s5_log_dt'])[:, None]
    mag = jnp.exp(lam_re * dt)
    ab_re = mag * jnp.cos(lam_im * dt)
    ab_im = mag * jnp.sin(lam_im * dt)
    den = lam_re * lam_re + lam_im * lam_im
    coef_re = ((ab_re - 1.0) * lam_re + ab_im * lam_im) / den
    coef_im = (ab_im * lam_re - (ab_re - 1.0) * lam_im) / den
    gps = S5_GROUPS // S5_SLABS
    eye = jnp.eye(gps, dtype=F32)

    def slab_in(b):
        b4 = b.reshape(S5_SLABS, gps, S5_STATE, S5_GROUP)
        w = jnp.einsum('sgpc,gh->sgchp', b4, eye)
        return w.reshape(S5_SLABS, gps * S5_GROUP, gps * S5_STATE).astype(BF16)

    def slab_out(c):
        c4 = c.reshape(S5_SLABS, gps, S5_GROUP, S5_STATE)
        w = jnp.einsum('sgcp,gh->sgphc', c4, eye)
        return w.reshape(S5_SLABS, gps * S5_STATE, gps * S5_GROUP).astype(BF16)

    flat = lambda a: a.reshape(1, S5_NSTATE)
    return dict(wbr=slab_in(p['s5_b_re']), wbi=slab_in(p['s5_b_im']),
                wcr=slab_out(p['s5_c_re']), wci=slab_out(p['s5_c_im']),
                cfr=flat(coef_re), cfi=flat(coef_im), abr=flat(ab_re), abi=flat(ab_im),
                d=p['s5_d'].reshape(1, D_MODEL), wglu=p['s5_w_glu'].astype(BF16))


def _lru_body(x_ref, g_ref, win_ref, cw_ref, cb_ref, wrg_ref, brg_ref, wig_ref, big_ref, lam_ref,
              wout_ref, c0_ref, h0_ref, o_ref, cnew_ref, hnew_ref,
              carry_s, h_s, a_s, bx_s, *, B, TM):
    hist = (LRU_CONV - 1) * B

    @pl.when(pl.program_id(0) == 0)
    def _():
        carry_s[...] = c0_ref[...]
        h_s[...] = h0_ref[...]

    x = x_ref[...]
    xn = _rms(x, g_ref[...])
    gy = _bdot(xn, win_ref[...])
    gate_br = jax.nn.gelu(gy[:, :D_RNN])
    ext = jnp.concatenate([carry_s[...], gy[:, D_RNN:]], axis=0)
    u = cb_ref[...] + ext[0:TM] * cw_ref[0:1, :]
    for k in range(1, LRU_CONV):
        u = u + ext[k * B:k * B + TM] * cw_ref[k:k + 1, :]
    carry_s[...] = ext[TM:TM + hist]
    ub = u.astype(BF16)
    rg_parts, ig_parts = [], []
    for nb in range(LRU_BLOCKS):
        blk = ub[:, nb * LRU_BLOCK:(nb + 1) * LRU_BLOCK]
        rg_parts.append(jnp.dot(blk, wrg_ref[nb], preferred_element_type=F32))
        ig_parts.append(jnp.dot(blk, wig_ref[nb], preferred_element_type=F32))
    rg = jax.nn.sigmoid(jnp.concatenate(rg_parts, axis=1) + brg_ref[...])
    ig = jax.nn.sigmoid(jnp.concatenate(ig_parts, axis=1) + big_ref[...])
    log_sig = -_softplus(-lam_ref[...])
    log_a = LRU_C * rg * log_sig
    a_s[...] = jnp.exp(log_a)
    bx_s[...] = jnp.sqrt(-_expm1(2.0 * log_a)) * ig * u

    def step(t, h):
        rows = pl.ds(pl.multiple_of(t * B, B), B)
        h = a_s[rows, :] * h + bx_s[rows, :]
        bx_s[rows, :] = h
        return h

    h_last = lax.fori_loop(0, TM // B, step, h_s[...])
    h_s[...] = h_last
    o_ref[...] = x + _bdot(bx_s[...] * gate_br, wout_ref[...])
    cnew_ref[...] = carry_s[...]
    hnew_ref[...] = h_last


def _lru(x, g, prm, c0, h0, *, B, TM):
    n = x.shape[0]
    hist = (LRU_CONV - 1) * B
    body = functools.partial(_lru_body, B=B, TM=TM)
    consts = [g, prm['win'], prm['cw'], prm['cb'], prm['wrg'], prm['brg'], prm['wig'], prm['big'],
              prm['lam'], prm['wout'], c0, h0]
    return pl.pallas_call(
        body,
        grid=(n // TM,),
        in_specs=[_row_spec(TM, D_MODEL)] + [_const_spec(a.shape) for a in consts],
        out_specs=[_row_spec(TM, D_MODEL), _const_spec((hist, D_RNN)), _const_spec((B, D_RNN))],
        out_shape=[jax.ShapeDtypeStruct((n, D_MODEL), F32),
                   jax.ShapeDtypeStruct((hist, D_RNN), F32),
                   jax.ShapeDtypeStruct((B, D_RNN), F32)],
        scratch_shapes=[pltpu.VMEM((hist, D_RNN), F32), pltpu.VMEM((B, D_RNN), F32),
                        pltpu.VMEM((TM, D_RNN), F32), pltpu.VMEM((TM, D_RNN), F32)],
        compiler_params=_cparams(),
        name="rglru_mixer",
    )(x, *consts)


def _lru_params(p):
    row = lambda a: a.reshape(1, D_RNN)
    return dict(win=p['lru_w_in'].astype(BF16), cw=p['lru_conv_w'], cb=row(p['lru_conv_b']),
                wrg=p['lru_w_rg'].astype(BF16), brg=row(p['lru_b_rg']),
                wig=p['lru_w_ig'].astype(BF16), big=row(p['lru_b_ig']),
                lam=row(p['lru_lambda']), wout=p['lru_w_out'].astype(BF16))


def _rw_pre_body(x_ref, g_ref, mu_ref, wrkv_ref, w0_ref, w1_ref, w2_ref, a0_ref, a1_ref, a2_ref,
                 g1_ref, g2_ref, kk_ref, ka_ref, ones_ref, sh0_ref,
                 r_o, cum_o, lw_o, k_o, v_o, kk_o, b_o, g_o, sh_o, sh_s, stage_s, *, B, T):
    TM = T * B

    @pl.when(pl.program_id(0) == 0)
    def _():
        sh_s[...] = sh0_ref[...]

    def emit(o_ref, val):
        if T == 1:
            o_ref[...] = val
        else:
            for j in range(D_MODEL // LANES):
                stage_s[j] = val[:, j * LANES:(j + 1) * LANES]
                for b in range(B):
                    o_ref[b * T:(b + 1) * T, j * LANES:(j + 1) * LANES] = (
                        stage_s[j, pl.ds(b, T, stride=B), :])

    xn = _rms(x_ref[...], g_ref[...])
    if TM > B:
        prev = jnp.concatenate([sh_s[...], xn[:TM - B]], axis=0)
    else:
        prev = sh_s[...]
    sh_s[...] = xn[TM - B:]
    xx = prev - xn
    mix = lambda n: xn + xx * mu_ref[n:n + 1, :]
    emit(r_o, _bdot(mix(0), wrkv_ref[0]))
    emit(v_o, _bdot(mix(2), wrkv_ref[2]))
    emit(g_o, _bdot(jax.nn.sigmoid(_bdot(mix(5), g1_ref[...])), g2_ref[...]))
    wl = w0_ref[...] + _bdot(jnp.tanh(_bdot(mix(3), w1_ref[...])), w2_ref[...])
    lw = -jnp.exp(-_softplus(-wl) - 0.5)
    emit(lw_o, lw)
    cum = lw
    sh = B
    while sh < TM:
        cum = cum + jnp.concatenate([jnp.zeros((sh, D_MODEL), F32), cum[:TM - sh]], axis=0)
        sh *= 2
    emit(cum_o, cum)
    a = jax.nn.sigmoid(a0_ref[...] + _bdot(_bdot(mix(4), a1_ref[...]), a2_ref[...]))
    k = _bdot(mix(1), wrkv_ref[1])
    kk = k * kk_ref[...]
    norm = jnp.sqrt(_head_sum(kk * kk, ones_ref[...]))
    kk = kk / jnp.maximum(norm, 1e-12)
    emit(kk_o, kk)
    emit(b_o, kk * a)
    emit(k_o, k * (1.0 + (a - 1.0) * ka_ref[...]))
    sh_o[...] = sh_s[...]


def _rw_pre(x, g, prm, sh0, *, B, T):
    n = x.shape[0]
    tm = T * B
    body = functools.partial(_rw_pre_body, B=B, T=T)
    consts = [g, prm['mu'], prm['wrkv'], prm['w0'], prm['w1'], prm['w2'], prm['a0'], prm['a1'],
              prm['a2'], prm['g1'], prm['g2'], prm['k_k'], prm['k_a'], prm['ones'], sh0]
    big = jax.ShapeDtypeStruct((n, D_MODEL), F32)
    return pl.pallas_call(
        body,
        grid=(n // tm,),
        in_specs=[_row_spec(tm, D_MODEL)] + [_const_spec(a.shape) for a in consts],
        out_specs=[_row_spec(tm, D_MODEL)] * 8 + [_const_spec((B, D_MODEL))],
        out_shape=[big] * 8 + [jax.ShapeDtypeStruct((B, D_MODEL), F32)],
        scratch_shapes=[pltpu.VMEM((B, D_MODEL), F32), pltpu.VMEM((D_MODEL // LANES, tm, LANES), F32)],
        compiler_params=_cparams(),
        name="rwkv_project",
    )(x, *consts)


def _rw_post_body(x_ref, o_ref, r_ref, k_ref, v_ref, g_ref, rk_ref, lnw_ref, lnb_ref, ones_ref,
                  wo_ref, out_ref, stage_s, *, B, T):
    ones = ones_ref[...]
    o = o_ref[...]
    mean = _head_sum(o, ones) * (1.0 / RWKV_HEAD)
    dlt = o - mean
    var = _head_sum(dlt * dlt, ones) * (1.0 / RWKV_HEAD)
    on = dlt * lax.rsqrt(var + RWKV_GN_EPS) * lnw_ref[...] + lnb_ref[...]
    bonus = _head_sum(r_ref[...] * k_ref[...] * rk_ref[...], ones) * v_ref[...]
    y = _bdot((on + bonus) * g_ref[...], wo_ref[...])
    if T == 1:
        out_ref[...] = x_ref[...] + y
    else:
        for j in range(D_MODEL // LANES):
            cols = slice(j * LANES, (j + 1) * LANES)
            for b in range(B):
                stage_s[j, pl.ds(b, T, stride=B), :] = y[b * T:(b + 1) * T, cols]
            out_ref[:, cols] = x_ref[:, cols] + stage_s[j]


def _rw_post(x, o, r, k, v, g, prm, *, B, T):
    n = x.shape[0]
    tm = T * B
    consts = [prm['r_k'], prm['ln_w'], prm['ln_b'], prm['ones'], prm['wo']]
    return pl.pallas_call(
        functools.partial(_rw_post_body, B=B, T=T),
        grid=(n // tm,),
        in_specs=[_row_spec(tm, D_MODEL)] * 6 + [_const_spec(a.shape) for a in consts],
        out_specs=_row_spec(tm, D_MODEL),
        out_shape=jax.ShapeDtypeStruct((n, D_MODEL), F32),
        scratch_shapes=[pltpu.VMEM((D_MODEL // LANES, tm, LANES), F32)],
        compiler_params=_cparams(),
        name="rwkv_output",
    )(x, o, r, k, v, g, *consts)


def _wkv_chunk_body(r_ref, cum_ref, lw_ref, k_ref, v_ref, kk_ref, b_ref, s0_ref, o_ref, sT_ref,
                    S_s, *, B, T):
    @pl.when(pl.program_id(0) == 0)
    def _():
        S_s[...] = s0_ref[...]

    lane = lax.broadcasted_iota(jnp.int32, (T, LANES), 1)
    trow = lax.broadcasted_iota(jnp.int32, (T, LANES), 0)
    first = lane < RWKV_HEAD
    strict = trow > (lane & (T - 1))
    incl = trow >= (lane & (T - 1))
    ri = lax.broadcasted_iota(jnp.int32, (LANES, LANES), 0)
    ci = lax.broadcasted_iota(jnp.int32, (LANES, LANES), 1)
    same_head = (ri < RWKV_HEAD) == (ci < RWKV_HEAD)

    def stack(y):
        return jnp.concatenate([jnp.where(first, y, 0.0), jnp.where(first, 0.0, y)],
                               axis=0).astype(BF16)

    nt = (((1,), (1,)), ((), ()))
    tn = (((0,), (0,)), ((), ()))
    nsteps = T.bit_length() - 1

    def per_b(b, carry):
        rows = pl.ds(pl.multiple_of(b * T, T), T)
        for hp in range(RWKV_PAIRS):
            ls = slice(hp * LANES, (hp + 1) * LANES)
            r = r_ref[rows, ls]
            cm = cum_ref[rows, ls]
            k = k_ref[rows, ls]
            v = v_ref[rows, ls]
            kk = kk_ref[rows, ls]
            bb = b_ref[rows, ls]
            cmx = cm - lw_ref[rows, ls]
            mid = cm[T // 2 - 1:T // 2, :]
            end = cm[T - 1:T, :]
            at = -kk * jnp.exp(cmx - mid)
            rt = r * jnp.exp(cm - mid)
            e_neg = jnp.exp(mid - cm)
            e_end = jnp.exp(end - cm)
            lhs_f = jnp.concatenate([at, rt], axis=0)
            lhs = lhs_f.astype(BF16)
            rhs = jnp.concatenate([stack(bb * e_neg), stack(k * e_neg)], axis=0)
            p = lax.dot_general(lhs, rhs, nt, preferred_element_type=F32)
            s_old = S_s[b, hp]
            sp = lax.dot_general((lhs_f * jnp.exp(mid)).astype(BF16), s_old.astype(BF16), nt,
                                 preferred_element_type=F32)
            ab = jnp.where(strict, p[0:T, 0:2 * T], 0.0)
            ak = jnp.where(strict, p[0:T, 2 * T:4 * T], 0.0)
            rb = jnp.where(incl, p[T:2 * T, 0:2 * T], 0.0)
            rk = jnp.where(incl, p[T:2 * T, 2 * T:4 * T], 0.0)
            vst = stack(v)
            y = sp[0:T] + jnp.dot(ak.astype(BF16), vst, preferred_element_type=F32)
            m = ab
            for it in range(nsteps):
                y = y + jnp.dot(m.astype(BF16), stack(y), preferred_element_type=F32)
                if it + 1 < nsteps:
                    m = jnp.dot(m.astype(BF16), stack(m), preferred_element_type=F32)
            o = (sp[T:2 * T] + jnp.dot(rb.astype(BF16), stack(y), preferred_element_type=F32)
                 + jnp.dot(rk.astype(BF16), vst, preferred_element_type=F32))
            o_ref[rows, ls] = o
            upd = lax.dot_general(jnp.concatenate([y, v], axis=0).astype(BF16),
                                  jnp.concatenate([bb * e_end, k * e_end], axis=0).astype(BF16),
                                  tn, preferred_element_type=F32)
            S_s[b, hp] = s_old * jnp.exp(end) + jnp.where(same_head, upd, 0.0)
        return carry

    lax.fori_loop(0, B, per_b, 0)
    sT_ref[...] = S_s[...]


def _wkv_chunk(r, cum, lw, k, v, kk, bvec, s0, *, B, T):
    n = r.shape[0]
    tm = T * B
    body = functools.partial(_wkv_chunk_body, B=B, T=T)
    st_shape = (B, RWKV_PAIRS, LANES, LANES)
    return pl.pallas_call(
        body,
        grid=(n // tm,),
        in_specs=[_row_spec(tm, D_MODEL)] * 7 + [_const_spec(st_shape)],
        out_specs=[_row_spec(tm, D_MODEL), _const_spec(st_shape)],
        out_shape=[jax.ShapeDtypeStruct((n, D_MODEL), F32), jax.ShapeDtypeStruct(st_shape, F32)],
        scratch_shapes=[pltpu.VMEM(st_shape, F32)],
        compiler_params=_cparams(),
        name="wkv_chunked",
    )(r, cum, lw, k, v, kk, bvec, s0)


def _wkv_step_body(s_ref, r_ref, lw_ref, k_ref, kk_ref, b_ref, v_ref, o_ref, sn_ref):
    s = s_ref[...]
    sa = -jnp.sum(s * kk_ref[...], axis=-1, keepdims=True)
    sn = s * jnp.exp(lw_ref[...]) + sa * b_ref[...] + v_ref[...] * k_ref[...]
    sn_ref[...] = sn
    o_ref[...] = jnp.sum(sn * r_ref[...], axis=-1, keepdims=True)


def _wkv_step(s0, r, lw, k, kk, bvec, v, *, bb=8):
    bsz = s0.shape[0]
    rowv = lambda t: t.reshape(bsz, RWKV_HEADS, 1, RWKV_HEAD)
    colv = lambda t: t.reshape(bsz, RWKV_HEADS, RWKV_HEAD, 1)
    s_spec = pl.BlockSpec((bb, RWKV_HEADS, RWKV_HEAD, RWKV_HEAD), lambda i: (i, 0, 0, 0))
    r_spec = pl.BlockSpec((bb, RWKV_HEADS, 1, RWKV_HEAD), lambda i: (i, 0, 0, 0))
    c_spec = pl.BlockSpec((bb, RWKV_HEADS, RWKV_HEAD, 1), lambda i: (i, 0, 0, 0))
    o, sn = pl.pallas_call(
        _wkv_step_body,
        grid=(bsz // bb,),
        in_specs=[s_spec] + [r_spec] * 5 + [c_spec],
        out_specs=[c_spec, s_spec],
        out_shape=[jax.ShapeDtypeStruct((bsz, RWKV_HEADS, RWKV_HEAD, 1), F32),
                   jax.ShapeDtypeStruct(s0.shape, F32)],
        compiler_params=_cparams(),
        name="wkv_step",
    )(s0, rowv(r), rowv(lw), rowv(k), rowv(kk), rowv(bvec), colv(v))
    return o.reshape(bsz, D_MODEL), sn


def _rw_params(p):
    row = lambda a: a.reshape(1, D_MODEL)
    pad_c = lambda w: jnp.pad(w, ((0, 0), (0, LORA_PAD - w.shape[1]))).astype(BF16)
    pad_r = lambda w: jnp.pad(w, ((0, LORA_PAD - w.shape[0]), (0, 0))).astype(BF16)
    head = jnp.arange(D_MODEL) // RWKV_HEAD
    ones = (head[:, None] == head[None, :]).astype(BF16)
    return dict(mu=p['rw_mu'], wrkv=p['rw_w_rkv'].astype(BF16), w0=row(p['rw_w0']),
                w1=pad_c(p['rw_w1']), w2=pad_r(p['rw_w2']), a0=row(p['rw_a0']),
                a1=pad_c(p['rw_a1']), a2=pad_r(p['rw_a2']), g1=pad_c(p['rw_g1']),
                g2=pad_r(p['rw_g2']), k_k=row(p['rw_k_k']), k_a=row(p['rw_k_a']),
                r_k=row(p['rw_r_k']), ln_w=row(p['rw_ln_w']), ln_b=row(p['rw_ln_b']),
                wo=p['rw_w_o'].astype(BF16), ones=ones)


def _pair_states(s):
    bsz = s.shape[0]
    s5 = s.reshape(bsz, RWKV_PAIRS, 2, RWKV_HEAD, RWKV_HEAD)
    eye = jnp.eye(2, dtype=s.dtype)
    bd = jnp.einsum('bpqij,qr->bpqirj', s5, eye)
    return bd.reshape(bsz, RWKV_PAIRS, LANES, LANES)


def _unpair_states(bd):
    bsz = bd.shape[0]
    s6 = bd.reshape(bsz, RWKV_PAIRS, 2, RWKV_HEAD, 2, RWKV_HEAD)
    s = jnp.stack([s6[:, :, 0, :, 0, :], s6[:, :, 1, :, 1, :]], axis=2)
    return s.reshape(bsz, RWKV_HEADS, RWKV_HEAD, RWKV_HEAD)


def _rwkv(x, g, prm, sh0, s0, *, B, L):
    T = min(L, WKV_CHUNK)
    r, cum, lw, k, v, kk, bvec, gt, sh_new = _rw_pre(x, g, prm, sh0, B=B, T=T)
    if L == 1:
        o, s_new = _wkv_step(s0, r, lw, k, kk, bvec, v)
    else:
        o, s_bd = _wkv_chunk(r, cum, lw, k, v, kk, bvec, _pair_states(s0), B=B, T=T)
        s_new = _unpair_states(s_bd)
    y = _rw_post(x, o, r, k, v, gt, prm, B=B, T=T)
    return y, sh_new, s_new


def _trunk(x, st, w, *, B, L):
    TM = B * min(L, WKV_CHUNK)
    row = lambda a: a.reshape(1, -1)
    new = {k: [] for k in ('s5_re', 's5_im', 'rw_wkv', 'rw_shift', 'lru_h', 'lru_conv', 'ffn_conv')}
    for i in range(DEPTH):
        kind, j = i % N_MIXERS, i // N_MIXERS
        g = row(w['norm_mix'][i])
        if kind == 0:
            x, hr, hi = _s5(x, g, w['s5'][j], st['s5_re'][j], st['s5_im'][j], B=B, TM=TM)
            new['s5_re'].append(hr)
            new['s5_im'].append(hi)
        elif kind == 1:
            x, sh, s = _rwkv(x, g, w['rw'][j], st['rw_shift'][j], st['rw_wkv'][j], B=B, L=L)
            new['rw_shift'].append(sh)
            new['rw_wkv'].append(s)
        else:
            x, cb, hl = _lru(x, g, w['lru'][j], st['lru_conv'][j], st['lru_h'][j], B=B, TM=TM)
            new['lru_conv'].append(cb)
            new['lru_h'].append(hl)
        x, cb = _ffn(x, row(w['norm_ffn'][i]), w['ffn_w_in'][i], w['ffn_conv_w'][i],
                     row(w['ffn_conv_b'][i]), w['ffn_w_out'][i], st['ffn_conv'][i],
                     row(w['norm_final']), B=B, TM=TM, final_norm=(i == DEPTH - 1))
        new['ffn_conv'].append(cb)
    return x, new


def _time_major_hist(buf):
    n, bsz, wm1, c = buf.shape
    t = jnp.transpose(buf, (0, 2, 1, 3)).reshape(n, wm1 * bsz, c)
    return [t[j] for j in range(n)]


def _batch_major_hist(rows, bsz):
    t = jnp.stack(rows)
    n, _, c = t.shape
    return jnp.transpose(t.reshape(n, -1, bsz, c), (0, 2, 1, 3))


def _run(x, st, w):
    bsz, length, _ = x.shape
    xt = jnp.transpose(x, (1, 0, 2)).reshape(length * bsz, D_MODEL)
    flat = lambda a: [a[j].reshape(bsz, -1) for j in range(a.shape[0])]
    stt = dict(s5_re=flat(st['s5_re']), s5_im=flat(st['s5_im']),
               rw_wkv=[st['rw_wkv'][j] for j in range(st['rw_wkv'].shape[0])],
               rw_shift=flat(st['rw_shift']), lru_h=flat(st['lru_h']),
               lru_conv=_time_major_hist(st['lru_conv']), ffn_conv=_time_major_hist(st['ffn_conv']))
    y, new = _trunk(xt, stt, w, B=bsz, L=length)
    y = jnp.transpose(y.reshape(length, bsz, D_MODEL), (1, 0, 2))
    n5 = len(new['s5_re'])
    out = dict(
        s5_re=jnp.stack(new['s5_re']).reshape(n5, bsz, S5_GROUPS, S5_STATE),
        s5_im=jnp.stack(new['s5_im']).reshape(n5, bsz, S5_GROUPS, S5_STATE),
        rw_wkv=jnp.stack(new['rw_wkv']), rw_shift=jnp.stack(new['rw_shift']),
        lru_h=jnp.stack(new['lru_h']), lru_conv=_batch_major_hist(new['lru_conv'], bsz),
        ffn_conv=_batch_major_hist(new['ffn_conv'], bsz))
    return y, out


def _prepare_weights(w):
    n_s5 = w['s5_a_re'].shape[0]
    n_rw = w['rw_mu'].shape[0]
    n_lru = w['lru_w_in'].shape[0]
    sub = lambda prefix, j: {k: v[j] for k, v in w.items() if k.startswith(prefix)}
    return dict(
        norm_mix=w['norm_mix'], norm_ffn=w['norm_ffn'], norm_final=w['norm_final'],
        s5=[_s5_params(sub('s5_', j)) for j in range(n_s5)],
        rw=[_rw_params(sub('rw_', j)) for j in range(n_rw)],
        lru=[_lru_params(sub('lru_', j)) for j in range(n_lru)],
        ffn_w_in=w['ffn_w_in'].astype(BF16), ffn_conv_w=w['ffn_conv_w'],
        ffn_conv_b=w['ffn_conv_b'], ffn_w_out=w['ffn_w_out'].astype(BF16))


def kernel(x_prompt, x_sample, state_s5_re, state_s5_im, state_rwkv_wkv, state_rwkv_shift, state_lru_h, state_lru_conv, state_ffn_conv, norm_mix, norm_ffn, norm_final, s5_a_re, s5_a_im, s5_log_dt, s5_b_re, s5_b_im, s5_c_re, s5_c_im, s5_d, s5_w_glu, rw_mu, rw_w_rkv, rw_w0, rw_w1, rw_w2, rw_a0, rw_a1, rw_a2, rw_g1, rw_g2, rw_k_k, rw_k_a, rw_r_k, rw_ln_w, rw_ln_b, rw_w_o, lru_w_in, lru_conv_w, lru_conv_b, lru_w_rg, lru_b_rg, lru_w_ig, lru_b_ig, lru_lambda, lru_w_out, ffn_w_in, ffn_conv_w, ffn_conv_b, ffn_w_out):
    w = _prepare_weights(dict(
        norm_mix=norm_mix, norm_ffn=norm_ffn, norm_final=norm_final,
        s5_a_re=s5_a_re, s5_a_im=s5_a_im, s5_log_dt=s5_log_dt, s5_b_re=s5_b_re, s5_b_im=s5_b_im,
        s5_c_re=s5_c_re, s5_c_im=s5_c_im, s5_d=s5_d, s5_w_glu=s5_w_glu,
        rw_mu=rw_mu, rw_w_rkv=rw_w_rkv, rw_w0=rw_w0, rw_w1=rw_w1, rw_w2=rw_w2, rw_a0=rw_a0,
        rw_a1=rw_a1, rw_a2=rw_a2, rw_g1=rw_g1, rw_g2=rw_g2, rw_k_k=rw_k_k, rw_k_a=rw_k_a,
        rw_r_k=rw_r_k, rw_ln_w=rw_ln_w, rw_ln_b=rw_ln_b, rw_w_o=rw_w_o,
        lru_w_in=lru_w_in, lru_conv_w=lru_conv_w, lru_conv_b=lru_conv_b, lru_w_rg=lru_w_rg,
        lru_b_rg=lru_b_rg, lru_w_ig=lru_w_ig, lru_b_ig=lru_b_ig, lru_lambda=lru_lambda,
        lru_w_out=lru_w_out, ffn_w_in=ffn_w_in, ffn_conv_w=ffn_conv_w, ffn_conv_b=ffn_conv_b,
        ffn_w_out=ffn_w_out))
    bsz, dt = x_prompt.shape[0], x_prompt.dtype
    n_s5, n_rw, n_lru = state_s5_re.shape[0], state_rwkv_wkv.shape[0], state_lru_h.shape[0]
    st_prompt = dict(
        s5_re=jnp.zeros((n_s5, bsz, S5_GROUPS, S5_STATE), dt),
        s5_im=jnp.zeros((n_s5, bsz, S5_GROUPS, S5_STATE), dt),
        rw_wkv=jnp.zeros((n_rw, bsz, RWKV_HEADS, RWKV_HEAD, RWKV_HEAD), dt),
        rw_shift=jnp.zeros((n_rw, bsz, D_MODEL), dt),
        lru_h=jnp.zeros((n_lru, bsz, D_RNN), dt),
        lru_conv=jnp.zeros((n_lru, bsz, LRU_CONV - 1, D_RNN), dt),
        ffn_conv=jnp.zeros((DEPTH, bsz, FFN_CONV - 1, D_FF), dt))
    st_sample = dict(s5_re=state_s5_re, s5_im=state_s5_im, rw_wkv=state_rwkv_wkv,
                     rw_shift=state_rwkv_shift, lru_h=state_lru_h, lru_conv=state_lru_conv,
                     ffn_conv=state_ffn_conv)
    y_p, new_p = _run(x_prompt, st_prompt, w)
    y_s, new_s = _run(x_sample, st_sample, w)
    return (y_p, y_s, new_p['s5_re'], new_s['s5_re'], new_p['s5_im'], new_s['s5_im'],
            new_p['rw_wkv'], new_s['rw_wkv'], new_p['rw_shift'], new_s['rw_shift'],
            new_p['lru_h'], new_s['lru_h'], new_p['lru_conv'], new_s['lru_conv'],
            new_p['ffn_conv'], new_s['ffn_conv'])
```

```python
import functools

import jax
import jax.numpy as jnp
from jax import lax
from jax.experimental import pallas as pl
from jax.experimental.pallas import tpu as pltpu

F32 = jnp.float32
BF16 = jnp.bfloat16

D_MODEL = 1024
DEPTH = 4
N_MIXERS = 3
RMS_EPS = 1e-6

S5_GROUP = 16
S5_GROUPS = D_MODEL // S5_GROUP
S5_STATE = 64
S5_SLABS = 8
S5_SLAB_STATE = (S5_GROUPS // S5_SLABS) * S5_STATE
S5_NSTATE = S5_GROUPS * S5_STATE

RWKV_HEAD = 64
RWKV_HEADS = D_MODEL // RWKV_HEAD
RWKV_PAIRS = RWKV_HEADS // 2
RWKV_GN_EPS = 64e-5
LORA_PAD = 128

D_RNN = D_MODEL
LRU_BLOCKS = 4
LRU_BLOCK = D_RNN // LRU_BLOCKS
LRU_C = 8.0
LRU_CONV = 4

D_FF = 2816
FFN_CONV = 3
FFN_CHUNK = 256
FFN_NCHUNK = D_FF // FFN_CHUNK

LANES = 128
WKV_CHUNK = 64
VMEM_LIMIT = 60 * 1024 * 1024


def _cparams():
    return pltpu.CompilerParams(dimension_semantics=("arbitrary",), vmem_limit_bytes=VMEM_LIMIT)


def _const_spec(shape):
    nd = len(shape)
    return pl.BlockSpec(shape, lambda i, _n=nd: (0,) * _n, pipeline_mode=pl.Buffered(1))


def _row_spec(tm, width):
    return pl.BlockSpec((tm, width), lambda i: (i, 0))


def _rms(x, g):
    ms = jnp.mean(x * x, axis=-1, keepdims=True)
    return x * lax.rsqrt(ms + RMS_EPS) * g


def _bdot(a, w):
    return jnp.dot(a.astype(BF16), w, preferred_element_type=F32)


def _softplus(z):
    return jnp.maximum(z, 0.0) + jnp.log1p(jnp.exp(-jnp.abs(z)))


def _expm1(x):
    u = jnp.exp(x)
    near = u == 1.0
    ratio = (u - 1.0) * x / jnp.where(near, 1.0, jnp.log(u))
    return jnp.where(near, x, jnp.where(jnp.abs(x) < 0.5, ratio, u - 1.0))


def _head_sum(x, ones_bd):
    return jnp.dot(x.astype(BF16), ones_bd, preferred_element_type=F32)


def _load_rows(x_ref, stage, *, B, TM):
    if not stage:
        return x_ref[...]
    T = TM // B
    for j in range(D_MODEL // LANES):
        for b in range(B):
            stage[0][j, pl.ds(b, T, stride=B), :] = x_ref[b, :, j * LANES:(j + 1) * LANES]
    return jnp.concatenate([stage[0][j] for j in range(D_MODEL // LANES)], axis=1)


def _store_rows(o_ref, stage, y, *, B, TM):
    if not stage:
        o_ref[...] = y
        return
    T = TM // B
    for j in range(D_MODEL // LANES):
        stage[0][j] = y[:, j * LANES:(j + 1) * LANES]
        for b in range(B):
            o_ref[b, :, j * LANES:(j + 1) * LANES] = stage[0][j, pl.ds(b, T, stride=B), :]


def _stage_scratch(tm, batch_major):
    return [pltpu.VMEM((D_MODEL // LANES, tm, LANES), F32)] if batch_major else []


def _rows_or_batch_spec(tm, B, batch_major):
    if batch_major:
        return pl.BlockSpec((B, tm // B, D_MODEL), lambda i: (0, i, 0))
    return _row_spec(tm, D_MODEL)


def _ffn_body(x_ref, g_ref, win_ref, cw_ref, cb_ref, wout_ref, c0_ref, gf_ref,
              o_ref, cnew_ref, carry_ref, *stage, B, TM, final_norm):
    hist = (FFN_CONV - 1) * B

    @pl.when(pl.program_id(0) == 0)
    def _():
        carry_ref[...] = c0_ref[...]

    x = x_ref[...]
    h = _rms(x, g_ref[...]).astype(BF16)
    acc = jnp.zeros((TM, D_MODEL), F32)
    for c in range(FFN_NCHUNK):
        lo = c * FFN_CHUNK
        gate = jnp.dot(h, win_ref[:, lo:lo + FFN_CHUNK], preferred_element_type=F32)
        up = jnp.dot(h, win_ref[:, D_FF + lo:D_FF + lo + FFN_CHUNK], preferred_element_type=F32)
        ext = jnp.concatenate([carry_ref[:, lo:lo + FFN_CHUNK], gate], axis=0)
        conv = cb_ref[:, lo:lo + FFN_CHUNK] + ext[0:TM] * cw_ref[0:1, lo:lo + FFN_CHUNK]
        for k in range(1, FFN_CONV):
            conv = conv + ext[k * B:k * B + TM] * cw_ref[k:k + 1, lo:lo + FFN_CHUNK]
        carry_ref[:, lo:lo + FFN_CHUNK] = ext[TM:TM + hist]
        act = (conv * jax.nn.sigmoid(conv)) * up
        acc = acc + jnp.dot(act.astype(BF16), wout_ref[lo:lo + FFN_CHUNK, :],
                            preferred_element_type=F32)
    y = x + acc
    if final_norm:
        y = _rms(y, gf_ref[...])
    _store_rows(o_ref, stage, y, B=B, TM=TM)
    cnew_ref[...] = carry_ref[...]


def _layer_spec(shape, layer):
    nd = len(shape) - 1
    return pl.BlockSpec((None,) + tuple(shape[1:]), lambda i, _l=layer, _n=nd: (_l,) + (0,) * _n,
                        pipeline_mode=pl.Buffered(1))


def _ffn(x, g, win_all, cw, cb, wout_all, c0, gf, *, layer, B, TM, final_norm, batch_major_out):
    n = x.shape[0]
    out_rows = (jax.ShapeDtypeStruct((B, n // B, D_MODEL), F32) if batch_major_out
                else jax.ShapeDtypeStruct((n, D_MODEL), F32))
    hist = (FFN_CONV - 1) * B
    body = functools.partial(_ffn_body, B=B, TM=TM, final_norm=final_norm)
    return pl.pallas_call(
        body,
        grid=(n // TM,),
        in_specs=[_row_spec(TM, D_MODEL), _const_spec((1, D_MODEL)), _layer_spec(win_all.shape, layer),
                  _const_spec(cw.shape), _const_spec(cb.shape), _layer_spec(wout_all.shape, layer),
                  _const_spec(c0.shape), _const_spec((1, D_MODEL))],
        out_specs=[_rows_or_batch_spec(TM, B, batch_major_out), _const_spec((hist, D_FF))],
        out_shape=[out_rows, jax.ShapeDtypeStruct((hist, D_FF), F32)],
        scratch_shapes=[pltpu.VMEM((hist, D_FF), F32)] + _stage_scratch(TM, batch_major_out),
        compiler_params=_cparams(),
        name="conv_ffn",
    )(x, g, win_all, cw, cb, wout_all, c0, gf)


def _s5_body(x_ref, g_ref, wbr_ref, wbi_ref, cfr_ref, cfi_ref, abr_ref, abi_ref, wcr_ref, wci_ref,
             d_ref, wglu_ref, h0r_ref, h0i_ref, o_ref, hr_out, hi_out,
             xr_s, xi_s, hr_s, hi_s, y_s, *stage, B, TM):
    @pl.when(pl.program_id(0) == 0)
    def _():
        hr_s[...] = h0r_ref[...]
        hi_s[...] = h0i_ref[...]

    x = _load_rows(x_ref, stage, B=B, TM=TM)
    u = _rms(x, g_ref[...])
    ub = u.astype(BF16)
    steps = TM // B
    for s in range(S5_SLABS):
        sl = slice(s * S5_SLAB_STATE, (s + 1) * S5_SLAB_STATE)
        us = ub[:, s * LANES:(s + 1) * LANES]
        bur = jnp.dot(us, wbr_ref[s], preferred_element_type=F32)
        bui = jnp.dot(us, wbi_ref[s], preferred_element_type=F32)
        cr = cfr_ref[:, sl]
        ci = cfi_ref[:, sl]
        xr_s[...] = cr * bur - ci * bui
        xi_s[...] = cr * bui + ci * bur
        ar = jnp.broadcast_to(abr_ref[:, sl], (B, S5_SLAB_STATE))
        ai = jnp.broadcast_to(abi_ref[:, sl], (B, S5_SLAB_STATE))

        def step(t, carry, ar=ar, ai=ai):
            hr, hi = carry
            rows = pl.ds(pl.multiple_of(t * B, B), B)
            nr = (ar * hr - ai * hi) + xr_s[rows, :]
            ni = (ar * hi + ai * hr) + xi_s[rows, :]
            xr_s[rows, :] = nr
            xi_s[rows, :] = ni
            return nr, ni

        hr, hi = lax.fori_loop(0, steps, step, (hr_s[:, sl], hi_s[:, sl]))
        hr_s[:, sl] = hr
        hi_s[:, sl] = hi
        y_s[:, s * LANES:(s + 1) * LANES] = (
            jnp.dot(xr_s[...].astype(BF16), wcr_ref[s], preferred_element_type=F32)
            - jnp.dot(xi_s[...].astype(BF16), wci_ref[s], preferred_element_type=F32))
    y = y_s[...] + d_ref[...] * u
    z = _bdot(jax.nn.gelu(y), wglu_ref[...])
    o_ref[...] = x + z[:, :D_MODEL] * jax.nn.sigmoid(z[:, D_MODEL:])
    hr_out[...] = hr_s[...]
    hi_out[...] = hi_s[...]


def _s5(x, g, prm, h0r, h0i, *, B, TM, batch_major_in):
    n = x.shape[0] * x.shape[1] if batch_major_in else x.shape[0]
    body = functools.partial(_s5_body, B=B, TM=TM)
    consts = [g, prm['wbr'], prm['wbi'], prm['cfr'], prm['cfi'], prm['abr'], prm['abi'],
              prm['wcr'], prm['wci'], prm['d'], prm['wglu'], h0r, h0i]
    return pl.pallas_call(
        body,
        grid=(n // TM,),
        in_specs=([_rows_or_batch_spec(TM, B, batch_major_in)]
                  + [_const_spec(a.shape) for a in consts]),
        out_specs=[_row_spec(TM, D_MODEL), _const_spec((B, S5_NSTATE)), _const_spec((B, S5_NSTATE))],
        out_shape=[jax.ShapeDtypeStruct((n, D_MODEL), F32),
                   jax.ShapeDtypeStruct((B, S5_NSTATE), F32),
                   jax.ShapeDtypeStruct((B, S5_NSTATE), F32)],
        scratch_shapes=[pltpu.VMEM((TM, S5_SLAB_STATE), F32), pltpu.VMEM((TM, S5_SLAB_STATE), F32),
                        pltpu.VMEM((B, S5_NSTATE), F32), pltpu.VMEM((B, S5_NSTATE), F32),
                        pltpu.VMEM((TM, D_MODEL), F32)] + _stage_scratch(TM, batch_major_in),
        compiler_params=_cparams(),
        name="s5_mixer",
    )(x, *consts)


def _s5_params(p):
    lam_re = jnp.minimum(p['s5_a_re'], -1e-4)
    lam_im = p['s5_a_im']
    dt = jnp.exp(p['s5_log_dt'])[:, None]
    mag = jnp.exp(lam_re * dt)
    ab_re = mag * jnp.cos(lam_im * dt)
    ab_im = mag * jnp.sin(lam_im * dt)
    den = lam_re * lam_re + lam_im * lam_im
    coef_re = ((ab_re - 1.0) * lam_re + ab_im * lam_im) / den
    coef_im = (ab_im * lam_re - (ab_re - 1.0) * lam_im) / den
    gps = S5_GROUPS // S5_SLABS
    eye = jnp.eye(gps, dtype=F32)

    def slab_in(b):
        b4 = b.reshape(S5_SLABS, gps, S5_STATE, S5_GROUP)
        w = jnp.einsum('sgpc,gh->sgchp', b4, eye)
        return w.reshape(S5_SLABS, gps * S5_GROUP, gps * S5_STATE).astype(BF16)

    def slab_out(c):
        c4 = c.reshape(S5_SLABS, gps, S5_GROUP, S5_STATE)
        w = jnp.einsum('sgcp,gh->sgphc', c4, eye)
        return w.reshape(S5_SLABS, gps * S5_STATE, gps * S5_GROUP).astype(BF16)

    flat = lambda a: a.reshape(1, S5_NSTATE)
    return dict(wbr=slab_in(p['s5_b_re']), wbi=slab_in(p['s5_b_im']),
                wcr=slab_out(p['s5_c_re']), wci=slab_out(p['s5_c_im']),
                cfr=flat(coef_re), cfi=flat(coef_im), abr=flat(ab_re), abi=flat(ab_im),
                d=p['s5_d'].reshape(1, D_MODEL), wglu=p['s5_w_glu'].astype(BF16))


def _lru_body(x_ref, g_ref, win_ref, cw_ref, cb_ref, wrg_ref, brg_ref, wig_ref, big_ref, lam_ref,
              wout_ref, c0_ref, h0_ref, o_ref, cnew_ref, hnew_ref,
              carry_s, h_s, a_s, bx_s, *, B, TM):
    hist = (LRU_CONV - 1) * B

    @pl.when(pl.program_id(0) == 0)
    def _():
        carry_s[...] = c0_ref[...]
        h_s[...] = h0_ref[...]

    x = x_ref[...]
    xn = _rms(x, g_ref[...])
    gy = _bdot(xn, win_ref[...])
    gate_br = jax.nn.gelu(gy[:, :D_RNN])
    ext = jnp.concatenate([carry_s[...], gy[:, D_RNN:]], axis=0)
    u = cb_ref[...] + ext[0:TM] * cw_ref[0:1, :]
    for k in range(1, LRU_CONV):
        u = u + ext[k * B:k * B + TM] * cw_ref[k:k + 1, :]
    carry_s[...] = ext[TM:TM + hist]
    ub = u.astype(BF16)
    rg_parts, ig_parts = [], []
    for nb in range(LRU_BLOCKS):
        blk = ub[:, nb * LRU_BLOCK:(nb + 1) * LRU_BLOCK]
        rg_parts.append(jnp.dot(blk, wrg_ref[nb], preferred_element_type=F32))
        ig_parts.append(jnp.dot(blk, wig_ref[nb], preferred_element_type=F32))
    rg = jax.nn.sigmoid(jnp.concatenate(rg_parts, axis=1) + brg_ref[...])
    ig = jax.nn.sigmoid(jnp.concatenate(ig_parts, axis=1) + big_ref[...])
    log_sig = -_softplus(-lam_ref[...])
    log_a = LRU_C * rg * log_sig
    a_s[...] = jnp.exp(log_a)
    bx_s[...] = jnp.sqrt(-_expm1(2.0 * log_a)) * ig * u

    def step(t, h):
        rows = pl.ds(pl.multiple_of(t * B, B), B)
        h = a_s[rows, :] * h + bx_s[rows, :]
        bx_s[rows, :] = h
        return h

    h_last = lax.fori_loop(0, TM // B, step, h_s[...])
    h_s[...] = h_last
    o_ref[...] = x + _bdot(bx_s[...] * gate_br, wout_ref[...])
    cnew_ref[...] = carry_s[...]
    hnew_ref[...] = h_last


def _lru(x, g, prm, c0, h0, *, B, TM):
    n = x.shape[0]
    hist = (LRU_CONV - 1) * B
    body = functools.partial(_lru_body, B=B, TM=TM)
    consts = [g, prm['win'], prm['cw'], prm['cb'], prm['wrg'], prm['brg'], prm['wig'], prm['big'],
              prm['lam'], prm['wout'], c0, h0]
    return pl.pallas_call(
        body,
        grid=(n // TM,),
        in_specs=[_row_spec(TM, D_MODEL)] + [_const_spec(a.shape) for a in consts],
        out_specs=[_row_spec(TM, D_MODEL), _const_spec((hist, D_RNN)), _const_spec((B, D_RNN))],
        out_shape=[jax.ShapeDtypeStruct((n, D_MODEL), F32),
                   jax.ShapeDtypeStruct((hist, D_RNN), F32),
                   jax.ShapeDtypeStruct((B, D_RNN), F32)],
        scratch_shapes=[pltpu.VMEM((hist, D_RNN), F32), pltpu.VMEM((B, D_RNN), F32),
                        pltpu.VMEM((TM, D_RNN), F32), pltpu.VMEM((TM, D_RNN), F32)],
        compiler_params=_cparams(),
        name="rglru_mixer",
    )(x, *consts)


def _lru_params(p):
    row = lambda a: a.reshape(1, D_RNN)
    return dict(win=p['lru_w_in'].astype(BF16), cw=p['lru_conv_w'], cb=row(p['lru_conv_b']),
                wrg=p['lru_w_rg'].astype(BF16), brg=row(p['lru_b_rg']),
                wig=p['lru_w_ig'].astype(BF16), big=row(p['lru_b_ig']),
                lam=row(p['lru_lambda']), wout=p['lru_w_out'].astype(BF16))


def _rw_pre_body(x_ref, g_ref, mu_ref, wrkv_ref, w0_ref, w1_ref, w2_ref, a0_ref, a1_ref, a2_ref,
                 g1_ref, g2_ref, kk_ref, ka_ref, ones_ref, sh0_ref,
                 r_o, cum_o, lw_o, k_o, v_o, kk_o, b_o, g_o, sh_o, sh_s, stage_s, *, B, T):
    TM = T * B

    @pl.when(pl.program_id(0) == 0)
    def _():
        sh_s[...] = sh0_ref[...]

    def emit(o_ref, val):
        if T == 1:
            o_ref[...] = val
        else:
            for j in range(D_MODEL // LANES):
                stage_s[j] = val[:, j * LANES:(j + 1) * LANES]
                for b in range(B):
                    o_ref[b * T:(b + 1) * T, j * LANES:(j + 1) * LANES] = (
                        stage_s[j, pl.ds(b, T, stride=B), :])

    xn = _rms(x_ref[...], g_ref[...])
    if TM > B:
        prev = jnp.concatenate([sh_s[...], xn[:TM - B]], axis=0)
    else:
        prev = sh_s[...]
    sh_s[...] = xn[TM - B:]
    xx = prev - xn
    mix = lambda n: xn + xx * mu_ref[n:n + 1, :]
    emit(r_o, _bdot(mix(0), wrkv_ref[0]))
    emit(v_o, _bdot(mix(2), wrkv_ref[2]))
    emit(g_o, _bdot(jax.nn.sigmoid(_bdot(mix(5), g1_ref[...])), g2_ref[...]))
    wl = w0_ref[...] + _bdot(jnp.tanh(_bdot(mix(3), w1_ref[...])), w2_ref[...])
    lw = -jnp.exp(-_softplus(-wl) - 0.5)
    emit(lw_o, lw)
    cum = lw
    sh = B
    while sh < TM:
        cum = cum + jnp.concatenate([jnp.zeros((sh, D_MODEL), F32), cum[:TM - sh]], axis=0)
        sh *= 2
    emit(cum_o, cum)
    a = jax.nn.sigmoid(a0_ref[...] + _bdot(_bdot(mix(4), a1_ref[...]), a2_ref[...]))
    k = _bdot(mix(1), wrkv_ref[1])
    kk = k * kk_ref[...]
    norm = jnp.sqrt(_head_sum(kk * kk, ones_ref[...]))
    kk = kk / jnp.maximum(norm, 1e-12)
    emit(kk_o, kk)
    emit(b_o, kk * a)
    emit(k_o, k * (1.0 + (a - 1.0) * ka_ref[...]))
    sh_o[...] = sh_s[...]


def _rw_pre(x, g, prm, sh0, *, B, T):
    n = x.shape[0]
    tm = T * B
    body = functools.partial(_rw_pre_body, B=B, T=T)
    consts = [g, prm['mu'], prm['wrkv'], prm['w0'], prm['w1'], prm['w2'], prm['a0'], prm['a1'],
              prm['a2'], prm['g1'], prm['g2'], prm['k_k'], prm['k_a'], prm['ones'], sh0]
    big = jax.ShapeDtypeStruct((n, D_MODEL), F32)
    return pl.pallas_call(
        body,
        grid=(n // tm,),
        in_specs=[_row_spec(tm, D_MODEL)] + [_const_spec(a.shape) for a in consts],
        out_specs=[_row_spec(tm, D_MODEL)] * 8 + [_const_spec((B, D_MODEL))],
        out_shape=[big] * 8 + [jax.ShapeDtypeStruct((B, D_MODEL), F32)],
        scratch_shapes=[pltpu.VMEM((B, D_MODEL), F32), pltpu.VMEM((D_MODEL // LANES, tm, LANES), F32)],
        compiler_params=_cparams(),
        name="rwkv_project",
    )(x, *consts)


def _rw_post_body(x_ref, o_ref, r_ref, k_ref, v_ref, g_ref, rk_ref, lnw_ref, lnb_ref, ones_ref,
                  wo_ref, out_ref, stage_s, *, B, T):
    ones = ones_ref[...]
    o = o_ref[...]
    mean = _head_sum(o, ones) * (1.0 / RWKV_HEAD)
    dlt = o - mean
    var = _head_sum(dlt * dlt, ones) * (1.0 / RWKV_HEAD)
    on = dlt * lax.rsqrt(var + RWKV_GN_EPS) * lnw_ref[...] + lnb_ref[...]
    bonus = _head_sum(r_ref[...] * k_ref[...] * rk_ref[...], ones) * v_ref[...]
    y = _bdot((on + bonus) * g_ref[...], wo_ref[...])
    if T == 1:
        out_ref[...] = x_ref[...] + y
    else:
        for j in range(D_MODEL // LANES):
            cols = slice(j * LANES, (j + 1) * LANES)
            for b in range(B):
                stage_s[j, pl.ds(b, T, stride=B), :] = y[b * T:(b + 1) * T, cols]
            out_ref[:, cols] = x_ref[:, cols] + stage_s[j]


def _rw_post(x, o, r, k, v, g, prm, *, B, T):
    n = x.shape[0]
    tm = T * B
    consts = [prm['r_k'], prm['ln_w'], prm['ln_b'], prm['ones'], prm['wo']]
    return pl.pallas_call(
        functools.partial(_rw_post_body, B=B, T=T),
        grid=(n // tm,),
        in_specs=[_row_spec(tm, D_MODEL)] * 6 + [_const_spec(a.shape) for a in consts],
        out_specs=_row_spec(tm, D_MODEL),
        out_shape=jax.ShapeDtypeStruct((n, D_MODEL), F32),
        scratch_shapes=[pltpu.VMEM((D_MODEL // LANES, tm, LANES), F32)],
        compiler_params=_cparams(),
        name="rwkv_output",
    )(x, o, r, k, v, g, *consts)


def _wkv_chunk_body(r_ref, cum_ref, lw_ref, k_ref, v_ref, kk_ref, b_ref, s0_ref, o_ref, sT_ref,
                    S_s, *, B, T):
    @pl.when(pl.program_id(0) == 0)
    def _():
        S_s[...] = s0_ref[...]

    lane = lax.broadcasted_iota(jnp.int32, (T, LANES), 1)
    trow = lax.broadcasted_iota(jnp.int32, (T, LANES), 0)
    first = lane < RWKV_HEAD
    strict = trow > (lane & (T - 1))
    incl = trow >= (lane & (T - 1))
    ri = lax.broadcasted_iota(jnp.int32, (LANES, LANES), 0)
    ci = lax.broadcasted_iota(jnp.int32, (LANES, LANES), 1)
    same_head = (ri < RWKV_HEAD) == (ci < RWKV_HEAD)

    def stack(y):
        return jnp.concatenate([jnp.where(first, y, 0.0), jnp.where(first, 0.0, y)],
                               axis=0).astype(BF16)

    nt = (((1,), (1,)), ((), ()))
    tn = (((0,), (0,)), ((), ()))
    nsteps = T.bit_length() - 1

    def per_b(b, carry):
        rows = pl.ds(pl.multiple_of(b * T, T), T)
        pairs = range(RWKV_PAIRS)
        cols = [slice(hp * LANES, (hp + 1) * LANES) for hp in pairs]
        s_old = [S_s[b, hp] for hp in pairs]
        ks, vs, bbs, ends, e_ends, ps, sps = [], [], [], [], [], [], []
        for hp in pairs:
            ls = cols[hp]
            cm = cum_ref[rows, ls]
            k = k_ref[rows, ls]
            bb = b_ref[rows, ls]
            mid = cm[T // 2 - 1:T // 2, :]
            end = cm[T - 1:T, :]
            at = -kk_ref[rows, ls] * jnp.exp((cm - lw_ref[rows, ls]) - mid)
            rt = r_ref[rows, ls] * jnp.exp(cm - mid)
            e_neg = jnp.exp(mid - cm)
            lhs_f = jnp.concatenate([at, rt], axis=0)
            rhs = jnp.concatenate([stack(bb * e_neg), stack(k * e_neg)], axis=0)
            ps.append(lax.dot_general(lhs_f.astype(BF16), rhs, nt, preferred_element_type=F32))
            sps.append(lax.dot_general((lhs_f * jnp.exp(mid)).astype(BF16), s_old[hp].astype(BF16),
                                       nt, preferred_element_type=F32))
            ks.append(k)
            vs.append(v_ref[rows, ls])
            bbs.append(bb)
            ends.append(end)
            e_ends.append(jnp.exp(end - cm))
        vst = [stack(v) for v in vs]
        ys = [sps[hp][0:T] + jnp.dot(jnp.where(strict, ps[hp][0:T, 2 * T:4 * T], 0.0).astype(BF16),
                                     vst[hp], preferred_element_type=F32) for hp in pairs]
        ms = [jnp.where(strict, ps[hp][0:T, 0:2 * T], 0.0) for hp in pairs]
        for it in range(nsteps):
            ys = [ys[hp] + jnp.dot(ms[hp].astype(BF16), stack(ys[hp]), preferred_element_type=F32)
                  for hp in pairs]
            if it + 1 < nsteps:
                ms = [jnp.dot(ms[hp].astype(BF16), stack(ms[hp]), preferred_element_type=F32)
                      for hp in pairs]
        outs, news = [], []
        for hp in pairs:
            rb = jnp.where(incl, ps[hp][T:2 * T, 0:2 * T], 0.0)
            rk = jnp.where(incl, ps[hp][T:2 * T, 2 * T:4 * T], 0.0)
            outs.append(sps[hp][T:2 * T]
                        + jnp.dot(rb.astype(BF16), stack(ys[hp]), preferred_element_type=F32)
                        + jnp.dot(rk.astype(BF16), vst[hp], preferred_element_type=F32))
            upd = lax.dot_general(
                jnp.concatenate([ys[hp], vs[hp]], axis=0).astype(BF16),
                jnp.concatenate([bbs[hp] * e_ends[hp], ks[hp] * e_ends[hp]], axis=0).astype(BF16),
                tn, preferred_element_type=F32)
            news.append(s_old[hp] * jnp.exp(ends[hp]) + jnp.where(same_head, upd, 0.0))
        for hp in pairs:
            o_ref[rows, cols[hp]] = outs[hp]
            S_s[b, hp] = news[hp]
        return carry

    lax.fori_loop(0, B, per_b, 0)

    @pl.when(pl.program_id(0) == pl.num_programs(0) - 1)
    def _():
        sT_ref[...] = S_s[...]


def _wkv_chunk(r, cum, lw, k, v, kk, bvec, s0, *, B, T):
    n = r.shape[0]
    tm = T * B
    body = functools.partial(_wkv_chunk_body, B=B, T=T)
    st_shape = (B, RWKV_PAIRS, LANES, LANES)
    return pl.pallas_call(
        body,
        grid=(n // tm,),
        in_specs=[_row_spec(tm, D_MODEL)] * 7 + [_const_spec(st_shape)],
        out_specs=[_row_spec(tm, D_MODEL), _const_spec(st_shape)],
        out_shape=[jax.ShapeDtypeStruct((n, D_MODEL), F32), jax.ShapeDtypeStruct(st_shape, F32)],
        scratch_shapes=[pltpu.VMEM(st_shape, F32)],
        compiler_params=_cparams(),
        name="wkv_chunked",
    )(r, cum, lw, k, v, kk, bvec, s0)


def _wkv_step_body(s_ref, r_ref, lw_ref, k_ref, kk_ref, b_ref, v_ref, o_ref, sn_ref):
    s = s_ref[...]
    eye = (lax.broadcasted_iota(jnp.int32, (RWKV_HEAD, RWKV_HEAD), 0)
           == lax.broadcasted_iota(jnp.int32, (RWKV_HEAD, RWKV_HEAD), 1)).astype(F32)
    v_col = jnp.sum(eye * v_ref[...], axis=-1, keepdims=True)
    sa = -jnp.sum(s * kk_ref[...], axis=-1, keepdims=True)
    sn = s * jnp.exp(lw_ref[...]) + sa * b_ref[...] + v_col * k_ref[...]
    sn_ref[...] = sn
    o_col = jnp.sum(sn * r_ref[...], axis=-1, keepdims=True)
    o_ref[...] = jnp.sum(eye * o_col, axis=-2, keepdims=True)


def _wkv_step(s0, r, lw, k, kk, bvec, v, *, bb=8):
    bsz = s0.shape[0]
    rowv = lambda t: t.reshape(bsz, RWKV_HEADS, 1, RWKV_HEAD)
    s_spec = pl.BlockSpec((bb, RWKV_HEADS, RWKV_HEAD, RWKV_HEAD), lambda i: (i, 0, 0, 0))
    r_spec = pl.BlockSpec((bb, RWKV_HEADS, 1, RWKV_HEAD), lambda i: (i, 0, 0, 0))
    o, sn = pl.pallas_call(
        _wkv_step_body,
        grid=(bsz // bb,),
        in_specs=[s_spec] + [r_spec] * 6,
        out_specs=[r_spec, s_spec],
        out_shape=[jax.ShapeDtypeStruct((bsz, RWKV_HEADS, 1, RWKV_HEAD), F32),
                   jax.ShapeDtypeStruct(s0.shape, F32)],
        compiler_params=_cparams(),
        name="wkv_step",
    )(s0, rowv(r), rowv(lw), rowv(k), rowv(kk), rowv(bvec), rowv(v))
    return o.reshape(bsz, D_MODEL), sn


def _rw_params(p):
    row = lambda a: a.reshape(1, D_MODEL)
    pad_c = lambda w: jnp.pad(w, ((0, 0), (0, LORA_PAD - w.shape[1]))).astype(BF16)
    pad_r = lambda w: jnp.pad(w, ((0, LORA_PAD - w.shape[0]), (0, 0))).astype(BF16)
    head = jnp.arange(D_MODEL) // RWKV_HEAD
    ones = (head[:, None] == head[None, :]).astype(BF16)
    return dict(mu=p['rw_mu'], wrkv=p['rw_w_rkv'].astype(BF16), w0=row(p['rw_w0']),
                w1=pad_c(p['rw_w1']), w2=pad_r(p['rw_w2']), a0=row(p['rw_a0']),
                a1=pad_c(p['rw_a1']), a2=pad_r(p['rw_a2']), g1=pad_c(p['rw_g1']),
                g2=pad_r(p['rw_g2']), k_k=row(p['rw_k_k']), k_a=row(p['rw_k_a']),
                r_k=row(p['rw_r_k']), ln_w=row(p['rw_ln_w']), ln_b=row(p['rw_ln_b']),
                wo=p['rw_w_o'].astype(BF16), ones=ones)


def _pair_states(s):
    bsz = s.shape[0]
    s5 = s.reshape(bsz, RWKV_PAIRS, 2, RWKV_HEAD, RWKV_HEAD)
    eye = jnp.eye(2, dtype=s.dtype)
    bd = jnp.einsum('bpqij,qr->bpqirj', s5, eye)
    return bd.reshape(bsz, RWKV_PAIRS, LANES, LANES)


def _unpair_states(bd):
    bsz = bd.shape[0]
    s6 = bd.reshape(bsz, RWKV_PAIRS, 2, RWKV_HEAD, 2, RWKV_HEAD)
    s = jnp.stack([s6[:, :, 0, :, 0, :], s6[:, :, 1, :, 1, :]], axis=2)
    return s.reshape(bsz, RWKV_HEADS, RWKV_HEAD, RWKV_HEAD)


def _rwkv(x, g, prm, sh0, s0, *, B, L):
    T = min(L, WKV_CHUNK)
    r, cum, lw, k, v, kk, bvec, gt, sh_new = _rw_pre(x, g, prm, sh0, B=B, T=T)
    if L == 1:
        o, s_new = _wkv_step(s0, r, lw, k, kk, bvec, v)
    else:
        o, s_bd = _wkv_chunk(r, cum, lw, k, v, kk, bvec, _pair_states(s0), B=B, T=T)
        s_new = _unpair_states(s_bd)
    y = _rw_post(x, o, r, k, v, gt, prm, B=B, T=T)
    return y, sh_new, s_new


def _trunk(x, st, w, *, B, L):
    TM = B * min(L, WKV_CHUNK)
    regroup = L > 1
    if not regroup:
        x = x.reshape(B, D_MODEL)
    row = lambda a: a.reshape(1, -1)
    new = {k: [] for k in ('s5_re', 's5_im', 'rw_wkv', 'rw_shift', 'lru_h', 'lru_conv', 'ffn_conv')}
    for i in range(DEPTH):
        kind, j = i % N_MIXERS, i // N_MIXERS
        g = row(w['norm_mix'][i])
        if kind == 0:
            x, hr, hi = _s5(x, g, w['s5'][j], st['s5_re'][j], st['s5_im'][j], B=B, TM=TM,
                            batch_major_in=(regroup and i == 0))
            new['s5_re'].append(hr)
            new['s5_im'].append(hi)
        elif kind == 1:
            x, sh, s = _rwkv(x, g, w['rw'][j], st['rw_shift'][j], st['rw_wkv'][j], B=B, L=L)
            new['rw_shift'].append(sh)
            new['rw_wkv'].append(s)
        else:
            x, cb, hl = _lru(x, g, w['lru'][j], st['lru_conv'][j], st['lru_h'][j], B=B, TM=TM)
            new['lru_conv'].append(cb)
            new['lru_h'].append(hl)
        x, cb = _ffn(x, row(w['norm_ffn'][i]), w['ffn_w_in'], w['ffn_conv_w'][i],
                     row(w['ffn_conv_b'][i]), w['ffn_w_out'], st['ffn_conv'][i],
                     row(w['norm_final']), layer=i, B=B, TM=TM, final_norm=(i == DEPTH - 1),
                     batch_major_out=(regroup and i == DEPTH - 1))
        new['ffn_conv'].append(cb)
    return x.reshape(B, L, D_MODEL), new


def _stack(parts):
    return parts[0][None] if len(parts) == 1 else jnp.stack(parts)


def _time_major_hist(buf):
    n, bsz, wm1, c = buf.shape
    t = jnp.transpose(buf, (0, 2, 1, 3)).reshape(n, wm1 * bsz, c)
    return [t[j] for j in range(n)]


def _batch_major_hist(rows, bsz):
    t = _stack(rows)
    n, _, c = t.shape
    return jnp.transpose(t.reshape(n, -1, bsz, c), (0, 2, 1, 3))


def _run(x, st, w):
    bsz, length, _ = x.shape
    flat = lambda a: [a[j].reshape(bsz, -1) for j in range(a.shape[0])]
    stt = dict(s5_re=flat(st['s5_re']), s5_im=flat(st['s5_im']),
               rw_wkv=[st['rw_wkv'][j] for j in range(st['rw_wkv'].shape[0])],
               rw_shift=flat(st['rw_shift']), lru_h=flat(st['lru_h']),
               lru_conv=_time_major_hist(st['lru_conv']), ffn_conv=_time_major_hist(st['ffn_conv']))
    y, new = _trunk(x, stt, w, B=bsz, L=length)
    n5 = len(new['s5_re'])
    out = dict(
        s5_re=_stack(new['s5_re']).reshape(n5, bsz, S5_GROUPS, S5_STATE),
        s5_im=_stack(new['s5_im']).reshape(n5, bsz, S5_GROUPS, S5_STATE),
        rw_wkv=_stack(new['rw_wkv']), rw_shift=_stack(new['rw_shift']),
        lru_h=_stack(new['lru_h']), lru_conv=_batch_major_hist(new['lru_conv'], bsz),
        ffn_conv=_batch_major_hist(new['ffn_conv'], bsz))
    return y, out


def _prepare_weights(w):
    n_s5 = w['s5_a_re'].shape[0]
    n_rw = w['rw_mu'].shape[0]
    n_lru = w['lru_w_in'].shape[0]
    sub = lambda prefix, j: {k: v[j] for k, v in w.items() if k.startswith(prefix)}
    return dict(
        norm_mix=w['norm_mix'], norm_ffn=w['norm_ffn'], norm_final=w['norm_final'],
        s5=[_s5_params(sub('s5_', j)) for j in range(n_s5)],
        rw=[_rw_params(sub('rw_', j)) for j in range(n_rw)],
        lru=[_lru_params(sub('lru_', j)) for j in range(n_lru)],
        ffn_w_in=w['ffn_w_in'].astype(BF16), ffn_conv_w=w['ffn_conv_w'],
        ffn_conv_b=w['ffn_conv_b'], ffn_w_out=w['ffn_w_out'].astype(BF16))


def kernel(x_prompt, x_sample, state_s5_re, state_s5_im, state_rwkv_wkv, state_rwkv_shift, state_lru_h, state_lru_conv, state_ffn_conv, norm_mix, norm_ffn, norm_final, s5_a_re, s5_a_im, s5_log_dt, s5_b_re, s5_b_im, s5_c_re, s5_c_im, s5_d, s5_w_glu, rw_mu, rw_w_rkv, rw_w0, rw_w1, rw_w2, rw_a0, rw_a1, rw_a2, rw_g1, rw_g2, rw_k_k, rw_k_a, rw_r_k, rw_ln_w, rw_ln_b, rw_w_o, lru_w_in, lru_conv_w, lru_conv_b, lru_w_rg, lru_b_rg, lru_w_ig, lru_b_ig, lru_lambda, lru_w_out, ffn_w_in, ffn_conv_w, ffn_conv_b, ffn_w_out):
    w = _prepare_weights(dict(
        norm_mix=norm_mix, norm_ffn=norm_ffn, norm_final=norm_final,
        s5_a_re=s5_a_re, s5_a_im=s5_a_im, s5_log_dt=s5_log_dt, s5_b_re=s5_b_re, s5_b_im=s5_b_im,
        s5_c_re=s5_c_re, s5_c_im=s5_c_im, s5_d=s5_d, s5_w_glu=s5_w_glu,
        rw_mu=rw_mu, rw_w_rkv=rw_w_rkv, rw_w0=rw_w0, rw_w1=rw_w1, rw_w2=rw_w2, rw_a0=rw_a0,
        rw_a1=rw_a1, rw_a2=rw_a2, rw_g1=rw_g1, rw_g2=rw_g2, rw_k_k=rw_k_k, rw_k_a=rw_k_a,
        rw_r_k=rw_r_k, rw_ln_w=rw_ln_w, rw_ln_b=rw_ln_b, rw_w_o=rw_w_o,
        lru_w_in=lru_w_in, lru_conv_w=lru_conv_w, lru_conv_b=lru_conv_b, lru_w_rg=lru_w_rg,
        lru_b_rg=lru_b_rg, lru_w_ig=lru_w_ig, lru_b_ig=lru_b_ig, lru_lambda=lru_lambda,
        lru_w_out=lru_w_out, ffn_w_in=ffn_w_in, ffn_conv_w=ffn_conv_w, ffn_conv_b=ffn_conv_b,
        ffn_w_out=ffn_w_out))
    bsz, dt = x_prompt.shape[0], x_prompt.dtype
    n_s5, n_rw, n_lru = state_s5_re.shape[0], state_rwkv_wkv.shape[0], state_lru_h.shape[0]
    st_prompt = dict(
        s5_re=jnp.zeros((n_s5, bsz, S5_GROUPS, S5_STATE), dt),
        s5_im=jnp.zeros((n_s5, bsz, S5_GROUPS, S5_STATE), dt),
        rw_wkv=jnp.zeros((n_rw, bsz, RWKV_HEADS, RWKV_HEAD, RWKV_HEAD), dt),
        rw_shift=jnp.zeros((n_rw, bsz, D_MODEL), dt),
        lru_h=jnp.zeros((n_lru, bsz, D_RNN), dt),
        lru_conv=jnp.zeros((n_lru, bsz, LRU_CONV - 1, D_RNN), dt),
        ffn_conv=jnp.zeros((DEPTH, bsz, FFN_CONV - 1, D_FF), dt))
    st_sample = dict(s5_re=state_s5_re, s5_im=state_s5_im, rw_wkv=state_rwkv_wkv,
                     rw_shift=state_rwkv_shift, lru_h=state_lru_h, lru_conv=state_lru_conv,
                     ffn_conv=state_ffn_conv)
    y_p, new_p = _run(x_prompt, st_prompt, w)
    y_s, new_s = _run(x_sample, st_sample, w)
    return (y_p, y_s, new_p['s5_re'], new_s['s5_re'], new_p['s5_im'], new_s['s5_im'],
            new_p['rw_wkv'], new_s['rw_wkv'], new_p['rw_shift'], new_s['rw_shift'],
            new_p['lru_h'], new_s['lru_h'], new_p['lru_conv'], new_s['lru_conv'],
            new_p['ffn_conv'], new_s['ffn_conv'])
```

```python
import functools

import jax
import jax.numpy as jnp
from jax import lax
from jax.experimental import pallas as pl
from jax.experimental.pallas import tpu as pltpu

F32 = jnp.float32
BF16 = jnp.bfloat16

D_MODEL = 1024
DEPTH = 4
N_MIXERS = 3
RMS_EPS = 1e-6

S5_GROUP = 16
S5_GROUPS = D_MODEL // S5_GROUP
S5_STATE = 64
S5_SLABS = 8
S5_SLAB_STATE = (S5_GROUPS // S5_SLABS) * S5_STATE
S5_NSTATE = S5_GROUPS * S5_STATE

RWKV_HEAD = 64
RWKV_HEADS = D_MODEL // RWKV_HEAD
RWKV_PAIRS = RWKV_HEADS // 2
RWKV_GN_EPS = 64e-5
LORA_PAD = 128

D_RNN = D_MODEL
LRU_BLOCKS = 4
LRU_BLOCK = D_RNN // LRU_BLOCKS
LRU_C = 8.0
LRU_CONV = 4

D_FF = 2816
FFN_CONV = 3
FFN_CHUNK = 256
FFN_NCHUNK = D_FF // FFN_CHUNK

LANES = 128
WKV_CHUNK = 64
WKV_ROWS = 2
VMEM_LIMIT = 60 * 1024 * 1024


def _cparams():
    return pltpu.CompilerParams(dimension_semantics=("arbitrary",), vmem_limit_bytes=VMEM_LIMIT)


def _const_spec(shape):
    nd = len(shape)
    return pl.BlockSpec(shape, lambda i, _n=nd: (0,) * _n, pipeline_mode=pl.Buffered(1))


def _row_spec(tm, width):
    return pl.BlockSpec((tm, width), lambda i: (i, 0))


def _rms(x, g):
    ms = jnp.mean(x * x, axis=-1, keepdims=True)
    return x * lax.rsqrt(ms + RMS_EPS) * g


def _bdot(a, w):
    return jnp.dot(a.astype(BF16), w, preferred_element_type=F32)


def _softplus(z):
    return jnp.maximum(z, 0.0) + jnp.log1p(jnp.exp(-jnp.abs(z)))


def _head_sum(x, ones_bd):
    return jnp.dot(x.astype(BF16), ones_bd, preferred_element_type=F32)


def _load_rows(x_ref, stage, *, B, TM):
    if not stage:
        return x_ref[...]
    T = TM // B
    for j in range(D_MODEL // LANES):
        for b in range(B):
            stage[0][j, pl.ds(b, T, stride=B), :] = x_ref[b, :, j * LANES:(j + 1) * LANES]
    return jnp.concatenate([stage[0][j] for j in range(D_MODEL // LANES)], axis=1)


def _store_rows(o_ref, stage, y, *, B, TM):
    if not stage:
        o_ref[...] = y
        return
    T = TM // B
    for j in range(D_MODEL // LANES):
        stage[0][j] = y[:, j * LANES:(j + 1) * LANES]
        for b in range(B):
            o_ref[b, :, j * LANES:(j + 1) * LANES] = stage[0][j, pl.ds(b, T, stride=B), :]


def _stage_scratch(tm, batch_major):
    return [pltpu.VMEM((D_MODEL // LANES, tm, LANES), F32)] if batch_major else []


def _rows_or_batch_spec(tm, B, batch_major):
    if batch_major:
        return pl.BlockSpec((B, tm // B, D_MODEL), lambda i: (0, i, 0))
    return _row_spec(tm, D_MODEL)


def _ffn_body(x_ref, g_ref, win_ref, cw_ref, cb_ref, wout_ref, c0_ref, gf_ref,
              o_ref, cnew_ref, carry_ref, *stage, B, TM, final_norm):
    hist = (FFN_CONV - 1) * B

    @pl.when(pl.program_id(0) == 0)
    def _():
        carry_ref[...] = c0_ref[...]

    x = x_ref[...]
    h = _rms(x, g_ref[...]).astype(BF16)
    acc = jnp.zeros((TM, D_MODEL), F32)
    for c in range(FFN_NCHUNK):
        lo = c * FFN_CHUNK
        gate = jnp.dot(h, win_ref[:, lo:lo + FFN_CHUNK], preferred_element_type=F32)
        up = jnp.dot(h, win_ref[:, D_FF + lo:D_FF + lo + FFN_CHUNK], preferred_element_type=F32)
        ext = jnp.concatenate([carry_ref[:, lo:lo + FFN_CHUNK], gate], axis=0)
        conv = cb_ref[:, lo:lo + FFN_CHUNK] + ext[0:TM] * cw_ref[0:1, lo:lo + FFN_CHUNK]
        for k in range(1, FFN_CONV):
            conv = conv + ext[k * B:k * B + TM] * cw_ref[k:k + 1, lo:lo + FFN_CHUNK]
        carry_ref[:, lo:lo + FFN_CHUNK] = ext[TM:TM + hist]
        act = (conv * jax.nn.sigmoid(conv)) * up
        acc = acc + jnp.dot(act.astype(BF16), wout_ref[lo:lo + FFN_CHUNK, :],
                            preferred_element_type=F32)
    y = x + acc
    if final_norm:
        y = _rms(y, gf_ref[...])
    _store_rows(o_ref, stage, y, B=B, TM=TM)
    cnew_ref[...] = carry_ref[...]


def _layer_spec(shape, layer):
    nd = len(shape) - 1
    return pl.BlockSpec((None,) + tuple(shape[1:]), lambda i, _l=layer, _n=nd: (_l,) + (0,) * _n,
                        pipeline_mode=pl.Buffered(1))


def _ffn(x, g, win_all, cw, cb, wout_all, c0, gf, *, layer, B, TM, final_norm, batch_major_out):
    n = x.shape[0]
    out_rows = (jax.ShapeDtypeStruct((B, n // B, D_MODEL), F32) if batch_major_out
                else jax.ShapeDtypeStruct((n, D_MODEL), F32))
    hist = (FFN_CONV - 1) * B
    body = functools.partial(_ffn_body, B=B, TM=TM, final_norm=final_norm)
    return pl.pallas_call(
        body,
        grid=(n // TM,),
        in_specs=[_row_spec(TM, D_MODEL), _const_spec((1, D_MODEL)), _layer_spec(win_all.shape, layer),
                  _const_spec(cw.shape), _const_spec(cb.shape), _layer_spec(wout_all.shape, layer),
                  _const_spec(c0.shape), _const_spec((1, D_MODEL))],
        out_specs=[_rows_or_batch_spec(TM, B, batch_major_out), _const_spec((hist, D_FF))],
        out_shape=[out_rows, jax.ShapeDtypeStruct((hist, D_FF), F32)],
        scratch_shapes=[pltpu.VMEM((hist, D_FF), F32)] + _stage_scratch(TM, batch_major_out),
        compiler_params=_cparams(),
        name="conv_ffn",
    )(x, g, win_all, cw, cb, wout_all, c0, gf)


def _s5_body(x_ref, g_ref, wbr_ref, wbi_ref, cfr_ref, cfi_ref, abr_ref, abi_ref, wcr_ref, wci_ref,
             d_ref, wglu_ref, h0r_ref, h0i_ref, o_ref, hr_out, hi_out,
             xr_s, xi_s, hr_s, hi_s, y_s, *stage, B, TM):
    @pl.when(pl.program_id(0) == 0)
    def _():
        hr_s[...] = h0r_ref[...]
        hi_s[...] = h0i_ref[...]

    x = _load_rows(x_ref, stage, B=B, TM=TM)
    u = _rms(x, g_ref[...])
    ub = u.astype(BF16)
    steps = TM // B
    for s in range(S5_SLABS):
        sl = slice(s * S5_SLAB_STATE, (s + 1) * S5_SLAB_STATE)
        us = ub[:, s * LANES:(s + 1) * LANES]
        bur = jnp.dot(us, wbr_ref[s], preferred_element_type=F32)
        bui = jnp.dot(us, wbi_ref[s], preferred_element_type=F32)
        cr = cfr_ref[:, sl]
        ci = cfi_ref[:, sl]
        xr_s[s] = cr * bur - ci * bui
        xi_s[s] = cr * bui + ci * bur
        ar = jnp.broadcast_to(abr_ref[:, sl], (B, S5_SLAB_STATE))
        ai = jnp.broadcast_to(abi_ref[:, sl], (B, S5_SLAB_STATE))

        hr, hi = hr_s[:, sl], hi_s[:, sl]
        for t in range(steps):
            rows = slice(t * B, (t + 1) * B)
            hr, hi = ((ar * hr - ai * hi) + xr_s[s, rows, :],
                      (ar * hi + ai * hr) + xi_s[s, rows, :])
            xr_s[s, rows, :] = hr
            xi_s[s, rows, :] = hi
        hr_s[:, sl] = hr
        hi_s[:, sl] = hi
        y_s[:, s * LANES:(s + 1) * LANES] = (
            jnp.dot(xr_s[s].astype(BF16), wcr_ref[s], preferred_element_type=F32)
            - jnp.dot(xi_s[s].astype(BF16), wci_ref[s], preferred_element_type=F32))
    y = y_s[...] + d_ref[...] * u
    z = _bdot(jax.nn.gelu(y), wglu_ref[...])
    o_ref[...] = x + z[:, :D_MODEL] * jax.nn.sigmoid(z[:, D_MODEL:])
    hr_out[...] = hr_s[...]
    hi_out[...] = hi_s[...]


def _s5(x, g, prm, h0r, h0i, *, B, TM, batch_major_in):
    n = x.shape[0] * x.shape[1] if batch_major_in else x.shape[0]
    body = functools.partial(_s5_body, B=B, TM=TM)
    consts = [g, prm['wbr'], prm['wbi'], prm['cfr'], prm['cfi'], prm['abr'], prm['abi'],
              prm['wcr'], prm['wci'], prm['d'], prm['wglu'], h0r, h0i]
    return pl.pallas_call(
        body,
        grid=(n // TM,),
        in_specs=([_rows_or_batch_spec(TM, B, batch_major_in)]
                  + [_const_spec(a.shape) for a in consts]),
        out_specs=[_row_spec(TM, D_MODEL), _const_spec((B, S5_NSTATE)), _const_spec((B, S5_NSTATE))],
        out_shape=[jax.ShapeDtypeStruct((n, D_MODEL), F32),
                   jax.ShapeDtypeStruct((B, S5_NSTATE), F32),
                   jax.ShapeDtypeStruct((B, S5_NSTATE), F32)],
        scratch_shapes=[pltpu.VMEM((S5_SLABS, TM, S5_SLAB_STATE), F32),
                        pltpu.VMEM((S5_SLABS, TM, S5_SLAB_STATE), F32),
                        pltpu.VMEM((B, S5_NSTATE), F32), pltpu.VMEM((B, S5_NSTATE), F32),
                        pltpu.VMEM((TM, D_MODEL), F32)] + _stage_scratch(TM, batch_major_in),
        compiler_params=_cparams(),
        name="s5_mixer",
    )(x, *consts)


def _s5_params(p):
    lam_re = jnp.minimum(p['s5_a_re'], -1e-4)
    lam_im = p['s5_a_im']
    dt = jnp.exp(p['s5_log_dt'])[:, None]
    mag = jnp.exp(lam_re * dt)
    ab_re = mag * jnp.cos(lam_im * dt)
    ab_im = mag * jnp.sin(lam_im * dt)
    den = lam_re * lam_re + lam_im * lam_im
    coef_re = ((ab_re - 1.0) * lam_re + ab_im * lam_im) / den
    coef_im = (ab_im * lam_re - (ab_re - 1.0) * lam_im) / den
    gps = S5_GROUPS // S5_SLABS
    eye = jnp.eye(gps, dtype=F32)

    def slab_in(b):
        b4 = b.reshape(S5_SLABS, gps, S5_STATE, S5_GROUP)
        w = jnp.einsum('sgpc,gh->sgchp', b4, eye)
        return w.reshape(S5_SLABS, gps * S5_GROUP, gps * S5_STATE).astype(BF16)

    def slab_out(c):
        c4 = c.reshape(S5_SLABS, gps, S5_GROUP, S5_STATE)
        w = jnp.einsum('sgcp,gh->sgphc', c4, eye)
        return w.reshape(S5_SLABS, gps * S5_STATE, gps * S5_GROUP).astype(BF16)

    flat = lambda a: a.reshape(1, S5_NSTATE)
    return dict(wbr=slab_in(p['s5_b_re']), wbi=slab_in(p['s5_b_im']),
                wcr=slab_out(p['s5_c_re']), wci=slab_out(p['s5_c_im']),
                cfr=flat(coef_re), cfi=flat(coef_im), abr=flat(ab_re), abi=flat(ab_im),
                d=p['s5_d'].reshape(1, D_MODEL), wglu=p['s5_w_glu'].astype(BF16))


def _lru_body(x_ref, g_ref, win_ref, cw_ref, cb_ref, wrg_ref, brg_ref, wig_ref, big_ref, lam_ref,
              wout_ref, c0_ref, h0_ref, o_ref, cnew_ref, hnew_ref,
              carry_s, h_s, a_s, bx_s, *, B, TM):
    hist = (LRU_CONV - 1) * B

    @pl.when(pl.program_id(0) == 0)
    def _():
        carry_s[...] = c0_ref[...]
        h_s[...] = h0_ref[...]

    x = x_ref[...]
    xn = _rms(x, g_ref[...])
    gy = _bdot(xn, win_ref[...])
    gate_br = jax.nn.gelu(gy[:, :D_RNN])
    ext = jnp.concatenate([carry_s[...], gy[:, D_RNN:]], axis=0)
    u = cb_ref[...] + ext[0:TM] * cw_ref[0:1, :]
    for k in range(1, LRU_CONV):
        u = u + ext[k * B:k * B + TM] * cw_ref[k:k + 1, :]
    carry_s[...] = ext[TM:TM + hist]
    ub = u.astype(BF16)
    rg_parts, ig_parts = [], []
    for nb in range(LRU_BLOCKS):
        blk = ub[:, nb * LRU_BLOCK:(nb + 1) * LRU_BLOCK]
        rg_parts.append(jnp.dot(blk, wrg_ref[nb], preferred_element_type=F32))
        ig_parts.append(jnp.dot(blk, wig_ref[nb], preferred_element_type=F32))
    rg = jax.nn.sigmoid(jnp.concatenate(rg_parts, axis=1) + brg_ref[...])
    ig = jax.nn.sigmoid(jnp.concatenate(ig_parts, axis=1) + big_ref[...])
    log_sig = -_softplus(-lam_ref[...])
    log_a = LRU_C * rg * log_sig
    a = jnp.exp(log_a)
    a_s[...] = a
    bx_s[...] = jnp.sqrt(1.0 - a * a) * ig * u

    def step(t, h):
        rows = pl.ds(pl.multiple_of(t * B, B), B)
        h = a_s[rows, :] * h + bx_s[rows, :]
        bx_s[rows, :] = h
        return h

    h_last = lax.fori_loop(0, TM // B, step, h_s[...])
    h_s[...] = h_last
    o_ref[...] = x + _bdot(bx_s[...] * gate_br, wout_ref[...])
    cnew_ref[...] = carry_s[...]
    hnew_ref[...] = h_last


def _lru(x, g, prm, c0, h0, *, B, TM):
    n = x.shape[0]
    hist = (LRU_CONV - 1) * B
    body = functools.partial(_lru_body, B=B, TM=TM)
    consts = [g, prm['win'], prm['cw'], prm['cb'], prm['wrg'], prm['brg'], prm['wig'], prm['big'],
              prm['lam'], prm['wout'], c0, h0]
    return pl.pallas_call(
        body,
        grid=(n // TM,),
        in_specs=[_row_spec(TM, D_MODEL)] + [_const_spec(a.shape) for a in consts],
        out_specs=[_row_spec(TM, D_MODEL), _const_spec((hist, D_RNN)), _const_spec((B, D_RNN))],
        out_shape=[jax.ShapeDtypeStruct((n, D_MODEL), F32),
                   jax.ShapeDtypeStruct((hist, D_RNN), F32),
                   jax.ShapeDtypeStruct((B, D_RNN), F32)],
        scratch_shapes=[pltpu.VMEM((hist, D_RNN), F32), pltpu.VMEM((B, D_RNN), F32),
                        pltpu.VMEM((TM, D_RNN), F32), pltpu.VMEM((TM, D_RNN), F32)],
        compiler_params=_cparams(),
        name="rglru_mixer",
    )(x, *consts)


def _lru_params(p):
    row = lambda a: a.reshape(1, D_RNN)
    return dict(win=p['lru_w_in'].astype(BF16), cw=p['lru_conv_w'], cb=row(p['lru_conv_b']),
                wrg=p['lru_w_rg'].astype(BF16), brg=row(p['lru_b_rg']),
                wig=p['lru_w_ig'].astype(BF16), big=row(p['lru_b_ig']),
                lam=row(p['lru_lambda']), wout=p['lru_w_out'].astype(BF16))


def _rw_pre_body(x_ref, g_ref, mu_ref, wrkv_ref, w0_ref, w1_ref, w2_ref, a0_ref, a1_ref, a2_ref,
                 g1_ref, g2_ref, kk_ref, ka_ref, ones_ref, sh0_ref,
                 r_o, cum_o, lw_o, k_o, v_o, kk_o, b_o, g_o, sh_o, sh_s, stage_s, *, B, T):
    TM = T * B

    @pl.when(pl.program_id(0) == 0)
    def _():
        sh_s[...] = sh0_ref[...]

    def emit(o_ref, val):
        if T == 1:
            o_ref[...] = val
        else:
            for j in range(D_MODEL // LANES):
                stage_s[j] = val[:, j * LANES:(j + 1) * LANES]
                for b in range(B):
                    o_ref[b * T:(b + 1) * T, j * LANES:(j + 1) * LANES] = (
                        stage_s[j, pl.ds(b, T, stride=B), :])

    xn = _rms(x_ref[...], g_ref[...])
    if TM > B:
        prev = jnp.concatenate([sh_s[...], xn[:TM - B]], axis=0)
    else:
        prev = sh_s[...]
    sh_s[...] = xn[TM - B:]
    xx = prev - xn
    mix = lambda n: xn + xx * mu_ref[n:n + 1, :]
    emit(r_o, _bdot(mix(0), wrkv_ref[0]))
    emit(v_o, _bdot(mix(2), wrkv_ref[2]))
    emit(g_o, _bdot(jax.nn.sigmoid(_bdot(mix(5), g1_ref[...])), g2_ref[...]))
    wl = w0_ref[...] + _bdot(jnp.tanh(_bdot(mix(3), w1_ref[...])), w2_ref[...])
    lw = -jnp.exp(-_softplus(-wl) - 0.5)
    emit(lw_o, lw)
    cum = lw
    sh = B
    while sh < TM:
        cum = cum + jnp.concatenate([jnp.zeros((sh, D_MODEL), F32), cum[:TM - sh]], axis=0)
        sh *= 2
    emit(cum_o, cum)
    a = jax.nn.sigmoid(a0_ref[...] + _bdot(_bdot(mix(4), a1_ref[...]), a2_ref[...]))
    k = _bdot(mix(1), wrkv_ref[1])
    kk = k * kk_ref[...]
    norm = jnp.sqrt(_head_sum(kk * kk, ones_ref[...]))
    kk = kk / jnp.maximum(norm, 1e-12)
    emit(kk_o, kk)
    emit(b_o, kk * a)
    emit(k_o, k * (1.0 + (a - 1.0) * ka_ref[...]))
    sh_o[...] = sh_s[...]


def _rw_pre(x, g, prm, sh0, *, B, T):
    n = x.shape[0]
    tm = T * B
    body = functools.partial(_rw_pre_body, B=B, T=T)
    consts = [g, prm['mu'], prm['wrkv'], prm['w0'], prm['w1'], prm['w2'], prm['a0'], prm['a1'],
              prm['a2'], prm['g1'], prm['g2'], prm['k_k'], prm['k_a'], prm['ones'], sh0]
    big = jax.ShapeDtypeStruct((n, D_MODEL), F32)
    return pl.pallas_call(
        body,
        grid=(n // tm,),
        in_specs=[_row_spec(tm, D_MODEL)] + [_const_spec(a.shape) for a in consts],
        out_specs=[_row_spec(tm, D_MODEL)] * 8 + [_const_spec((B, D_MODEL))],
        out_shape=[big] * 8 + [jax.ShapeDtypeStruct((B, D_MODEL), F32)],
        scratch_shapes=[pltpu.VMEM((B, D_MODEL), F32), pltpu.VMEM((D_MODEL // LANES, tm, LANES), F32)],
        compiler_params=_cparams(),
        name="rwkv_project",
    )(x, *consts)


def _rw_post_body(x_ref, o_ref, r_ref, k_ref, v_ref, g_ref, rk_ref, lnw_ref, lnb_ref, ones_ref,
                  wo_ref, out_ref, stage_s, *, B, T):
    ones = ones_ref[...]
    o = o_ref[...]
    mean = _head_sum(o, ones) * (1.0 / RWKV_HEAD)
    dlt = o - mean
    var = _head_sum(dlt * dlt, ones) * (1.0 / RWKV_HEAD)
    on = dlt * lax.rsqrt(var + RWKV_GN_EPS) * lnw_ref[...] + lnb_ref[...]
    bonus = _head_sum(r_ref[...] * k_ref[...] * rk_ref[...], ones) * v_ref[...]
    y = _bdot((on + bonus) * g_ref[...], wo_ref[...])
    if T == 1:
        out_ref[...] = x_ref[...] + y
    else:
        for j in range(D_MODEL // LANES):
            cols = slice(j * LANES, (j + 1) * LANES)
            for b in range(B):
                stage_s[j, pl.ds(b, T, stride=B), :] = y[b * T:(b + 1) * T, cols]
            out_ref[:, cols] = x_ref[:, cols] + stage_s[j]


def _rw_post(x, o, r, k, v, g, prm, *, B, T):
    n = x.shape[0]
    tm = T * B
    consts = [prm['r_k'], prm['ln_w'], prm['ln_b'], prm['ones'], prm['wo']]
    return pl.pallas_call(
        functools.partial(_rw_post_body, B=B, T=T),
        grid=(n // tm,),
        in_specs=[_row_spec(tm, D_MODEL)] * 6 + [_const_spec(a.shape) for a in consts],
        out_specs=_row_spec(tm, D_MODEL),
        out_shape=jax.ShapeDtypeStruct((n, D_MODEL), F32),
        scratch_shapes=[pltpu.VMEM((D_MODEL // LANES, tm, LANES), F32)],
        compiler_params=_cparams(),
        name="rwkv_output",
    )(x, o, r, k, v, g, *consts)


def _wkv_chunk_body(r_ref, cum_ref, lw_ref, k_ref, v_ref, kk_ref, b_ref, s0_ref, o_ref, sT_ref,
                    S_s, *, B, T):
    @pl.when(pl.program_id(0) == 0)
    def _():
        S_s[...] = s0_ref[...]

    lane = lax.broadcasted_iota(jnp.int32, (T, LANES), 1)
    trow = lax.broadcasted_iota(jnp.int32, (T, LANES), 0)
    first = lane < RWKV_HEAD
    strict = trow > (lane & (T - 1))
    incl = trow >= (lane & (T - 1))
    ri = lax.broadcasted_iota(jnp.int32, (LANES, LANES), 0)
    ci = lax.broadcasted_iota(jnp.int32, (LANES, LANES), 1)
    same_head = (ri < RWKV_HEAD) == (ci < RWKV_HEAD)

    def stack(y):
        return jnp.concatenate([jnp.where(first, y, 0.0), jnp.where(first, 0.0, y)],
                               axis=0).astype(BF16)

    nt = (((1,), (1,)), ((), ()))
    tn = (((0,), (0,)), ((), ()))
    nsteps = T.bit_length() - 1

    def per_group(g, carry):
        chains = [(g * WKV_ROWS + i, hp) for i in range(WKV_ROWS) for hp in range(RWKV_PAIRS)]
        idx = range(len(chains))
        rows = [pl.ds(pl.multiple_of(b * T, T), T) for b, _ in chains]
        cols = [slice(hp * LANES, (hp + 1) * LANES) for _, hp in chains]
        s_old = [S_s[b, hp] for b, hp in chains]
        ks, vs, bbs, ends, e_ends, ps, sps = [], [], [], [], [], [], []
        for c in idx:
            at_c = (rows[c], cols[c])
            cm = cum_ref[at_c]
            k = k_ref[at_c]
            bb = b_ref[at_c]
            mid = cm[T // 2 - 1:T // 2, :]
            end = cm[T - 1:T, :]
            at = -kk_ref[at_c] * jnp.exp((cm - lw_ref[at_c]) - mid)
            rt = r_ref[at_c] * jnp.exp(cm - mid)
            e_neg = jnp.exp(mid - cm)
            lhs_f = jnp.concatenate([at, rt], axis=0)
            rhs = jnp.concatenate([stack(bb * e_neg), stack(k * e_neg)], axis=0)
            ps.append(lax.dot_general(lhs_f.astype(BF16), rhs, nt, preferred_element_type=F32))
            sps.append(lax.dot_general((lhs_f * jnp.exp(mid)).astype(BF16), s_old[c].astype(BF16),
                                       nt, preferred_element_type=F32))
            ks.append(k)
            vs.append(v_ref[at_c])
            bbs.append(bb)
            ends.append(end)
            e_ends.append(jnp.exp(end - cm))
        vst = [stack(v) for v in vs]
        ys = [sps[c][0:T] + jnp.dot(jnp.where(strict, ps[c][0:T, 2 * T:4 * T], 0.0).astype(BF16),
                                    vst[c], preferred_element_type=F32) for c in idx]
        ms = [jnp.where(strict, ps[c][0:T, 0:2 * T], 0.0) for c in idx]
        for it in range(nsteps):
            if it + 1 < nsteps:
                both = [jnp.dot(ms[c].astype(BF16),
                                jnp.concatenate([stack(ys[c]), stack(ms[c])], axis=1),
                                preferred_element_type=F32) for c in idx]
                ys = [ys[c] + both[c][:, 0:LANES] for c in idx]
                ms = [both[c][:, LANES:2 * LANES] for c in idx]
            else:
                ys = [ys[c] + jnp.dot(ms[c].astype(BF16), stack(ys[c]), preferred_element_type=F32)
                      for c in idx]
        outs, news = [], []
        for c in idx:
            rbk = jnp.where(jnp.concatenate([incl, incl], axis=1), ps[c][T:2 * T, :], 0.0)
            outs.append(sps[c][T:2 * T]
                        + jnp.dot(rbk.astype(BF16), jnp.concatenate([stack(ys[c]), vst[c]], axis=0),
                                  preferred_element_type=F32))
            upd = lax.dot_general(
                jnp.concatenate([ys[c], vs[c]], axis=0).astype(BF16),
                jnp.concatenate([bbs[c] * e_ends[c], ks[c] * e_ends[c]], axis=0).astype(BF16),
                tn, preferred_element_type=F32)
            news.append(s_old[c] * jnp.exp(ends[c]) + jnp.where(same_head, upd, 0.0))
        for c in idx:
            o_ref[rows[c], cols[c]] = outs[c]
            S_s[chains[c][0], chains[c][1]] = news[c]
        return carry

    lax.fori_loop(0, B // WKV_ROWS, per_group, 0)

    @pl.when(pl.program_id(0) == pl.num_programs(0) - 1)
    def _():
        sT_ref[...] = S_s[...]


def _wkv_chunk(r, cum, lw, k, v, kk, bvec, s0, *, B, T):
    n = r.shape[0]
    tm = T * B
    body = functools.partial(_wkv_chunk_body, B=B, T=T)
    st_shape = (B, RWKV_PAIRS, LANES, LANES)
    return pl.pallas_call(
        body,
        grid=(n // tm,),
        in_specs=[_row_spec(tm, D_MODEL)] * 7 + [_const_spec(st_shape)],
        out_specs=[_row_spec(tm, D_MODEL), _const_spec(st_shape)],
        out_shape=[jax.ShapeDtypeStruct((n, D_MODEL), F32), jax.ShapeDtypeStruct(st_shape, F32)],
        scratch_shapes=[pltpu.VMEM(st_shape, F32)],
        compiler_params=_cparams(),
        name="wkv_chunked",
    )(r, cum, lw, k, v, kk, bvec, s0)


def _wkv_step_body(s_ref, r_ref, lw_ref, k_ref, kk_ref, b_ref, v_ref, o_ref, sn_ref):
    s = s_ref[...]
    eye = (lax.broadcasted_iota(jnp.int32, (RWKV_HEAD, RWKV_HEAD), 0)
           == lax.broadcasted_iota(jnp.int32, (RWKV_HEAD, RWKV_HEAD), 1)).astype(F32)
    v_col = jnp.sum(eye * v_ref[...], axis=-1, keepdims=True)
    sa = -jnp.sum(s * kk_ref[...], axis=-1, keepdims=True)
    sn = s * jnp.exp(lw_ref[...]) + sa * b_ref[...] + v_col * k_ref[...]
    sn_ref[...] = sn
    o_col = jnp.sum(sn * r_ref[...], axis=-1, keepdims=True)
    o_ref[...] = jnp.sum(eye * o_col, axis=-2, keepdims=True)


def _wkv_step(s0, r, lw, k, kk, bvec, v, *, bb=8):
    bsz = s0.shape[0]
    rowv = lambda t: t.reshape(bsz, RWKV_HEADS, 1, RWKV_HEAD)
    s_spec = pl.BlockSpec((bb, RWKV_HEADS, RWKV_HEAD, RWKV_HEAD), lambda i: (i, 0, 0, 0))
    r_spec = pl.BlockSpec((bb, RWKV_HEADS, 1, RWKV_HEAD), lambda i: (i, 0, 0, 0))
    o, sn = pl.pallas_call(
        _wkv_step_body,
        grid=(bsz // bb,),
        in_specs=[s_spec] + [r_spec] * 6,
        out_specs=[r_spec, s_spec],
        out_shape=[jax.ShapeDtypeStruct((bsz, RWKV_HEADS, 1, RWKV_HEAD), F32),
                   jax.ShapeDtypeStruct(s0.shape, F32)],
        compiler_params=_cparams(),
        name="wkv_step",
    )(s0, rowv(r), rowv(lw), rowv(k), rowv(kk), rowv(bvec), rowv(v))
    return o.reshape(bsz, D_MODEL), sn


def _rw_params(p):
    row = lambda a: a.reshape(1, D_MODEL)
    pad_c = lambda w: jnp.pad(w, ((0, 0), (0, LORA_PAD - w.shape[1]))).astype(BF16)
    pad_r = lambda w: jnp.pad(w, ((0, LORA_PAD - w.shape[0]), (0, 0))).astype(BF16)
    head = jnp.arange(D_MODEL) // RWKV_HEAD
    ones = (head[:, None] == head[None, :]).astype(BF16)
    return dict(mu=p['rw_mu'], wrkv=p['rw_w_rkv'].astype(BF16), w0=row(p['rw_w0']),
                w1=pad_c(p['rw_w1']), w2=pad_r(p['rw_w2']), a0=row(p['rw_a0']),
                a1=pad_c(p['rw_a1']), a2=pad_r(p['rw_a2']), g1=pad_c(p['rw_g1']),
                g2=pad_r(p['rw_g2']), k_k=row(p['rw_k_k']), k_a=row(p['rw_k_a']),
                r_k=row(p['rw_r_k']), ln_w=row(p['rw_ln_w']), ln_b=row(p['rw_ln_b']),
                wo=p['rw_w_o'].astype(BF16), ones=ones)


def _pair_states(s):
    bsz = s.shape[0]
    s5 = s.reshape(bsz, RWKV_PAIRS, 2, RWKV_HEAD, RWKV_HEAD)
    eye = jnp.eye(2, dtype=s.dtype)
    bd = jnp.einsum('bpqij,qr->bpqirj', s5, eye)
    return bd.reshape(bsz, RWKV_PAIRS, LANES, LANES)


def _unpair_states(bd):
    bsz = bd.shape[0]
    s6 = bd.reshape(bsz, RWKV_PAIRS, 2, RWKV_HEAD, 2, RWKV_HEAD)
    s = jnp.stack([s6[:, :, 0, :, 0, :], s6[:, :, 1, :, 1, :]], axis=2)
    return s.reshape(bsz, RWKV_HEADS, RWKV_HEAD, RWKV_HEAD)


def _rwkv(x, g, prm, sh0, s0, *, B, L):
    T = min(L, WKV_CHUNK)
    r, cum, lw, k, v, kk, bvec, gt, sh_new = _rw_pre(x, g, prm, sh0, B=B, T=T)
    if L == 1:
        o, s_new = _wkv_step(s0, r, lw, k, kk, bvec, v)
    else:
        o, s_bd = _wkv_chunk(r, cum, lw, k, v, kk, bvec, _pair_states(s0), B=B, T=T)
        s_new = _unpair_states(s_bd)
    y = _rw_post(x, o, r, k, v, gt, prm, B=B, T=T)
    return y, sh_new, s_new


def _trunk(x, st, w, *, B, L):
    TM = B * min(L, WKV_CHUNK)
    regroup = L > 1
    if not regroup:
        x = x.reshape(B, D_MODEL)
    row = lambda a: a.reshape(1, -1)
    new = {k: [] for k in ('s5_re', 's5_im', 'rw_wkv', 'rw_shift', 'lru_h', 'lru_conv', 'ffn_conv')}
    for i in range(DEPTH):
        kind, j = i % N_MIXERS, i // N_MIXERS
        g = row(w['norm_mix'][i])
        if kind == 0:
            x, hr, hi = _s5(x, g, w['s5'][j], st['s5_re'][j], st['s5_im'][j], B=B, TM=TM,
                            batch_major_in=(regroup and i == 0))
            new['s5_re'].append(hr)
            new['s5_im'].append(hi)
        elif kind == 1:
            x, sh, s = _rwkv(x, g, w['rw'][j], st['rw_shift'][j], st['rw_wkv'][j], B=B, L=L)
            new['rw_shift'].append(sh)
            new['rw_wkv'].append(s)
        else:
            x, cb, hl = _lru(x, g, w['lru'][j], st['lru_conv'][j], st['lru_h'][j], B=B, TM=TM)
            new['lru_conv'].append(cb)
            new['lru_h'].append(hl)
        x, cb = _ffn(x, row(w['norm_ffn'][i]), w['ffn_w_in'], w['ffn_conv_w'][i],
                     row(w['ffn_conv_b'][i]), w['ffn_w_out'], st['ffn_conv'][i],
                     row(w['norm_final']), layer=i, B=B, TM=TM, final_norm=(i == DEPTH - 1),
                     batch_major_out=(regroup and i == DEPTH - 1))
        new['ffn_conv'].append(cb)
    return x.reshape(B, L, D_MODEL), new


def _stack(parts):
    return parts[0][None] if len(parts) == 1 else jnp.stack(parts)


def _time_major_hist(buf):
    n, bsz, wm1, c = buf.shape
    t = jnp.transpose(buf, (0, 2, 1, 3)).reshape(n, wm1 * bsz, c)
    return [t[j] for j in range(n)]


def _batch_major_hist(rows, bsz):
    t = _stack(rows)
    n, _, c = t.shape
    return jnp.transpose(t.reshape(n, -1, bsz, c), (0, 2, 1, 3))


def _run(x, st, w):
    bsz, length, _ = x.shape
    flat = lambda a: [a[j].reshape(bsz, -1) for j in range(a.shape[0])]
    stt = dict(s5_re=flat(st['s5_re']), s5_im=flat(st['s5_im']),
               rw_wkv=[st['rw_wkv'][j] for j in range(st['rw_wkv'].shape[0])],
               rw_shift=flat(st['rw_shift']), lru_h=flat(st['lru_h']),
               lru_conv=_time_major_hist(st['lru_conv']), ffn_conv=_time_major_hist(st['ffn_conv']))
    y, new = _trunk(x, stt, w, B=bsz, L=length)
    n5 = len(new['s5_re'])
    out = dict(
        s5_re=_stack(new['s5_re']).reshape(n5, bsz, S5_GROUPS, S5_STATE),
        s5_im=_stack(new['s5_im']).reshape(n5, bsz, S5_GROUPS, S5_STATE),
        rw_wkv=_stack(new['rw_wkv']), rw_shift=_stack(new['rw_shift']),
        lru_h=_stack(new['lru_h']), lru_conv=_batch_major_hist(new['lru_conv'], bsz),
        ffn_conv=_batch_major_hist(new['ffn_conv'], bsz))
    return y, out


def _prepare_weights(w):
    n_s5 = w['s5_a_re'].shape[0]
    n_rw = w['rw_mu'].shape[0]
    n_lru = w['lru_w_in'].shape[0]
    sub = lambda prefix, j: {k: v[j] for k, v in w.items() if k.startswith(prefix)}
    return dict(
        norm_mix=w['norm_mix'], norm_ffn=w['norm_ffn'], norm_final=w['norm_final'],
        s5=[_s5_params(sub('s5_', j)) for j in range(n_s5)],
        rw=[_rw_params(sub('rw_', j)) for j in range(n_rw)],
        lru=[_lru_params(sub('lru_', j)) for j in range(n_lru)],
        ffn_w_in=w['ffn_w_in'].astype(BF16), ffn_conv_w=w['ffn_conv_w'],
        ffn_conv_b=w['ffn_conv_b'], ffn_w_out=w['ffn_w_out'].astype(BF16))


def kernel(x_prompt, x_sample, state_s5_re, state_s5_im, state_rwkv_wkv, state_rwkv_shift, state_lru_h, state_lru_conv, state_ffn_conv, norm_mix, norm_ffn, norm_final, s5_a_re, s5_a_im, s5_log_dt, s5_b_re, s5_b_im, s5_c_re, s5_c_im, s5_d, s5_w_glu, rw_mu, rw_w_rkv, rw_w0, rw_w1, rw_w2, rw_a0, rw_a1, rw_a2, rw_g1, rw_g2, rw_k_k, rw_k_a, rw_r_k, rw_ln_w, rw_ln_b, rw_w_o, lru_w_in, lru_conv_w, lru_conv_b, lru_w_rg, lru_b_rg, lru_w_ig, lru_b_ig, lru_lambda, lru_w_out, ffn_w_in, ffn_conv_w, ffn_conv_b, ffn_w_out):
    w = _prepare_weights(dict(
        norm_mix=norm_mix, norm_ffn=norm_ffn, norm_final=norm_final,
        s5_a_re=s5_a_re, s5_a_im=s5_a_im, s5_log_dt=s5_log_dt, s5_b_re=s5_b_re, s5_b_im=s5_b_im,
        s5_c_re=s5_c_re, s5_c_im=s5_c_im, s5_d=s5_d, s5_w_glu=s5_w_glu,
        rw_mu=rw_mu, rw_w_rkv=rw_w_rkv, rw_w0=rw_w0, rw_w1=rw_w1, rw_w2=rw_w2, rw_a0=rw_a0,
        rw_a1=rw_a1, rw_a2=rw_a2, rw_g1=rw_g1, rw_g2=rw_g2, rw_k_k=rw_k_k, rw_k_a=rw_k_a,
        rw_r_k=rw_r_k, rw_ln_w=rw_ln_w, rw_ln_b=rw_ln_b, rw_w_o=rw_w_o,
        lru_w_in=lru_w_in, lru_conv_w=lru_conv_w, lru_conv_b=lru_conv_b, lru_w_rg=lru_w_rg,
        lru_b_rg=lru_b_rg, lru_w_ig=lru_w_ig, lru_b_ig=lru_b_ig, lru_lambda=lru_lambda,
        lru_w_out=lru_w_out, ffn_w_in=ffn_w_in, ffn_conv_w=ffn_conv_w, ffn_conv_b=ffn_conv_b,
        ffn_w_out=ffn_w_out))
    bsz, dt = x_prompt.shape[0], x_prompt.dtype
    n_s5, n_rw, n_lru = state_s5_re.shape[0], state_rwkv_wkv.shape[0], state_lru_h.shape[0]
    st_prompt = dict(
        s5_re=jnp.zeros((n_s5, bsz, S5_GROUPS, S5_STATE), dt),
        s5_im=jnp.zeros((n_s5, bsz, S5_GROUPS, S5_STATE), dt),
        rw_wkv=jnp.zeros((n_rw, bsz, RWKV_HEADS, RWKV_HEAD, RWKV_HEAD), dt),
        rw_shift=jnp.zeros((n_rw, bsz, D_MODEL), dt),
        lru_h=jnp.zeros((n_lru, bsz, D_RNN), dt),
        lru_conv=jnp.zeros((n_lru, bsz, LRU_CONV - 1, D_RNN), dt),
        ffn_conv=jnp.zeros((DEPTH, bsz, FFN_CONV - 1, D_FF), dt))
    st_sample = dict(s5_re=state_s5_re, s5_im=state_s5_im, rw_wkv=state_rwkv_wkv,
                     rw_shift=state_rwkv_shift, lru_h=state_lru_h, lru_conv=state_lru_conv,
                     ffn_conv=state_ffn_conv)
    y_p, new_p = _run(x_prompt, st_prompt, w)
    y_s, new_s = _run(x_sample, st_sample, w)
    return (y_p, y_s, new_p['s5_re'], new_s['s5_re'], new_p['s5_im'], new_s['s5_im'],
            new_p['rw_wkv'], new_s['rw_wkv'], new_p['rw_shift'], new_s['rw_shift'],
            new_p['lru_h'], new_s['lru_h'], new_p['lru_conv'], new_s['lru_conv'],
            new_p['ffn_conv'], new_s['ffn_conv'])
```

```python
import functools

import jax
import jax.numpy as jnp
from jax import lax
from jax.experimental import pallas as pl
from jax.experimental.pallas import tpu as pltpu

F32 = jnp.float32
BF16 = jnp.bfloat16

D_MODEL = 1024
DEPTH = 4
N_MIXERS = 3
RMS_EPS = 1e-6

S5_GROUP = 16
S5_GROUPS = D_MODEL // S5_GROUP
S5_STATE = 64
S5_SLABS = 8
S5_SLAB_STATE = (S5_GROUPS // S5_SLABS) * S5_STATE
S5_NSTATE = S5_GROUPS * S5_STATE

RWKV_HEAD = 64
RWKV_HEADS = D_MODEL // RWKV_HEAD
RWKV_PAIRS = RWKV_HEADS // 2
RWKV_GN_EPS = 64e-5
LORA_PAD = 128

D_RNN = D_MODEL
LRU_BLOCKS = 4
LRU_BLOCK = D_RNN // LRU_BLOCKS
LRU_C = 8.0
LRU_CONV = 4

D_FF = 2816
FFN_CONV = 3
FFN_CHUNK = 256
FFN_NCHUNK = D_FF // FFN_CHUNK

LANES = 128
WKV_CHUNK = 64
WKV_ROWS = 2
VMEM_LIMIT = 60 * 1024 * 1024


def _cparams():
    return pltpu.CompilerParams(dimension_semantics=("arbitrary",), vmem_limit_bytes=VMEM_LIMIT)


def _const_spec(shape):
    nd = len(shape)
    return pl.BlockSpec(shape, lambda i, _n=nd: (0,) * _n, pipeline_mode=pl.Buffered(1))


def _row_spec(tm, width):
    return pl.BlockSpec((tm, width), lambda i: (i, 0))


def _rms(x, g):
    ms = jnp.mean(x * x, axis=-1, keepdims=True)
    return x * lax.rsqrt(ms + RMS_EPS) * g


def _bdot(a, w):
    return jnp.dot(a.astype(BF16), w, preferred_element_type=F32)


def _softplus(z):
    return jnp.maximum(z, 0.0) + jnp.log1p(jnp.exp(-jnp.abs(z)))


def _head_sum(x, ones_bd):
    return jnp.dot(x.astype(BF16), ones_bd, preferred_element_type=F32)


def _load_rows(x_ref, stage, *, B, TM):
    if not stage:
        return x_ref[...]
    T = TM // B
    for j in range(D_MODEL // LANES):
        for b in range(B):
            stage[0][j, pl.ds(b, T, stride=B), :] = x_ref[b, :, j * LANES:(j + 1) * LANES]
    return jnp.concatenate([stage[0][j] for j in range(D_MODEL // LANES)], axis=1)


def _store_rows(o_ref, stage, y, *, B, TM):
    if not stage:
        o_ref[...] = y
        return
    T = TM // B
    for j in range(D_MODEL // LANES):
        stage[0][j] = y[:, j * LANES:(j + 1) * LANES]
        for b in range(B):
            o_ref[b, :, j * LANES:(j + 1) * LANES] = stage[0][j, pl.ds(b, T, stride=B), :]


def _stage_scratch(tm, batch_major):
    return [pltpu.VMEM((D_MODEL // LANES, tm, LANES), F32)] if batch_major else []


def _rows_or_batch_spec(tm, B, batch_major):
    if batch_major:
        return pl.BlockSpec((B, tm // B, D_MODEL), lambda i: (0, i, 0))
    return _row_spec(tm, D_MODEL)


def _ffn_body(x_ref, g_ref, win_ref, cw_ref, cb_ref, wout_ref, c0_ref, gf_ref,
              o_ref, cnew_ref, carry_ref, *stage, B, TM, final_norm):
    hist = (FFN_CONV - 1) * B

    @pl.when(pl.program_id(0) == 0)
    def _():
        carry_ref[...] = c0_ref[...]

    x = x_ref[...]
    h = _rms(x, g_ref[...]).astype(BF16)
    acc = jnp.zeros((TM, D_MODEL), F32)
    for c in range(FFN_NCHUNK):
        lo = c * FFN_CHUNK
        gate = jnp.dot(h, win_ref[:, lo:lo + FFN_CHUNK], preferred_element_type=F32)
        up = jnp.dot(h, win_ref[:, D_FF + lo:D_FF + lo + FFN_CHUNK], preferred_element_type=F32)
        ext = jnp.concatenate([carry_ref[:, lo:lo + FFN_CHUNK], gate], axis=0)
        conv = cb_ref[:, lo:lo + FFN_CHUNK] + ext[0:TM] * cw_ref[0:1, lo:lo + FFN_CHUNK]
        for k in range(1, FFN_CONV):
            conv = conv + ext[k * B:k * B + TM] * cw_ref[k:k + 1, lo:lo + FFN_CHUNK]
        carry_ref[:, lo:lo + FFN_CHUNK] = ext[TM:TM + hist]
        act = (conv * jax.nn.sigmoid(conv)) * up
        acc = acc + jnp.dot(act.astype(BF16), wout_ref[lo:lo + FFN_CHUNK, :],
                            preferred_element_type=F32)
    y = x + acc
    if final_norm:
        y = _rms(y, gf_ref[...])
    _store_rows(o_ref, stage, y, B=B, TM=TM)
    cnew_ref[...] = carry_ref[...]


def _layer_spec(shape, layer):
    nd = len(shape) - 1
    return pl.BlockSpec((None,) + tuple(shape[1:]), lambda i, _l=layer, _n=nd: (_l,) + (0,) * _n,
                        pipeline_mode=pl.Buffered(1))


def _ffn(x, g, win_all, cw, cb, wout_all, c0, gf, *, layer, B, TM, final_norm, batch_major_out):
    n = x.shape[0]
    out_rows = (jax.ShapeDtypeStruct((B, n // B, D_MODEL), F32) if batch_major_out
                else jax.ShapeDtypeStruct((n, D_MODEL), F32))
    hist = (FFN_CONV - 1) * B
    body = functools.partial(_ffn_body, B=B, TM=TM, final_norm=final_norm)
    return pl.pallas_call(
        body,
        grid=(n // TM,),
        in_specs=[_row_spec(TM, D_MODEL), _const_spec((1, D_MODEL)), _layer_spec(win_all.shape, layer),
                  _const_spec(cw.shape), _const_spec(cb.shape), _layer_spec(wout_all.shape, layer),
                  _const_spec(c0.shape), _const_spec((1, D_MODEL))],
        out_specs=[_rows_or_batch_spec(TM, B, batch_major_out), _const_spec((hist, D_FF))],
        out_shape=[out_rows, jax.ShapeDtypeStruct((hist, D_FF), F32)],
        scratch_shapes=[pltpu.VMEM((hist, D_FF), F32)] + _stage_scratch(TM, batch_major_out),
        compiler_params=_cparams(),
        name="conv_ffn",
    )(x, g, win_all, cw, cb, wout_all, c0, gf)


def _s5_body(x_ref, g_ref, wbr_ref, wbi_ref, cfr_ref, cfi_ref, abr_ref, abi_ref, wcr_ref, wci_ref,
             d_ref, wglu_ref, h0r_ref, h0i_ref, o_ref, hr_out, hi_out,
             xr_s, xi_s, hr_s, hi_s, y_s, *stage, B, TM):
    @pl.when(pl.program_id(0) == 0)
    def _():
        hr_s[...] = h0r_ref[...]
        hi_s[...] = h0i_ref[...]

    x = _load_rows(x_ref, stage, B=B, TM=TM)
    u = _rms(x, g_ref[...])
    ub = u.astype(BF16)
    steps = TM // B
    for s in range(S5_SLABS):
        sl = slice(s * S5_SLAB_STATE, (s + 1) * S5_SLAB_STATE)
        us = ub[:, s * LANES:(s + 1) * LANES]
        bur = jnp.dot(us, wbr_ref[s], preferred_element_type=F32)
        bui = jnp.dot(us, wbi_ref[s], preferred_element_type=F32)
        cr = cfr_ref[:, sl]
        ci = cfi_ref[:, sl]
        xr_s[s] = cr * bur - ci * bui
        xi_s[s] = cr * bui + ci * bur
        ar = jnp.broadcast_to(abr_ref[:, sl], (B, S5_SLAB_STATE))
        ai = jnp.broadcast_to(abi_ref[:, sl], (B, S5_SLAB_STATE))

        hr, hi = hr_s[:, sl], hi_s[:, sl]
        for t in range(steps):
            rows = slice(t * B, (t + 1) * B)
            hr, hi = ((ar * hr - ai * hi) + xr_s[s, rows, :],
                      (ar * hi + ai * hr) + xi_s[s, rows, :])
            xr_s[s, rows, :] = hr
            xi_s[s, rows, :] = hi
        hr_s[:, sl] = hr
        hi_s[:, sl] = hi
        y_s[:, s * LANES:(s + 1) * LANES] = (
            jnp.dot(xr_s[s].astype(BF16), wcr_ref[s], preferred_element_type=F32)
            - jnp.dot(xi_s[s].astype(BF16), wci_ref[s], preferred_element_type=F32))
    y = y_s[...] + d_ref[...] * u
    z = _bdot(jax.nn.gelu(y), wglu_ref[...])
    o_ref[...] = x + z[:, :D_MODEL] * jax.nn.sigmoid(z[:, D_MODEL:])
    hr_out[...] = hr_s[...]
    hi_out[...] = hi_s[...]


def _s5(x, g, prm, h0r, h0i, *, B, TM, batch_major_in):
    n = x.shape[0] * x.shape[1] if batch_major_in else x.shape[0]
    body = functools.partial(_s5_body, B=B, TM=TM)
    consts = [g, prm['wbr'], prm['wbi'], prm['cfr'], prm['cfi'], prm['abr'], prm['abi'],
              prm['wcr'], prm['wci'], prm['d'], prm['wglu'], h0r, h0i]
    return pl.pallas_call(
        body,
        grid=(n // TM,),
        in_specs=([_rows_or_batch_spec(TM, B, batch_major_in)]
                  + [_const_spec(a.shape) for a in consts]),
        out_specs=[_row_spec(TM, D_MODEL), _const_spec((B, S5_NSTATE)), _const_spec((B, S5_NSTATE))],
        out_shape=[jax.ShapeDtypeStruct((n, D_MODEL), F32),
                   jax.ShapeDtypeStruct((B, S5_NSTATE), F32),
                   jax.ShapeDtypeStruct((B, S5_NSTATE), F32)],
        scratch_shapes=[pltpu.VMEM((S5_SLABS, TM, S5_SLAB_STATE), F32),
                        pltpu.VMEM((S5_SLABS, TM, S5_SLAB_STATE), F32),
                        pltpu.VMEM((B, S5_NSTATE), F32), pltpu.VMEM((B, S5_NSTATE), F32),
                        pltpu.VMEM((TM, D_MODEL), F32)] + _stage_scratch(TM, batch_major_in),
        compiler_params=_cparams(),
        name="s5_mixer",
    )(x, *consts)


def _s5_params(p):
    lam_re = jnp.minimum(p['s5_a_re'], -1e-4)
    lam_im = p['s5_a_im']
    dt = jnp.exp(p['s5_log_dt'])[:, None]
    mag = jnp.exp(lam_re * dt)
    ab_re = mag * jnp.cos(lam_im * dt)
    ab_im = mag * jnp.sin(lam_im * dt)
    den = lam_re * lam_re + lam_im * lam_im
    coef_re = ((ab_re - 1.0) * lam_re + ab_im * lam_im) / den
    coef_im = (ab_im * lam_re - (ab_re - 1.0) * lam_im) / den
    gps = S5_GROUPS // S5_SLABS
    eye = jnp.eye(gps, dtype=F32)

    def slab_in(b):
        b4 = b.reshape(S5_SLABS, gps, S5_STATE, S5_GROUP)
        w = jnp.einsum('sgpc,gh->sgchp', b4, eye)
        return w.reshape(S5_SLABS, gps * S5_GROUP, gps * S5_STATE).astype(BF16)

    def slab_out(c):
        c4 = c.reshape(S5_SLABS, gps, S5_GROUP, S5_STATE)
        w = jnp.einsum('sgcp,gh->sgphc', c4, eye)
        return w.reshape(S5_SLABS, gps * S5_STATE, gps * S5_GROUP).astype(BF16)

    flat = lambda a: a.reshape(1, S5_NSTATE)
    return dict(wbr=slab_in(p['s5_b_re']), wbi=slab_in(p['s5_b_im']),
                wcr=slab_out(p['s5_c_re']), wci=slab_out(p['s5_c_im']),
                cfr=flat(coef_re), cfi=flat(coef_im), abr=flat(ab_re), abi=flat(ab_im),
                d=p['s5_d'].reshape(1, D_MODEL), wglu=p['s5_w_glu'].astype(BF16))


def _lru_body(x_ref, g_ref, win_ref, cw_ref, cb_ref, wrg_ref, brg_ref, wig_ref, big_ref, lam_ref,
              wout_ref, c0_ref, h0_ref, o_ref, cnew_ref, hnew_ref,
              carry_s, h_s, a_s, bx_s, *, B, TM):
    hist = (LRU_CONV - 1) * B

    @pl.when(pl.program_id(0) == 0)
    def _():
        carry_s[...] = c0_ref[...]
        h_s[...] = h0_ref[...]

    x = x_ref[...]
    xn = _rms(x, g_ref[...])
    gy = _bdot(xn, win_ref[...])
    gate_br = jax.nn.gelu(gy[:, :D_RNN])
    ext = jnp.concatenate([carry_s[...], gy[:, D_RNN:]], axis=0)
    u = cb_ref[...] + ext[0:TM] * cw_ref[0:1, :]
    for k in range(1, LRU_CONV):
        u = u + ext[k * B:k * B + TM] * cw_ref[k:k + 1, :]
    carry_s[...] = ext[TM:TM + hist]
    ub = u.astype(BF16)
    rg_parts, ig_parts = [], []
    for nb in range(LRU_BLOCKS):
        blk = ub[:, nb * LRU_BLOCK:(nb + 1) * LRU_BLOCK]
        rg_parts.append(jnp.dot(blk, wrg_ref[nb], preferred_element_type=F32))
        ig_parts.append(jnp.dot(blk, wig_ref[nb], preferred_element_type=F32))
    rg = jax.nn.sigmoid(jnp.concatenate(rg_parts, axis=1) + brg_ref[...])
    ig = jax.nn.sigmoid(jnp.concatenate(ig_parts, axis=1) + big_ref[...])
    log_sig = -_softplus(-lam_ref[...])
    log_a = LRU_C * rg * log_sig
    a = jnp.exp(log_a)
    a_s[...] = a
    bx_s[...] = jnp.sqrt(1.0 - a * a) * ig * u

    def step(t, h):
        rows = pl.ds(pl.multiple_of(t * B, B), B)
        h = a_s[rows, :] * h + bx_s[rows, :]
        bx_s[rows, :] = h
        return h

    h_last = lax.fori_loop(0, TM // B, step, h_s[...])
    h_s[...] = h_last
    o_ref[...] = x + _bdot(bx_s[...] * gate_br, wout_ref[...])
    cnew_ref[...] = carry_s[...]
    hnew_ref[...] = h_last


def _lru(x, g, prm, c0, h0, *, B, TM):
    n = x.shape[0]
    hist = (LRU_CONV - 1) * B
    body = functools.partial(_lru_body, B=B, TM=TM)
    consts = [g, prm['win'], prm['cw'], prm['cb'], prm['wrg'], prm['brg'], prm['wig'], prm['big'],
              prm['lam'], prm['wout'], c0, h0]
    return pl.pallas_call(
        body,
        grid=(n // TM,),
        in_specs=[_row_spec(TM, D_MODEL)] + [_const_spec(a.shape) for a in consts],
        out_specs=[_row_spec(TM, D_MODEL), _const_spec((hist, D_RNN)), _const_spec((B, D_RNN))],
        out_shape=[jax.ShapeDtypeStruct((n, D_MODEL), F32),
                   jax.ShapeDtypeStruct((hist, D_RNN), F32),
                   jax.ShapeDtypeStruct((B, D_RNN), F32)],
        scratch_shapes=[pltpu.VMEM((hist, D_RNN), F32), pltpu.VMEM((B, D_RNN), F32),
                        pltpu.VMEM((TM, D_RNN), F32), pltpu.VMEM((TM, D_RNN), F32)],
        compiler_params=_cparams(),
        name="rglru_mixer",
    )(x, *consts)


def _lru_params(p):
    row = lambda a: a.reshape(1, D_RNN)
    return dict(win=p['lru_w_in'].astype(BF16), cw=p['lru_conv_w'], cb=row(p['lru_conv_b']),
                wrg=p['lru_w_rg'].astype(BF16), brg=row(p['lru_b_rg']),
                wig=p['lru_w_ig'].astype(BF16), big=row(p['lru_b_ig']),
                lam=row(p['lru_lambda']), wout=p['lru_w_out'].astype(BF16))


NCOL = D_MODEL // LANES


def _emit_cols(o_ref, val):
    for j in range(NCOL):
        o_ref[j] = val[:, j * LANES:(j + 1) * LANES]


def _load_cols(ref):
    return jnp.concatenate([ref[j] for j in range(NCOL)], axis=1)


def _rw_pre_body(x_ref, g_ref, mu_ref, wrkv_ref, w0_ref, w1_ref, w2_ref, a0_ref, a1_ref, a2_ref,
                 g1_ref, g2_ref, kk_ref, ka_ref, ones_ref, sh0_ref,
                 r_o, cum_o, lw_o, k_o, v_o, kk_o, b_o, g_o, sh_o, sh_s, *, B, T):
    TM = T * B

    @pl.when(pl.program_id(0) == 0)
    def _():
        sh_s[...] = sh0_ref[...]

    def emit(o_ref, val):
        if T == 1:
            o_ref[...] = val
        else:
            _emit_cols(o_ref, val)

    xn = _rms(x_ref[...], g_ref[...])
    if TM > B:
        prev = jnp.concatenate([sh_s[...], xn[:TM - B]], axis=0)
    else:
        prev = sh_s[...]
    sh_s[...] = xn[TM - B:]
    xx = prev - xn
    mix = lambda n: xn + xx * mu_ref[n:n + 1, :]
    emit(r_o, _bdot(mix(0), wrkv_ref[0]))
    emit(v_o, _bdot(mix(2), wrkv_ref[2]))
    emit(g_o, _bdot(jax.nn.sigmoid(_bdot(mix(5), g1_ref[...])), g2_ref[...]))
    wl = w0_ref[...] + _bdot(jnp.tanh(_bdot(mix(3), w1_ref[...])), w2_ref[...])
    lw = -jnp.exp(-_softplus(-wl) - 0.5)
    emit(lw_o, lw)
    cum = lw
    sh = B
    while sh < TM:
        cum = cum + jnp.concatenate([jnp.zeros((sh, D_MODEL), F32), cum[:TM - sh]], axis=0)
        sh *= 2
    emit(cum_o, cum)
    a = jax.nn.sigmoid(a0_ref[...] + _bdot(_bdot(mix(4), a1_ref[...]), a2_ref[...]))
    k = _bdot(mix(1), wrkv_ref[1])
    kk = k * kk_ref[...]
    norm = jnp.sqrt(_head_sum(kk * kk, ones_ref[...]))
    kk = kk / jnp.maximum(norm, 1e-12)
    emit(kk_o, kk)
    emit(b_o, kk * a)
    emit(k_o, k * (1.0 + (a - 1.0) * ka_ref[...]))
    sh_o[...] = sh_s[...]


def _cols_spec(tm):
    return pl.BlockSpec((NCOL, tm, LANES), lambda i: (0, i, 0))


def _rw_pre(x, g, prm, sh0, *, B, T):
    n = x.shape[0]
    tm = T * B
    body = functools.partial(_rw_pre_body, B=B, T=T)
    consts = [g, prm['mu'], prm['wrkv'], prm['w0'], prm['w1'], prm['w2'], prm['a0'], prm['a1'],
              prm['a2'], prm['g1'], prm['g2'], prm['k_k'], prm['k_a'], prm['ones'], sh0]
    if T == 1:
        big, big_spec = jax.ShapeDtypeStruct((n, D_MODEL), F32), _row_spec(tm, D_MODEL)
    else:
        big, big_spec = jax.ShapeDtypeStruct((NCOL, n, LANES), F32), _cols_spec(tm)
    return pl.pallas_call(
        body,
        grid=(n // tm,),
        in_specs=[_row_spec(tm, D_MODEL)] + [_const_spec(a.shape) for a in consts],
        out_specs=[big_spec] * 8 + [_const_spec((B, D_MODEL))],
        out_shape=[big] * 8 + [jax.ShapeDtypeStruct((B, D_MODEL), F32)],
        scratch_shapes=[pltpu.VMEM((B, D_MODEL), F32)],
        compiler_params=_cparams(),
        name="rwkv_project",
    )(x, *consts)


def _rw_post_body(x_ref, o_ref, r_ref, k_ref, v_ref, g_ref, rk_ref, lnw_ref, lnb_ref, ones_ref,
                  wo_ref, out_ref, *, T):
    load = (lambda ref: ref[...]) if T == 1 else _load_cols
    o, r, k, v, g = (load(ref) for ref in (o_ref, r_ref, k_ref, v_ref, g_ref))
    ones = ones_ref[...]
    mean = _head_sum(o, ones) * (1.0 / RWKV_HEAD)
    dlt = o - mean
    var = _head_sum(dlt * dlt, ones) * (1.0 / RWKV_HEAD)
    on = dlt * lax.rsqrt(var + RWKV_GN_EPS) * lnw_ref[...] + lnb_ref[...]
    bonus = _head_sum(r * k * rk_ref[...], ones) * v
    out_ref[...] = x_ref[...] + _bdot((on + bonus) * g, wo_ref[...])


def _rw_post(x, o, r, k, v, g, prm, *, B, T):
    n = x.shape[0]
    tm = T * B
    consts = [prm['r_k'], prm['ln_w'], prm['ln_b'], prm['ones'], prm['wo']]
    big_spec = _row_spec(tm, D_MODEL) if T == 1 else _cols_spec(tm)
    return pl.pallas_call(
        functools.partial(_rw_post_body, T=T),
        grid=(n // tm,),
        in_specs=([_row_spec(tm, D_MODEL)] + [big_spec] * 5
                  + [_const_spec(a.shape) for a in consts]),
        out_specs=_row_spec(tm, D_MODEL),
        out_shape=jax.ShapeDtypeStruct((n, D_MODEL), F32),
        compiler_params=_cparams(),
        name="rwkv_output",
    )(x, o, r, k, v, g, *consts)


def _wkv_chunk_body(r_ref, cum_ref, lw_ref, k_ref, v_ref, kk_ref, b_ref, s0_ref, o_ref, sT_ref,
                    S_s, *, B, T):
    @pl.when(pl.program_id(0) == 0)
    def _():
        S_s[...] = s0_ref[...]

    lane = lax.broadcasted_iota(jnp.int32, (T, LANES), 1)
    trow = lax.broadcasted_iota(jnp.int32, (T, LANES), 0)
    first = lane < RWKV_HEAD
    strict = trow > (lane & (T - 1))
    incl = trow >= (lane & (T - 1))
    ri = lax.broadcasted_iota(jnp.int32, (LANES, LANES), 0)
    ci = lax.broadcasted_iota(jnp.int32, (LANES, LANES), 1)
    same_head = (ri < RWKV_HEAD) == (ci < RWKV_HEAD)

    def stack(y):
        return jnp.concatenate([jnp.where(first, y, 0.0), jnp.where(first, 0.0, y)],
                               axis=0).astype(BF16)

    nt = (((1,), (1,)), ((), ()))
    tn = (((0,), (0,)), ((), ()))
    nsteps = T.bit_length() - 1

    def per_group(g, carry):
        chains = [(g * WKV_ROWS + i, hp) for i in range(WKV_ROWS) for hp in range(RWKV_PAIRS)]
        idx = range(len(chains))
        at_chain = [(hp, pl.ds(b, T, stride=B), slice(None)) for b, hp in chains]
        s_old = [S_s[b, hp] for b, hp in chains]
        ks, vs, bbs, ends, e_ends, ps, sps = [], [], [], [], [], [], []
        for c in idx:
            at_c = at_chain[c]
            cm = cum_ref[at_c]
            k = k_ref[at_c]
            bb = b_ref[at_c]
            mid = cm[T // 2 - 1:T // 2, :]
            end = cm[T - 1:T, :]
            at = -kk_ref[at_c] * jnp.exp((cm - lw_ref[at_c]) - mid)
            rt = r_ref[at_c] * jnp.exp(cm - mid)
            e_neg = jnp.exp(mid - cm)
            lhs_f = jnp.concatenate([at, rt], axis=0)
            rhs = jnp.concatenate([stack(bb * e_neg), stack(k * e_neg)], axis=0)
            ps.append(lax.dot_general(lhs_f.astype(BF16), rhs, nt, preferred_element_type=F32))
            sps.append(lax.dot_general((lhs_f * jnp.exp(mid)).astype(BF16), s_old[c].astype(BF16),
                                       nt, preferred_element_type=F32))
            ks.append(k)
            vs.append(v_ref[at_c])
            bbs.append(bb)
            ends.append(end)
            e_ends.append(jnp.exp(end - cm))
        vst = [stack(v) for v in vs]
        ys = [sps[c][0:T] + jnp.dot(jnp.where(strict, ps[c][0:T, 2 * T:4 * T], 0.0).astype(BF16),
                                    vst[c], preferred_element_type=F32) for c in idx]
        ms = [jnp.where(strict, ps[c][0:T, 0:2 * T], 0.0) for c in idx]
        for it in range(nsteps):
            if it + 1 < nsteps:
                both = [jnp.dot(ms[c].astype(BF16),
                                jnp.concatenate([stack(ys[c]), stack(ms[c])], axis=1),
                                preferred_element_type=F32) for c in idx]
                ys = [ys[c] + both[c][:, 0:LANES] for c in idx]
                ms = [both[c][:, LANES:2 * LANES] for c in idx]
            else:
                ys = [ys[c] + jnp.dot(ms[c].astype(BF16), stack(ys[c]), preferred_element_type=F32)
                      for c in idx]
        outs, news = [], []
        for c in idx:
            rbk = jnp.where(jnp.concatenate([incl, incl], axis=1), ps[c][T:2 * T, :], 0.0)
            outs.append(sps[c][T:2 * T]
                        + jnp.dot(rbk.astype(BF16), jnp.concatenate([stack(ys[c]), vst[c]], axis=0),
                                  preferred_element_type=F32))
            upd = lax.dot_general(
                jnp.concatenate([ys[c], vs[c]], axis=0).astype(BF16),
                jnp.concatenate([bbs[c] * e_ends[c], ks[c] * e_ends[c]], axis=0).astype(BF16),
                tn, preferred_element_type=F32)
            news.append(s_old[c] * jnp.exp(ends[c]) + jnp.where(same_head, upd, 0.0))
        for c in idx:
            o_ref[at_chain[c]] = outs[c]
            S_s[chains[c][0], chains[c][1]] = news[c]
        return carry

    lax.fori_loop(0, B // WKV_ROWS, per_group, 0)

    @pl.when(pl.program_id(0) == pl.num_programs(0) - 1)
    def _():
        sT_ref[...] = S_s[...]


def _wkv_chunk(r, cum, lw, k, v, kk, bvec, s0, *, B, T):
    n = r.shape[1]
    tm = T * B
    body = functools.partial(_wkv_chunk_body, B=B, T=T)
    st_shape = (B, RWKV_PAIRS, LANES, LANES)
    return pl.pallas_call(
        body,
        grid=(n // tm,),
        in_specs=[_cols_spec(tm)] * 7 + [_const_spec(st_shape)],
        out_specs=[_cols_spec(tm), _const_spec(st_shape)],
        out_shape=[jax.ShapeDtypeStruct((NCOL, n, LANES), F32),
                   jax.ShapeDtypeStruct(st_shape, F32)],
        scratch_shapes=[pltpu.VMEM(st_shape, F32)],
        compiler_params=_cparams(),
        name="wkv_chunked",
    )(r, cum, lw, k, v, kk, bvec, s0)


def _wkv_step_body(s_ref, r_ref, lw_ref, k_ref, kk_ref, b_ref, v_ref, o_ref, sn_ref):
    w = jnp.exp(lw_ref[...])
    kk, bv, k, r = kk_ref[...], b_ref[...], k_ref[...], r_ref[...]

    def per_value_row(i, carry):
        s = s_ref[i]
        sa = -jnp.sum(s * kk, axis=0, keepdims=True)
        sn = s * w + sa * bv + v_ref[pl.ds(i, 1), :] * k
        sn_ref[i] = sn
        o_ref[pl.ds(i, 1), :] = jnp.sum(sn * r, axis=0, keepdims=True)
        return carry

    lax.fori_loop(0, RWKV_HEAD, per_value_row, 0)


def _wkv_step(s0, r, lw, k, kk, bvec, v):
    bsz = s0.shape[-1]
    vec = lambda t: t.T.reshape(RWKV_HEADS, RWKV_HEAD, bsz)
    s_spec = pl.BlockSpec((None, RWKV_HEAD, RWKV_HEAD, bsz), lambda h: (h, 0, 0, 0))
    v_spec = pl.BlockSpec((None, RWKV_HEAD, bsz), lambda h: (h, 0, 0))
    o, sn = pl.pallas_call(
        _wkv_step_body,
        grid=(RWKV_HEADS,),
        in_specs=[s_spec] + [v_spec] * 6,
        out_specs=[v_spec, s_spec],
        out_shape=[jax.ShapeDtypeStruct((RWKV_HEADS, RWKV_HEAD, bsz), F32),
                   jax.ShapeDtypeStruct(s0.shape, F32)],
        compiler_params=_cparams(),
        name="wkv_step",
    )(s0, vec(r), vec(lw), vec(k), vec(kk), vec(bvec), vec(v))
    return o.reshape(D_MODEL, bsz).T, sn


def _rw_params(p):
    row = lambda a: a.reshape(1, D_MODEL)
    pad_c = lambda w: jnp.pad(w, ((0, 0), (0, LORA_PAD - w.shape[1]))).astype(BF16)
    pad_r = lambda w: jnp.pad(w, ((0, LORA_PAD - w.shape[0]), (0, 0))).astype(BF16)
    head = jnp.arange(D_MODEL) // RWKV_HEAD
    ones = (head[:, None] == head[None, :]).astype(BF16)
    return dict(mu=p['rw_mu'], wrkv=p['rw_w_rkv'].astype(BF16), w0=row(p['rw_w0']),
                w1=pad_c(p['rw_w1']), w2=pad_r(p['rw_w2']), a0=row(p['rw_a0']),
                a1=pad_c(p['rw_a1']), a2=pad_r(p['rw_a2']), g1=pad_c(p['rw_g1']),
                g2=pad_r(p['rw_g2']), k_k=row(p['rw_k_k']), k_a=row(p['rw_k_a']),
                r_k=row(p['rw_r_k']), ln_w=row(p['rw_ln_w']), ln_b=row(p['rw_ln_b']),
                wo=p['rw_w_o'].astype(BF16), ones=ones)


def _pair_states(s):
    bsz = s.shape[0]
    s5 = s.reshape(bsz, RWKV_PAIRS, 2, RWKV_HEAD, RWKV_HEAD)
    eye = jnp.eye(2, dtype=s.dtype)
    bd = jnp.einsum('bpqij,qr->bpqirj', s5, eye)
    return bd.reshape(bsz, RWKV_PAIRS, LANES, LANES)


def _unpair_states(bd):
    bsz = bd.shape[0]
    s6 = bd.reshape(bsz, RWKV_PAIRS, 2, RWKV_HEAD, 2, RWKV_HEAD)
    s = jnp.stack([s6[:, :, 0, :, 0, :], s6[:, :, 1, :, 1, :]], axis=2)
    return s.reshape(bsz, RWKV_HEADS, RWKV_HEAD, RWKV_HEAD)


def _rwkv(x, g, prm, sh0, s0, *, B, L):
    T = min(L, WKV_CHUNK)
    r, cum, lw, k, v, kk, bvec, gt, sh_new = _rw_pre(x, g, prm, sh0, B=B, T=T)
    if L == 1:
        o, s_t = _wkv_step(jnp.transpose(s0, (1, 2, 3, 0)), r, lw, k, kk, bvec, v)
        s_new = jnp.transpose(s_t, (3, 0, 1, 2))
    else:
        o, s_bd = _wkv_chunk(r, cum, lw, k, v, kk, bvec, _pair_states(s0), B=B, T=T)
        s_new = _unpair_states(s_bd)
    y = _rw_post(x, o, r, k, v, gt, prm, B=B, T=T)
    return y, sh_new, s_new


def _trunk(x, st, w, *, B, L):
    TM = B * min(L, WKV_CHUNK)
    regroup = L > 1
    if not regroup:
        x = x.reshape(B, D_MODEL)
    row = lambda a: a.reshape(1, -1)
    new = {k: [] for k in ('s5_re', 's5_im', 'rw_wkv', 'rw_shift', 'lru_h', 'lru_conv', 'ffn_conv')}
    for i in range(DEPTH):
        kind, j = i % N_MIXERS, i // N_MIXERS
        g = row(w['norm_mix'][i])
        if kind == 0:
            x, hr, hi = _s5(x, g, w['s5'][j], st['s5_re'][j], st['s5_im'][j], B=B, TM=TM,
                            batch_major_in=(regroup and i == 0))
            new['s5_re'].append(hr)
            new['s5_im'].append(hi)
        elif kind == 1:
            x, sh, s = _rwkv(x, g, w['rw'][j], st['rw_shift'][j], st['rw_wkv'][j], B=B, L=L)
            new['rw_shift'].append(sh)
            new['rw_wkv'].append(s)
        else:
            x, cb, hl = _lru(x, g, w['lru'][j], st['lru_conv'][j], st['lru_h'][j], B=B, TM=TM)
            new['lru_conv'].append(cb)
            new['lru_h'].append(hl)
        x, cb = _ffn(x, row(w['norm_ffn'][i]), w['ffn_w_in'], w['ffn_conv_w'][i],
                     row(w['ffn_conv_b'][i]), w['ffn_w_out'], st['ffn_conv'][i],
                     row(w['norm_final']), layer=i, B=B, TM=TM, final_norm=(i == DEPTH - 1),
                     batch_major_out=(regroup and i == DEPTH - 1))
        new['ffn_conv'].append(cb)
    return x.reshape(B, L, D_MODEL), new


def _stack(parts):
    return parts[0][None] if len(parts) == 1 else jnp.stack(parts)


def _time_major_hist(buf):
    n, bsz, wm1, c = buf.shape
    t = jnp.transpose(buf, (0, 2, 1, 3)).reshape(n, wm1 * bsz, c)
    return [t[j] for j in range(n)]


def _batch_major_hist(rows, bsz):
    t = _stack(rows)
    n, _, c = t.shape
    return jnp.transpose(t.reshape(n, -1, bsz, c), (0, 2, 1, 3))


def _run(x, st, w):
    bsz, length, _ = x.shape
    flat = lambda a: [a[j].reshape(bsz, -1) for j in range(a.shape[0])]
    stt = dict(s5_re=flat(st['s5_re']), s5_im=flat(st['s5_im']),
               rw_wkv=[st['rw_wkv'][j] for j in range(st['rw_wkv'].shape[0])],
               rw_shift=flat(st['rw_shift']), lru_h=flat(st['lru_h']),
               lru_conv=_time_major_hist(st['lru_conv']), ffn_conv=_time_major_hist(st['ffn_conv']))
    y, new = _trunk(x, stt, w, B=bsz, L=length)
    n5 = len(new['s5_re'])
    out = dict(
        s5_re=_stack(new['s5_re']).reshape(n5, bsz, S5_GROUPS, S5_STATE),
        s5_im=_stack(new['s5_im']).reshape(n5, bsz, S5_GROUPS, S5_STATE),
        rw_wkv=_stack(new['rw_wkv']), rw_shift=_stack(new['rw_shift']),
        lru_h=_stack(new['lru_h']), lru_conv=_batch_major_hist(new['lru_conv'], bsz),
        ffn_conv=_batch_major_hist(new['ffn_conv'], bsz))
    return y, out


def _prepare_weights(w):
    n_s5 = w['s5_a_re'].shape[0]
    n_rw = w['rw_mu'].shape[0]
    n_lru = w['lru_w_in'].shape[0]
    sub = lambda prefix, j: {k: v[j] for k, v in w.items() if k.startswith(prefix)}
    return dict(
        norm_mix=w['norm_mix'], norm_ffn=w['norm_ffn'], norm_final=w['norm_final'],
        s5=[_s5_params(sub('s5_', j)) for j in range(n_s5)],
        rw=[_rw_params(sub('rw_', j)) for j in range(n_rw)],
        lru=[_lru_params(sub('lru_', j)) for j in range(n_lru)],
        ffn_w_in=w['ffn_w_in'].astype(BF16), ffn_conv_w=w['ffn_conv_w'],
        ffn_conv_b=w['ffn_conv_b'], ffn_w_out=w['ffn_w_out'].astype(BF16))


def kernel(x_prompt, x_sample, state_s5_re, state_s5_im, state_rwkv_wkv, state_rwkv_shift, state_lru_h, state_lru_conv, state_ffn_conv, norm_mix, norm_ffn, norm_final, s5_a_re, s5_a_im, s5_log_dt, s5_b_re, s5_b_im, s5_c_re, s5_c_im, s5_d, s5_w_glu, rw_mu, rw_w_rkv, rw_w0, rw_w1, rw_w2, rw_a0, rw_a1, rw_a2, rw_g1, rw_g2, rw_k_k, rw_k_a, rw_r_k, rw_ln_w, rw_ln_b, rw_w_o, lru_w_in, lru_conv_w, lru_conv_b, lru_w_rg, lru_b_rg, lru_w_ig, lru_b_ig, lru_lambda, lru_w_out, ffn_w_in, ffn_conv_w, ffn_conv_b, ffn_w_out):
    w = _prepare_weights(dict(
        norm_mix=norm_mix, norm_ffn=norm_ffn, norm_final=norm_final,
        s5_a_re=s5_a_re, s5_a_im=s5_a_im, s5_log_dt=s5_log_dt, s5_b_re=s5_b_re, s5_b_im=s5_b_im,
        s5_c_re=s5_c_re, s5_c_im=s5_c_im, s5_d=s5_d, s5_w_glu=s5_w_glu,
        rw_mu=rw_mu, rw_w_rkv=rw_w_rkv, rw_w0=rw_w0, rw_w1=rw_w1, rw_w2=rw_w2, rw_a0=rw_a0,
        rw_a1=rw_a1, rw_a2=rw_a2, rw_g1=rw_g1, rw_g2=rw_g2, rw_k_k=rw_k_k, rw_k_a=rw_k_a,
        rw_r_k=rw_r_k, rw_ln_w=rw_ln_w, rw_ln_b=rw_ln_b, rw_w_o=rw_w_o,
        lru_w_in=lru_w_in, lru_conv_w=lru_conv_w, lru_conv_b=lru_conv_b, lru_w_rg=lru_w_rg,
        lru_b_rg=lru_b_rg, lru_w_ig=lru_w_ig, lru_b_ig=lru_b_ig, lru_lambda=lru_lambda,
        lru_w_out=lru_w_out, ffn_w_in=ffn_w_in, ffn_conv_w=ffn_conv_w, ffn_conv_b=ffn_conv_b,
        ffn_w_out=ffn_w_out))
    bsz, dt = x_prompt.shape[0], x_prompt.dtype
    n_s5, n_rw, n_lru = state_s5_re.shape[0], state_rwkv_wkv.shape[0], state_lru_h.shape[0]
    st_prompt = dict(
        s5_re=jnp.zeros((n_s5, bsz, S5_GROUPS, S5_STATE), dt),
        s5_im=jnp.zeros((n_s5, bsz, S5_GROUPS, S5_STATE), dt),
        rw_wkv=jnp.zeros((n_rw, bsz, RWKV_HEADS, RWKV_HEAD, RWKV_HEAD), dt),
        rw_shift=jnp.zeros((n_rw, bsz, D_MODEL), dt),
        lru_h=jnp.zeros((n_lru, bsz, D_RNN), dt),
        lru_conv=jnp.zeros((n_lru, bsz, LRU_CONV - 1, D_RNN), dt),
        ffn_conv=jnp.zeros((DEPTH, bsz, FFN_CONV - 1, D_FF), dt))
    st_sample = dict(s5_re=state_s5_re, s5_im=state_s5_im, rw_wkv=state_rwkv_wkv,
                     rw_shift=state_rwkv_shift, lru_h=state_lru_h, lru_conv=state_lru_conv,
                     ffn_conv=state_ffn_conv)
    y_p, new_p = _run(x_prompt, st_prompt, w)
    y_s, new_s = _run(x_sample, st_sample, w)
    return (y_p, y_s, new_p['s5_re'], new_s['s5_re'], new_p['s5_im'], new_s['s5_im'],
            new_p['rw_wkv'], new_s['rw_wkv'], new_p['rw_shift'], new_s['rw_shift'],
            new_p['lru_h'], new_s['lru_h'], new_p['lru_conv'], new_s['lru_conv'],
            new_p['ffn_conv'], new_s['ffn_conv'])
```

```python
import functools

import jax
import jax.numpy as jnp
from jax import lax
from jax.experimental import pallas as pl
from jax.experimental.pallas import tpu as pltpu

F32 = jnp.float32
BF16 = jnp.bfloat16

D_MODEL = 1024
DEPTH = 4
N_MIXERS = 3
RMS_EPS = 1e-6

S5_GROUP = 16
S5_GROUPS = D_MODEL // S5_GROUP
S5_STATE = 64
S5_SLABS = 8
S5_SLAB_STATE = (S5_GROUPS // S5_SLABS) * S5_STATE
S5_NSTATE = S5_GROUPS * S5_STATE

RWKV_HEAD = 64
RWKV_HEADS = D_MODEL // RWKV_HEAD
RWKV_PAIRS = RWKV_HEADS // 2
RWKV_GN_EPS = 64e-5
LORA_PAD = 128

D_RNN = D_MODEL
LRU_BLOCKS = 4
LRU_BLOCK = D_RNN // LRU_BLOCKS
LRU_C = 8.0
LRU_CONV = 4

D_FF = 2816
FFN_CONV = 3
FFN_CHUNK = 256
FFN_NCHUNK = D_FF // FFN_CHUNK
FFN_STEPS = 128

LANES = 128
WKV_CHUNK = 64
WKV_ROWS = 2
VMEM_LIMIT = 60 * 1024 * 1024


def _cparams():
    return pltpu.CompilerParams(dimension_semantics=("arbitrary",), vmem_limit_bytes=VMEM_LIMIT)


def _const_spec(shape):
    nd = len(shape)
    return pl.BlockSpec(shape, lambda i, _n=nd: (0,) * _n, pipeline_mode=pl.Buffered(1))


def _row_spec(tm, width):
    return pl.BlockSpec((tm, width), lambda i: (i, 0))


def _rms(x, g):
    ms = jnp.mean(x * x, axis=-1, keepdims=True)
    return x * lax.rsqrt(ms + RMS_EPS) * g


def _bdot(a, w):
    return jnp.dot(a.astype(BF16), w, preferred_element_type=F32)


def _softplus(z):
    return jnp.maximum(z, 0.0) + jnp.log1p(jnp.exp(-jnp.abs(z)))


def _head_sum(x, ones_bd):
    return jnp.dot(x.astype(BF16), ones_bd, preferred_element_type=F32)


def _load_rows(x_ref, stage, *, B, TM):
    if not stage:
        return x_ref[...]
    T = TM // B
    for j in range(D_MODEL // LANES):
        for b in range(B):
            stage[0][j, pl.ds(b, T, stride=B), :] = x_ref[b, :, j * LANES:(j + 1) * LANES]
    return jnp.concatenate([stage[0][j] for j in range(D_MODEL // LANES)], axis=1)


def _store_rows(o_ref, stage, y, *, B, TM):
    if not stage:
        o_ref[...] = y
        return
    T = TM // B
    for j in range(D_MODEL // LANES):
        stage[0][j] = y[:, j * LANES:(j + 1) * LANES]
        for b in range(B):
            o_ref[b, :, j * LANES:(j + 1) * LANES] = stage[0][j, pl.ds(b, T, stride=B), :]


def _stage_scratch(tm, batch_major):
    return [pltpu.VMEM((D_MODEL // LANES, tm, LANES), F32)] if batch_major else []


def _rows_or_batch_spec(tm, B, batch_major):
    if batch_major:
        return pl.BlockSpec((B, tm // B, D_MODEL), lambda i: (0, i, 0))
    return _row_spec(tm, D_MODEL)


def _ffn_body(x_ref, g_ref, win_ref, cw_ref, cb_ref, wout_ref, c0_ref, gf_ref,
              o_ref, cnew_ref, carry_ref, *stage, B, TM, final_norm):
    hist = (FFN_CONV - 1) * B

    @pl.when(pl.program_id(0) == 0)
    def _():
        carry_ref[...] = c0_ref[...]

    x = x_ref[...]
    h = _rms(x, g_ref[...]).astype(BF16)
    acc = jnp.zeros((TM, D_MODEL), F32)
    for c in range(FFN_NCHUNK):
        lo = c * FFN_CHUNK
        gate = jnp.dot(h, win_ref[:, lo:lo + FFN_CHUNK], preferred_element_type=F32)
        up = jnp.dot(h, win_ref[:, D_FF + lo:D_FF + lo + FFN_CHUNK], preferred_element_type=F32)
        ext = jnp.concatenate([carry_ref[:, lo:lo + FFN_CHUNK], gate], axis=0)
        conv = cb_ref[:, lo:lo + FFN_CHUNK] + ext[0:TM] * cw_ref[0:1, lo:lo + FFN_CHUNK]
        for k in range(1, FFN_CONV):
            conv = conv + ext[k * B:k * B + TM] * cw_ref[k:k + 1, lo:lo + FFN_CHUNK]
        carry_ref[:, lo:lo + FFN_CHUNK] = ext[TM:TM + hist]
        act = (conv * jax.nn.sigmoid(conv)) * up
        acc = acc + jnp.dot(act.astype(BF16), wout_ref[lo:lo + FFN_CHUNK, :],
                            preferred_element_type=F32)
    y = x + acc
    if final_norm:
        y = _rms(y, gf_ref[...])
    _store_rows(o_ref, stage, y, B=B, TM=TM)
    cnew_ref[...] = carry_ref[...]


def _layer_spec(shape, layer):
    nd = len(shape) - 1
    return pl.BlockSpec((None,) + tuple(shape[1:]), lambda i, _l=layer, _n=nd: (_l,) + (0,) * _n,
                        pipeline_mode=pl.Buffered(1))


def _ffn(x, g, win_all, cw, cb, wout_all, c0, gf, *, layer, B, TM, final_norm, batch_major_out):
    n = x.shape[0]
    out_rows = (jax.ShapeDtypeStruct((B, n // B, D_MODEL), F32) if batch_major_out
                else jax.ShapeDtypeStruct((n, D_MODEL), F32))
    hist = (FFN_CONV - 1) * B
    body = functools.partial(_ffn_body, B=B, TM=TM, final_norm=final_norm)
    return pl.pallas_call(
        body,
        grid=(n // TM,),
        in_specs=[_row_spec(TM, D_MODEL), _const_spec((1, D_MODEL)), _layer_spec(win_all.shape, layer),
                  _const_spec(cw.shape), _const_spec(cb.shape), _layer_spec(wout_all.shape, layer),
                  _const_spec(c0.shape), _const_spec((1, D_MODEL))],
        out_specs=[_rows_or_batch_spec(TM, B, batch_major_out), _const_spec((hist, D_FF))],
        out_shape=[out_rows, jax.ShapeDtypeStruct((hist, D_FF), F32)],
        scratch_shapes=[pltpu.VMEM((hist, D_FF), F32)] + _stage_scratch(TM, batch_major_out),
        compiler_params=_cparams(),
        name="conv_ffn",
    )(x, g, win_all, cw, cb, wout_all, c0, gf)


def _s5_body(x_ref, g_ref, wbr_ref, wbi_ref, cfr_ref, cfi_ref, abr_ref, abi_ref, wcr_ref, wci_ref,
             d_ref, wglu_ref, h0r_ref, h0i_ref, o_ref, hr_out, hi_out,
             xr_s, xi_s, hr_s, hi_s, y_s, *stage, B, TM):
    @pl.when(pl.program_id(0) == 0)
    def _():
        cr, ci = cfr_ref[...], cfi_ref[...]
        h0r, h0i = h0r_ref[...], h0i_ref[...]
        den = cr * cr + ci * ci
        hr_s[...] = (h0r * cr + h0i * ci) / den
        hi_s[...] = (h0i * cr - h0r * ci) / den

    x = _load_rows(x_ref, stage, B=B, TM=TM)
    u = _rms(x, g_ref[...])
    ub = u.astype(BF16)
    steps = TM // B
    for s in range(S5_SLABS):
        sl = slice(s * S5_SLAB_STATE, (s + 1) * S5_SLAB_STATE)
        us = ub[:, s * LANES:(s + 1) * LANES]
        xr_s[s] = jnp.dot(us, wbr_ref[s], preferred_element_type=F32)
        xi_s[s] = jnp.dot(us, wbi_ref[s], preferred_element_type=F32)
        ar = jnp.broadcast_to(abr_ref[:, sl], (B, S5_SLAB_STATE))
        ai = jnp.broadcast_to(abi_ref[:, sl], (B, S5_SLAB_STATE))

        hr, hi = hr_s[:, sl], hi_s[:, sl]
        for t in range(steps):
            rows = slice(t * B, (t + 1) * B)
            hr, hi = ((ar * hr - ai * hi) + xr_s[s, rows, :],
                      (ar * hi + ai * hr) + xi_s[s, rows, :])
            xr_s[s, rows, :] = hr
            xi_s[s, rows, :] = hi
        hr_s[:, sl] = hr
        hi_s[:, sl] = hi
        y_s[:, s * LANES:(s + 1) * LANES] = (
            jnp.dot(xr_s[s].astype(BF16), wcr_ref[s], preferred_element_type=F32)
            - jnp.dot(xi_s[s].astype(BF16), wci_ref[s], preferred_element_type=F32))
    y = y_s[...] + d_ref[...] * u
    z = _bdot(jax.nn.gelu(y), wglu_ref[...])
    o_ref[...] = x + z[:, :D_MODEL] * jax.nn.sigmoid(z[:, D_MODEL:])
    cr, ci = cfr_ref[...], cfi_ref[...]
    hr_out[...] = cr * hr_s[...] - ci * hi_s[...]
    hi_out[...] = cr * hi_s[...] + ci * hr_s[...]


def _s5(x, g, prm, h0r, h0i, *, B, TM, batch_major_in):
    n = x.shape[0] * x.shape[1] if batch_major_in else x.shape[0]
    body = functools.partial(_s5_body, B=B, TM=TM)
    consts = [g, prm['wbr'], prm['wbi'], prm['cfr'], prm['cfi'], prm['abr'], prm['abi'],
              prm['wcr'], prm['wci'], prm['d'], prm['wglu'], h0r, h0i]
    return pl.pallas_call(
        body,
        grid=(n // TM,),
        in_specs=([_rows_or_batch_spec(TM, B, batch_major_in)]
                  + [_const_spec(a.shape) for a in consts]),
        out_specs=[_row_spec(TM, D_MODEL), _const_spec((B, S5_NSTATE)), _const_spec((B, S5_NSTATE))],
        out_shape=[jax.ShapeDtypeStruct((n, D_MODEL), F32),
                   jax.ShapeDtypeStruct((B, S5_NSTATE), F32),
                   jax.ShapeDtypeStruct((B, S5_NSTATE), F32)],
        scratch_shapes=[pltpu.VMEM((S5_SLABS, TM, S5_SLAB_STATE), F32),
                        pltpu.VMEM((S5_SLABS, TM, S5_SLAB_STATE), F32),
                        pltpu.VMEM((B, S5_NSTATE), F32), pltpu.VMEM((B, S5_NSTATE), F32),
                        pltpu.VMEM((TM, D_MODEL), F32)] + _stage_scratch(TM, batch_major_in),
        compiler_params=_cparams(),
        name="s5_mixer",
    )(x, *consts)


def _s5_params(p):
    lam_re = jnp.minimum(p['s5_a_re'], -1e-4)
    lam_im = p['s5_a_im']
    dt = jnp.exp(p['s5_log_dt'])[:, None]
    mag = jnp.exp(lam_re * dt)
    ab_re = mag * jnp.cos(lam_im * dt)
    ab_im = mag * jnp.sin(lam_im * dt)
    den = lam_re * lam_re + lam_im * lam_im
    coef_re = ((ab_re - 1.0) * lam_re + ab_im * lam_im) / den
    coef_im = (ab_im * lam_re - (ab_re - 1.0) * lam_im) / den
    gps = S5_GROUPS // S5_SLABS
    eye = jnp.eye(gps, dtype=F32)

    def slab_in(b):
        b4 = b.reshape(S5_SLABS, gps, S5_STATE, S5_GROUP)
        w = jnp.einsum('sgpc,gh->sgchp', b4, eye)
        return w.reshape(S5_SLABS, gps * S5_GROUP, gps * S5_STATE).astype(BF16)

    def slab_out(c):
        c4 = c.reshape(S5_SLABS, gps, S5_GROUP, S5_STATE)
        w = jnp.einsum('sgcp,gh->sgphc', c4, eye)
        return w.reshape(S5_SLABS, gps * S5_STATE, gps * S5_GROUP).astype(BF16)

    flat = lambda a: a.reshape(1, S5_NSTATE)
    c_re, c_im = p['s5_c_re'], p['s5_c_im']
    cc_re = c_re * coef_re[:, None, :] - c_im * coef_im[:, None, :]
    cc_im = c_re * coef_im[:, None, :] + c_im * coef_re[:, None, :]
    return dict(wbr=slab_in(p['s5_b_re']), wbi=slab_in(p['s5_b_im']),
                wcr=slab_out(cc_re), wci=slab_out(cc_im),
                cfr=flat(coef_re), cfi=flat(coef_im), abr=flat(ab_re), abi=flat(ab_im),
                d=p['s5_d'].reshape(1, D_MODEL), wglu=p['s5_w_glu'].astype(BF16))


def _lru_body(x_ref, g_ref, win_ref, cw_ref, cb_ref, wrg_ref, brg_ref, wig_ref, big_ref, lam_ref,
              wout_ref, c0_ref, h0_ref, o_ref, cnew_ref, hnew_ref,
              carry_s, h_s, a_s, bx_s, *, B, TM):
    hist = (LRU_CONV - 1) * B

    @pl.when(pl.program_id(0) == 0)
    def _():
        carry_s[...] = c0_ref[...]
        h_s[...] = h0_ref[...]

    x = x_ref[...]
    xn = _rms(x, g_ref[...])
    gy = _bdot(xn, win_ref[...])
    gate_br = jax.nn.gelu(gy[:, :D_RNN])
    ext = jnp.concatenate([carry_s[...], gy[:, D_RNN:]], axis=0)
    u = cb_ref[...] + ext[0:TM] * cw_ref[0:1, :]
    for k in range(1, LRU_CONV):
        u = u + ext[k * B:k * B + TM] * cw_ref[k:k + 1, :]
    carry_s[...] = ext[TM:TM + hist]
    ub = u.astype(BF16)
    rg_parts, ig_parts = [], []
    for nb in range(LRU_BLOCKS):
        blk = ub[:, nb * LRU_BLOCK:(nb + 1) * LRU_BLOCK]
        rg_parts.append(jnp.dot(blk, wrg_ref[nb], preferred_element_type=F32))
        ig_parts.append(jnp.dot(blk, wig_ref[nb], preferred_element_type=F32))
    rg = jax.nn.sigmoid(jnp.concatenate(rg_parts, axis=1) + brg_ref[...])
    ig = jax.nn.sigmoid(jnp.concatenate(ig_parts, axis=1) + big_ref[...])
    log_sig = -_softplus(-lam_ref[...])
    log_a = LRU_C * rg * log_sig
    a = jnp.exp(log_a)
    a_s[...] = a
    bx_s[...] = jnp.sqrt(1.0 - a * a) * ig * u

    def step(t, h):
        rows = pl.ds(pl.multiple_of(t * B, B), B)
        h = a_s[rows, :] * h + bx_s[rows, :]
        bx_s[rows, :] = h
        return h

    h_last = lax.fori_loop(0, TM // B, step, h_s[...])
    h_s[...] = h_last
    o_ref[...] = x + _bdot(bx_s[...] * gate_br, wout_ref[...])
    cnew_ref[...] = carry_s[...]
    hnew_ref[...] = h_last


def _lru(x, g, prm, c0, h0, *, B, TM):
    n = x.shape[0]
    hist = (LRU_CONV - 1) * B
    body = functools.partial(_lru_body, B=B, TM=TM)
    consts = [g, prm['win'], prm['cw'], prm['cb'], prm['wrg'], prm['brg'], prm['wig'], prm['big'],
              prm['lam'], prm['wout'], c0, h0]
    return pl.pallas_call(
        body,
        grid=(n // TM,),
        in_specs=[_row_spec(TM, D_MODEL)] + [_const_spec(a.shape) for a in consts],
        out_specs=[_row_spec(TM, D_MODEL), _const_spec((hist, D_RNN)), _const_spec((B, D_RNN))],
        out_shape=[jax.ShapeDtypeStruct((n, D_MODEL), F32),
                   jax.ShapeDtypeStruct((hist, D_RNN), F32),
                   jax.ShapeDtypeStruct((B, D_RNN), F32)],
        scratch_shapes=[pltpu.VMEM((hist, D_RNN), F32), pltpu.VMEM((B, D_RNN), F32),
                        pltpu.VMEM((TM, D_RNN), F32), pltpu.VMEM((TM, D_RNN), F32)],
        compiler_params=_cparams(),
        name="rglru_mixer",
    )(x, *consts)


def _lru_params(p):
    row = lambda a: a.reshape(1, D_RNN)
    return dict(win=p['lru_w_in'].astype(BF16), cw=p['lru_conv_w'], cb=row(p['lru_conv_b']),
                wrg=p['lru_w_rg'].astype(BF16), brg=row(p['lru_b_rg']),
                wig=p['lru_w_ig'].astype(BF16), big=row(p['lru_b_ig']),
                lam=row(p['lru_lambda']), wout=p['lru_w_out'].astype(BF16))


NCOL = D_MODEL // LANES


def _emit_cols(o_ref, val):
    for j in range(NCOL):
        o_ref[j] = val[:, j * LANES:(j + 1) * LANES]


def _load_cols(ref):
    return jnp.concatenate([ref[j] for j in range(NCOL)], axis=1)


def _rw_pre_body(x_ref, g_ref, mu_ref, wrkv_ref, w0_ref, w1_ref, w2_ref, a0_ref, a1_ref, a2_ref,
                 g1_ref, g2_ref, kk_ref, ka_ref, ones_ref, sh0_ref,
                 r_o, cum_o, lw_o, k_o, v_o, kk_o, b_o, g_o, sh_o, sh_s, *, B, T):
    TM = T * B

    @pl.when(pl.program_id(0) == 0)
    def _():
        sh_s[...] = sh0_ref[...]

    def emit(o_ref, val):
        if T == 1:
            o_ref[...] = val
        else:
            _emit_cols(o_ref, val)

    xn = _rms(x_ref[...], g_ref[...])
    if TM > B:
        prev = jnp.concatenate([sh_s[...], xn[:TM - B]], axis=0)
    else:
        prev = sh_s[...]
    sh_s[...] = xn[TM - B:]
    xx = prev - xn
    mix = lambda n: xn + xx * mu_ref[n:n + 1, :]
    emit(r_o, _bdot(mix(0), wrkv_ref[0]))
    emit(v_o, _bdot(mix(2), wrkv_ref[2]))
    emit(g_o, _bdot(jax.nn.sigmoid(_bdot(mix(5), g1_ref[...])), g2_ref[...]))
    wl = w0_ref[...] + _bdot(jnp.tanh(_bdot(mix(3), w1_ref[...])), w2_ref[...])
    lw = -jnp.exp(-_softplus(-wl) - 0.5)
    emit(lw_o, lw)
    cum = lw
    sh = B
    while sh < TM:
        cum = cum + jnp.concatenate([jnp.zeros((sh, D_MODEL), F32), cum[:TM - sh]], axis=0)
        sh *= 2
    emit(cum_o, cum)
    a = jax.nn.sigmoid(a0_ref[...] + _bdot(_bdot(mix(4), a1_ref[...]), a2_ref[...]))
    k = _bdot(mix(1), wrkv_ref[1])
    kk = k * kk_ref[...]
    norm = jnp.sqrt(_head_sum(kk * kk, ones_ref[...]))
    kk = kk / jnp.maximum(norm, 1e-12)
    emit(kk_o, kk)
    emit(b_o, kk * a)
    emit(k_o, k * (1.0 + (a - 1.0) * ka_ref[...]))
    sh_o[...] = sh_s[...]


def _cols_spec(tm):
    return pl.BlockSpec((NCOL, tm, LANES), lambda i: (0, i, 0))


def _rw_pre(x, g, prm, sh0, *, B, T):
    n = x.shape[0]
    tm = T * B
    body = functools.partial(_rw_pre_body, B=B, T=T)
    consts = [g, prm['mu'], prm['wrkv'], prm['w0'], prm['w1'], prm['w2'], prm['a0'], prm['a1'],
              prm['a2'], prm['g1'], prm['g2'], prm['k_k'], prm['k_a'], prm['ones'], sh0]
    if T == 1:
        big, big_spec = jax.ShapeDtypeStruct((n, D_MODEL), F32), _row_spec(tm, D_MODEL)
    else:
        big, big_spec = jax.ShapeDtypeStruct((NCOL, n, LANES), F32), _cols_spec(tm)
    return pl.pallas_call(
        body,
        grid=(n // tm,),
        in_specs=[_row_spec(tm, D_MODEL)] + [_const_spec(a.shape) for a in consts],
        out_specs=[big_spec] * 8 + [_const_spec((B, D_MODEL))],
        out_shape=[big] * 8 + [jax.ShapeDtypeStruct((B, D_MODEL), F32)],
        scratch_shapes=[pltpu.VMEM((B, D_MODEL), F32)],
        compiler_params=_cparams(),
        name="rwkv_project",
    )(x, *consts)


def _rw_post_body(x_ref, o_ref, r_ref, k_ref, v_ref, g_ref, rk_ref, lnw_ref, lnb_ref, ones_ref,
                  wo_ref, out_ref, *, T):
    load = (lambda ref: ref[...]) if T == 1 else _load_cols
    o, r, k, v, g = (load(ref) for ref in (o_ref, r_ref, k_ref, v_ref, g_ref))
    ones = ones_ref[...]
    mean = _head_sum(o, ones) * (1.0 / RWKV_HEAD)
    dlt = o - mean
    var = _head_sum(dlt * dlt, ones) * (1.0 / RWKV_HEAD)
    on = dlt * lax.rsqrt(var + RWKV_GN_EPS) * lnw_ref[...] + lnb_ref[...]
    bonus = _head_sum(r * k * rk_ref[...], ones) * v
    out_ref[...] = x_ref[...] + _bdot((on + bonus) * g, wo_ref[...])


def _rw_post(x, o, r, k, v, g, prm, *, B, T):
    n = x.shape[0]
    tm = T * B
    consts = [prm['r_k'], prm['ln_w'], prm['ln_b'], prm['ones'], prm['wo']]
    big_spec = _row_spec(tm, D_MODEL) if T == 1 else _cols_spec(tm)
    return pl.pallas_call(
        functools.partial(_rw_post_body, T=T),
        grid=(n // tm,),
        in_specs=([_row_spec(tm, D_MODEL)] + [big_spec] * 5
                  + [_const_spec(a.shape) for a in consts]),
        out_specs=_row_spec(tm, D_MODEL),
        out_shape=jax.ShapeDtypeStruct((n, D_MODEL), F32),
        compiler_params=_cparams(),
        name="rwkv_output",
    )(x, o, r, k, v, g, *consts)


def _wkv_chunk_body(r_ref, cum_ref, lw_ref, k_ref, v_ref, kk_ref, b_ref, s0_ref, o_ref, sT_ref,
                    S_s, *, B, T):
    @pl.when(pl.program_id(0) == 0)
    def _():
        S_s[...] = s0_ref[...]

    lane = lax.broadcasted_iota(jnp.int32, (T, LANES), 1)
    trow = lax.broadcasted_iota(jnp.int32, (T, LANES), 0)
    first = lane < RWKV_HEAD
    strict = trow > (lane & (T - 1))
    incl = trow >= (lane & (T - 1))
    ri = lax.broadcasted_iota(jnp.int32, (LANES, LANES), 0)
    ci = lax.broadcasted_iota(jnp.int32, (LANES, LANES), 1)
    same_head = (ri < RWKV_HEAD) == (ci < RWKV_HEAD)

    def stack(y):
        return jnp.concatenate([jnp.where(first, y, 0.0), jnp.where(first, 0.0, y)],
                               axis=0).astype(BF16)

    nt = (((1,), (1,)), ((), ()))
    tn = (((0,), (0,)), ((), ()))
    nsteps = T.bit_length() - 1

    def per_group(g, carry):
        chains = [(g * WKV_ROWS + i, hp) for i in range(WKV_ROWS) for hp in range(RWKV_PAIRS)]
        idx = range(len(chains))
        at_chain = [(hp, pl.ds(b, T, stride=B), slice(None)) for b, hp in chains]
        s_old = [S_s[b, hp] for b, hp in chains]
        ks, vs, bbs, ends, e_ends, ps, sps = [], [], [], [], [], [], []
        for c in idx:
            at_c = at_chain[c]
            cm = cum_ref[at_c]
            k = k_ref[at_c]
            bb = b_ref[at_c]
            mid = cm[T // 2 - 1:T // 2, :]
            end = cm[T - 1:T, :]
            at = -kk_ref[at_c] * jnp.exp((cm - lw_ref[at_c]) - mid)
            rt = r_ref[at_c] * jnp.exp(cm - mid)
            e_neg = jnp.exp(mid - cm)
            lhs_f = jnp.concatenate([at, rt], axis=0)
            rhs = jnp.concatenate([stack(bb * e_neg), stack(k * e_neg)], axis=0)
            ps.append(lax.dot_general(lhs_f.astype(BF16), rhs, nt, preferred_element_type=F32))
            sps.append(lax.dot_general((lhs_f * jnp.exp(mid)).astype(BF16), s_old[c].astype(BF16),
                                       nt, preferred_element_type=F32))
            ks.append(k)
            vs.append(v_ref[at_c])
            bbs.append(bb)
            ends.append(end)
            e_ends.append(jnp.exp(end - cm))
        vst = [stack(v) for v in vs]
        ys = [sps[c][0:T] + jnp.dot(jnp.where(strict, ps[c][0:T, 2 * T:4 * T], 0.0).astype(BF16),
                                    vst[c], preferred_element_type=F32) for c in idx]
        ms = [jnp.where(strict, ps[c][0:T, 0:2 * T], 0.0) for c in idx]
        for it in range(nsteps):
            if it + 1 < nsteps:
                both = [jnp.dot(ms[c].astype(BF16),
                                jnp.concatenate([stack(ys[c]), stack(ms[c])], axis=1),
                                preferred_element_type=F32) for c in idx]
                ys = [ys[c] + both[c][:, 0:LANES] for c in idx]
                ms = [both[c][:, LANES:2 * LANES] for c in idx]
            else:
                ys = [ys[c] + jnp.dot(ms[c].astype(BF16), stack(ys[c]), preferred_element_type=F32)
                      for c in idx]
        outs, news = [], []
        for c in idx:
            rbk = jnp.where(jnp.concatenate([incl, incl], axis=1), ps[c][T:2 * T, :], 0.0)
            outs.append(sps[c][T:2 * T]
                        + jnp.dot(rbk.astype(BF16), jnp.concatenate([stack(ys[c]), vst[c]], axis=0),
                                  preferred_element_type=F32))
            upd = lax.dot_general(
                jnp.concatenate([ys[c], vs[c]], axis=0).astype(BF16),
                jnp.concatenate([bbs[c] * e_ends[c], ks[c] * e_ends[c]], axis=0).astype(BF16),
                tn, preferred_element_type=F32)
            news.append(s_old[c] * jnp.exp(ends[c]) + jnp.where(same_head, upd, 0.0))
        for c in idx:
            o_ref[at_chain[c]] = outs[c]
            S_s[chains[c][0], chains[c][1]] = news[c]
        return carry

    lax.fori_loop(0, B // WKV_ROWS, per_group, 0)

    @pl.when(pl.program_id(0) == pl.num_programs(0) - 1)
    def _():
        sT_ref[...] = S_s[...]


def _wkv_chunk(r, cum, lw, k, v, kk, bvec, s0, *, B, T):
    n = r.shape[1]
    tm = T * B
    body = functools.partial(_wkv_chunk_body, B=B, T=T)
    st_shape = (B, RWKV_PAIRS, LANES, LANES)
    return pl.pallas_call(
        body,
        grid=(n // tm,),
        in_specs=[_cols_spec(tm)] * 7 + [_const_spec(st_shape)],
        out_specs=[_cols_spec(tm), _const_spec(st_shape)],
        out_shape=[jax.ShapeDtypeStruct((NCOL, n, LANES), F32),
                   jax.ShapeDtypeStruct(st_shape, F32)],
        scratch_shapes=[pltpu.VMEM(st_shape, F32)],
        compiler_params=_cparams(),
        name="wkv_chunked",
    )(r, cum, lw, k, v, kk, bvec, s0)


def _wkv_step_body(s_ref, r_ref, lw_ref, k_ref, kk_ref, b_ref, v_ref, o_ref, sn_ref):
    w = jnp.exp(lw_ref[...])
    kk, bv, k, r = kk_ref[...], b_ref[...], k_ref[...], r_ref[...]

    def per_value_row(i, carry):
        s = s_ref[i]
        sa = -jnp.sum(s * kk, axis=0, keepdims=True)
        sn = s * w + sa * bv + v_ref[pl.ds(i, 1), :] * k
        sn_ref[i] = sn
        o_ref[pl.ds(i, 1), :] = jnp.sum(sn * r, axis=0, keepdims=True)
        return carry

    lax.fori_loop(0, RWKV_HEAD, per_value_row, 0)


def _wkv_step(s0, r, lw, k, kk, bvec, v):
    bsz = s0.shape[-1]
    vec = lambda t: t.T.reshape(RWKV_HEADS, RWKV_HEAD, bsz)
    s_spec = pl.BlockSpec((None, RWKV_HEAD, RWKV_HEAD, bsz), lambda h: (h, 0, 0, 0))
    v_spec = pl.BlockSpec((None, RWKV_HEAD, bsz), lambda h: (h, 0, 0))
    o, sn = pl.pallas_call(
        _wkv_step_body,
        grid=(RWKV_HEADS,),
        in_specs=[s_spec] + [v_spec] * 6,
        out_specs=[v_spec, s_spec],
        out_shape=[jax.ShapeDtypeStruct((RWKV_HEADS, RWKV_HEAD, bsz), F32),
                   jax.ShapeDtypeStruct(s0.shape, F32)],
        compiler_params=_cparams(),
        name="wkv_step",
    )(s0, vec(r), vec(lw), vec(k), vec(kk), vec(bvec), vec(v))
    return o.reshape(D_MODEL, bsz).T, sn


def _rw_params(p):
    row = lambda a: a.reshape(1, D_MODEL)
    pad_c = lambda w: jnp.pad(w, ((0, 0), (0, LORA_PAD - w.shape[1]))).astype(BF16)
    pad_r = lambda w: jnp.pad(w, ((0, LORA_PAD - w.shape[0]), (0, 0))).astype(BF16)
    head = jnp.arange(D_MODEL) // RWKV_HEAD
    ones = (head[:, None] == head[None, :]).astype(BF16)
    return dict(mu=p['rw_mu'], wrkv=p['rw_w_rkv'].astype(BF16), w0=row(p['rw_w0']),
                w1=pad_c(p['rw_w1']), w2=pad_r(p['rw_w2']), a0=row(p['rw_a0']),
                a1=pad_c(p['rw_a1']), a2=pad_r(p['rw_a2']), g1=pad_c(p['rw_g1']),
                g2=pad_r(p['rw_g2']), k_k=row(p['rw_k_k']), k_a=row(p['rw_k_a']),
                r_k=row(p['rw_r_k']), ln_w=row(p['rw_ln_w']), ln_b=row(p['rw_ln_b']),
                wo=p['rw_w_o'].astype(BF16), ones=ones)


def _pair_states(s):
    bsz = s.shape[0]
    s5 = s.reshape(bsz, RWKV_PAIRS, 2, RWKV_HEAD, RWKV_HEAD)
    eye = jnp.eye(2, dtype=s.dtype)
    bd = jnp.einsum('bpqij,qr->bpqirj', s5, eye)
    return bd.reshape(bsz, RWKV_PAIRS, LANES, LANES)


def _unpair_states(bd):
    bsz = bd.shape[0]
    s6 = bd.reshape(bsz, RWKV_PAIRS, 2, RWKV_HEAD, 2, RWKV_HEAD)
    s = jnp.stack([s6[:, :, 0, :, 0, :], s6[:, :, 1, :, 1, :]], axis=2)
    return s.reshape(bsz, RWKV_HEADS, RWKV_HEAD, RWKV_HEAD)


def _rwkv(x, g, prm, sh0, s0, *, B, L):
    T = min(L, WKV_CHUNK)
    r, cum, lw, k, v, kk, bvec, gt, sh_new = _rw_pre(x, g, prm, sh0, B=B, T=T)
    if L == 1:
        o, s_t = _wkv_step(jnp.transpose(s0, (1, 2, 3, 0)), r, lw, k, kk, bvec, v)
        s_new = jnp.transpose(s_t, (3, 0, 1, 2))
    else:
        o, s_bd = _wkv_chunk(r, cum, lw, k, v, kk, bvec, _pair_states(s0), B=B, T=T)
        s_new = _unpair_states(s_bd)
    y = _rw_post(x, o, r, k, v, gt, prm, B=B, T=T)
    return y, sh_new, s_new


def _trunk(x, st, w, *, B, L):
    TM = B * min(L, WKV_CHUNK)
    TM_FFN = B * min(L, FFN_STEPS)
    regroup = L > 1
    if not regroup:
        x = x.reshape(B, D_MODEL)
    row = lambda a: a.reshape(1, -1)
    new = {k: [] for k in ('s5_re', 's5_im', 'rw_wkv', 'rw_shift', 'lru_h', 'lru_conv', 'ffn_conv')}
    for i in range(DEPTH):
        kind, j = i % N_MIXERS, i // N_MIXERS
        g = row(w['norm_mix'][i])
        if kind == 0:
            x, hr, hi = _s5(x, g, w['s5'][j], st['s5_re'][j], st['s5_im'][j], B=B, TM=TM,
                            batch_major_in=(regroup and i == 0))
            new['s5_re'].append(hr)
            new['s5_im'].append(hi)
        elif kind == 1:
            x, sh, s = _rwkv(x, g, w['rw'][j], st['rw_shift'][j], st['rw_wkv'][j], B=B, L=L)
            new['rw_shift'].append(sh)
            new['rw_wkv'].append(s)
        else:
            x, cb, hl = _lru(x, g, w['lru'][j], st['lru_conv'][j], st['lru_h'][j], B=B, TM=TM)
            new['lru_conv'].append(cb)
            new['lru_h'].append(hl)
        x, cb = _ffn(x, row(w['norm_ffn'][i]), w['ffn_w_in'], w['ffn_conv_w'][i],
                     row(w['ffn_conv_b'][i]), w['ffn_w_out'], st['ffn_conv'][i],
                     row(w['norm_final']), layer=i, B=B, TM=TM_FFN, final_norm=(i == DEPTH - 1),
                     batch_major_out=(regroup and i == DEPTH - 1))
        new['ffn_conv'].append(cb)
    return x.reshape(B, L, D_MODEL), new


def _stack(parts):
    return parts[0][None] if len(parts) == 1 else jnp.stack(parts)


def _time_major_hist(buf):
    n, bsz, wm1, c = buf.shape
    t = jnp.transpose(buf, (0, 2, 1, 3)).reshape(n, wm1 * bsz, c)
    return [t[j] for j in range(n)]


def _batch_major_hist(rows, bsz):
    t = _stack(rows)
    n, _, c = t.shape
    return jnp.transpose(t.reshape(n, -1, bsz, c), (0, 2, 1, 3))


def _run(x, st, w):
    bsz, length, _ = x.shape
    flat = lambda a: [a[j].reshape(bsz, -1) for j in range(a.shape[0])]
    stt = dict(s5_re=flat(st['s5_re']), s5_im=flat(st['s5_im']),
               rw_wkv=[st['rw_wkv'][j] for j in range(st['rw_wkv'].shape[0])],
               rw_shift=flat(st['rw_shift']), lru_h=flat(st['lru_h']),
               lru_conv=_time_major_hist(st['lru_conv']), ffn_conv=_time_major_hist(st['ffn_conv']))
    y, new = _trunk(x, stt, w, B=bsz, L=length)
    n5 = len(new['s5_re'])
    out = dict(
        s5_re=_stack(new['s5_re']).reshape(n5, bsz, S5_GROUPS, S5_STATE),
        s5_im=_stack(new['s5_im']).reshape(n5, bsz, S5_GROUPS, S5_STATE),
        rw_wkv=_stack(new['rw_wkv']), rw_shift=_stack(new['rw_shift']),
        lru_h=_stack(new['lru_h']), lru_conv=_batch_major_hist(new['lru_conv'], bsz),
        ffn_conv=_batch_major_hist(new['ffn_conv'], bsz))
    return y, out


def _prepare_weights(w):
    n_s5 = w['s5_a_re'].shape[0]
    n_rw = w['rw_mu'].shape[0]
    n_lru = w['lru_w_in'].shape[0]
    sub = lambda prefix, j: {k: v[j] for k, v in w.items() if k.startswith(prefix)}
    return dict(
        norm_mix=w['norm_mix'], norm_ffn=w['norm_ffn'], norm_final=w['norm_final'],
        s5=[_s5_params(sub('s5_', j)) for j in range(n_s5)],
        rw=[_rw_params(sub('rw_', j)) for j in range(n_rw)],
        lru=[_lru_params(sub('lru_', j)) for j in range(n_lru)],
        ffn_w_in=w['ffn_w_in'].astype(BF16), ffn_conv_w=w['ffn_conv_w'],
        ffn_conv_b=w['ffn_conv_b'], ffn_w_out=w['ffn_w_out'].astype(BF16))


def kernel(x_prompt, x_sample, state_s5_re, state_s5_im, state_rwkv_wkv, state_rwkv_shift, state_lru_h, state_lru_conv, state_ffn_conv, norm_mix, norm_ffn, norm_final, s5_a_re, s5_a_im, s5_log_dt, s5_b_re, s5_b_im, s5_c_re, s5_c_im, s5_d, s5_w_glu, rw_mu, rw_w_rkv, rw_w0, rw_w1, rw_w2, rw_a0, rw_a1, rw_a2, rw_g1, rw_g2, rw_k_k, rw_k_a, rw_r_k, rw_ln_w, rw_ln_b, rw_w_o, lru_w_in, lru_conv_w, lru_conv_b, lru_w_rg, lru_b_rg, lru_w_ig, lru_b_ig, lru_lambda, lru_w_out, ffn_w_in, ffn_conv_w, ffn_conv_b, ffn_w_out):
    w = _prepare_weights(dict(
        norm_mix=norm_mix, norm_ffn=norm_ffn, norm_final=norm_final,
        s5_a_re=s5_a_re, s5_a_im=s5_a_im, s5_log_dt=s5_log_dt, s5_b_re=s5_b_re, s5_b_im=s5_b_im,
        s5_c_re=s5_c_re, s5_c_im=s5_c_im, s5_d=s5_d, s5_w_glu=s5_w_glu,
        rw_mu=rw_mu, rw_w_rkv=rw_w_rkv, rw_w0=rw_w0, rw_w1=rw_w1, rw_w2=rw_w2, rw_a0=rw_a0,
        rw_a1=rw_a1, rw_a2=rw_a2, rw_g1=rw_g1, rw_g2=rw_g2, rw_k_k=rw_k_k, rw_k_a=rw_k_a,
        rw_r_k=rw_r_k, rw_ln_w=rw_ln_w, rw_ln_b=rw_ln_b, rw_w_o=rw_w_o,
        lru_w_in=lru_w_in, lru_conv_w=lru_conv_w, lru_conv_b=lru_conv_b, lru_w_rg=lru_w_rg,
        lru_b_rg=lru_b_rg, lru_w_ig=lru_w_ig, lru_b_ig=lru_b_ig, lru_lambda=lru_lambda,
        lru_w_out=lru_w_out, ffn_w_in=ffn_w_in, ffn_conv_w=ffn_conv_w, ffn_conv_b=ffn_conv_b,
        ffn_w_out=ffn_w_out))
    bsz, dt = x_prompt.shape[0], x_prompt.dtype
    n_s5, n_rw, n_lru = state_s5_re.shape[0], state_rwkv_wkv.shape[0], state_lru_h.shape[0]
    st_prompt = dict(
        s5_re=jnp.zeros((n_s5, bsz, S5_GROUPS, S5_STATE), dt),
        s5_im=jnp.zeros((n_s5, bsz, S5_GROUPS, S5_STATE), dt),
        rw_wkv=jnp.zeros((n_rw, bsz, RWKV_HEADS, RWKV_HEAD, RWKV_HEAD), dt),
        rw_shift=jnp.zeros((n_rw, bsz, D_MODEL), dt),
        lru_h=jnp.zeros((n_lru, bsz, D_RNN), dt),
        lru_conv=jnp.zeros((n_lru, bsz, LRU_CONV - 1, D_RNN), dt),
        ffn_conv=jnp.zeros((DEPTH, bsz, FFN_CONV - 1, D_FF), dt))
    st_sample = dict(s5_re=state_s5_re, s5_im=state_s5_im, rw_wkv=state_rwkv_wkv,
                     rw_shift=state_rwkv_shift, lru_h=state_lru_h, lru_conv=state_lru_conv,
                     ffn_conv=state_ffn_conv)
    y_p, new_p = _run(x_prompt, st_prompt, w)
    y_s, new_s = _run(x_sample, st_sample, w)
    return (y_p, y_s, new_p['s5_re'], new_s['s5_re'], new_p['s5_im'], new_s['s5_im'],
            new_p['rw_wkv'], new_s['rw_wkv'], new_p['rw_shift'], new_s['rw_shift'],
            new_p['lru_h'], new_s['lru_h'], new_p['lru_conv'], new_s['lru_conv'],
            new_p['ffn_conv'], new_s['ffn_conv'])
```

```python
import functools

import jax
import jax.numpy as jnp
from jax import lax
from jax.experimental import pallas as pl
from jax.experimental.pallas import tpu as pltpu

F32 = jnp.float32
BF16 = jnp.bfloat16

D_MODEL = 1024
DEPTH = 4
N_MIXERS = 3
RMS_EPS = 1e-6

S5_GROUP = 16
S5_GROUPS = D_MODEL // S5_GROUP
S5_STATE = 64
S5_SLABS = 8
S5_SLAB_STATE = (S5_GROUPS // S5_SLABS) * S5_STATE
S5_NSTATE = S5_GROUPS * S5_STATE

RWKV_HEAD = 64
RWKV_HEADS = D_MODEL // RWKV_HEAD
RWKV_PAIRS = RWKV_HEADS // 2
RWKV_GN_EPS = 64e-5
LORA_PAD = 128

D_RNN = D_MODEL
LRU_BLOCKS = 4
LRU_BLOCK = D_RNN // LRU_BLOCKS
LRU_C = 8.0
LRU_CONV = 4

D_FF = 2816
FFN_CONV = 3
FFN_CHUNK = 256
FFN_NCHUNK = D_FF // FFN_CHUNK
FFN_STEPS = 64

LANES = 128
WKV_CHUNK = 64
WKV_ROWS = 2
VMEM_LIMIT = 60 * 1024 * 1024


def _cparams():
    return pltpu.CompilerParams(dimension_semantics=("arbitrary",), vmem_limit_bytes=VMEM_LIMIT)


def _const_spec(shape):
    nd = len(shape)
    return pl.BlockSpec(shape, lambda i, _n=nd: (0,) * _n, pipeline_mode=pl.Buffered(1))


def _row_spec(tm, width):
    return pl.BlockSpec((tm, width), lambda i: (i, 0))


def _rms(x, g):
    ms = jnp.mean(x * x, axis=-1, keepdims=True)
    return x * lax.rsqrt(ms + RMS_EPS) * g


def _bdot(a, w):
    return jnp.dot(a.astype(BF16), w, preferred_element_type=F32)


def _softplus(z):
    return jnp.maximum(z, 0.0) + jnp.log1p(jnp.exp(-jnp.abs(z)))


def _sigmoid(x):
    return 0.5 * jnp.tanh(0.5 * x) + 0.5


def _head_sum(x, sel, sel_t):
    return _bdot(_bdot(x, sel), sel_t)


def _load_rows(x_ref, stage, *, B, TM):
    if not stage:
        return x_ref[...]
    T = TM // B
    for j in range(D_MODEL // LANES):
        for b in range(B):
            stage[0][j, pl.ds(b, T, stride=B), :] = x_ref[b, :, j * LANES:(j + 1) * LANES]
    return jnp.concatenate([stage[0][j] for j in range(D_MODEL // LANES)], axis=1)


def _store_rows(o_ref, stage, y, *, B, TM):
    if not stage:
        o_ref[...] = y
        return
    T = TM // B
    for j in range(D_MODEL // LANES):
        stage[0][j] = y[:, j * LANES:(j + 1) * LANES]
        for b in range(B):
            o_ref[b, :, j * LANES:(j + 1) * LANES] = stage[0][j, pl.ds(b, T, stride=B), :]


def _stage_scratch(tm, batch_major):
    return [pltpu.VMEM((D_MODEL // LANES, tm, LANES), F32)] if batch_major else []


def _rows_or_batch_spec(tm, B, batch_major):
    if batch_major:
        return pl.BlockSpec((B, tm // B, D_MODEL), lambda i: (0, i, 0))
    return _row_spec(tm, D_MODEL)


def _ffn_body(x_ref, g_ref, win_ref, cw_ref, cb_ref, wout_ref, c0_ref, gf_ref,
              o_ref, cnew_ref, carry_ref, *stage, B, TM, final_norm):
    hist = (FFN_CONV - 1) * B

    @pl.when(pl.program_id(0) == 0)
    def _():
        carry_ref[...] = c0_ref[...]

    x = x_ref[...]
    h = _rms(x, g_ref[...]).astype(BF16)
    acc = jnp.zeros((TM, D_MODEL), F32)
    for c in range(FFN_NCHUNK):
        lo = c * FFN_CHUNK
        gate = jnp.dot(h, win_ref[:, lo:lo + FFN_CHUNK], preferred_element_type=F32)
        up = jnp.dot(h, win_ref[:, D_FF + lo:D_FF + lo + FFN_CHUNK], preferred_element_type=F32)
        ext = jnp.concatenate([carry_ref[:, lo:lo + FFN_CHUNK], gate], axis=0)
        conv = cb_ref[:, lo:lo + FFN_CHUNK] + ext[0:TM] * cw_ref[0:1, lo:lo + FFN_CHUNK]
        for k in range(1, FFN_CONV):
            conv = conv + ext[k * B:k * B + TM] * cw_ref[k:k + 1, lo:lo + FFN_CHUNK]
        carry_ref[:, lo:lo + FFN_CHUNK] = ext[TM:TM + hist]
        act = (conv * _sigmoid(conv)) * up
        acc = acc + jnp.dot(act.astype(BF16), wout_ref[lo:lo + FFN_CHUNK, :],
                            preferred_element_type=F32)
    y = x + acc
    if final_norm:
        y = _rms(y, gf_ref[...])
    _store_rows(o_ref, stage, y, B=B, TM=TM)
    cnew_ref[...] = carry_ref[...]


def _layer_spec(shape, layer):
    nd = len(shape) - 1
    return pl.BlockSpec((None,) + tuple(shape[1:]), lambda i, _l=layer, _n=nd: (_l,) + (0,) * _n,
                        pipeline_mode=pl.Buffered(1))


def _ffn(x, g, win_all, cw, cb, wout_all, c0, gf, *, layer, B, TM, final_norm, batch_major_out):
    n = x.shape[0]
    out_rows = (jax.ShapeDtypeStruct((B, n // B, D_MODEL), F32) if batch_major_out
                else jax.ShapeDtypeStruct((n, D_MODEL), F32))
    hist = (FFN_CONV - 1) * B
    body = functools.partial(_ffn_body, B=B, TM=TM, final_norm=final_norm)
    return pl.pallas_call(
        body,
        grid=(n // TM,),
        in_specs=[_row_spec(TM, D_MODEL), _const_spec((1, D_MODEL)), _layer_spec(win_all.shape, layer),
                  _const_spec(cw.shape), _const_spec(cb.shape), _layer_spec(wout_all.shape, layer),
                  _const_spec(c0.shape), _const_spec((1, D_MODEL))],
        out_specs=[_rows_or_batch_spec(TM, B, batch_major_out), _const_spec((hist, D_FF))],
        out_shape=[out_rows, jax.ShapeDtypeStruct((hist, D_FF), F32)],
        scratch_shapes=[pltpu.VMEM((hist, D_FF), F32)] + _stage_scratch(TM, batch_major_out),
        compiler_params=_cparams(),
        name="conv_ffn",
    )(x, g, win_all, cw, cb, wout_all, c0, gf)


def _s5_body(x_ref, g_ref, wbr_ref, wbi_ref, cfr_ref, cfi_ref, abr_ref, abi_ref, wcr_ref, wci_ref,
             d_ref, wglu_ref, h0r_ref, h0i_ref, o_ref, hr_out, hi_out,
             xr_s, xi_s, hr_s, hi_s, y_s, *stage, B, TM):
    @pl.when(pl.program_id(0) == 0)
    def _():
        cr, ci = cfr_ref[...], cfi_ref[...]
        h0r, h0i = h0r_ref[...], h0i_ref[...]
        den = cr * cr + ci * ci
        hr_s[...] = (h0r * cr + h0i * ci) / den
        hi_s[...] = (h0i * cr - h0r * ci) / den

    x = _load_rows(x_ref, stage, B=B, TM=TM)
    u = _rms(x, g_ref[...])
    ub = u.astype(BF16)
    steps = TM // B
    for s in range(S5_SLABS):
        sl = slice(s * S5_SLAB_STATE, (s + 1) * S5_SLAB_STATE)
        us = ub[:, s * LANES:(s + 1) * LANES]
        xr_s[s] = jnp.dot(us, wbr_ref[s], preferred_element_type=F32)
        xi_s[s] = jnp.dot(us, wbi_ref[s], preferred_element_type=F32)
        ar = jnp.broadcast_to(abr_ref[:, sl], (B, S5_SLAB_STATE))
        ai = jnp.broadcast_to(abi_ref[:, sl], (B, S5_SLAB_STATE))

        hr, hi = hr_s[:, sl], hi_s[:, sl]
        for t in range(steps):
            rows = slice(t * B, (t + 1) * B)
            hr, hi = ((ar * hr - ai * hi) + xr_s[s, rows, :],
                      (ar * hi + ai * hr) + xi_s[s, rows, :])
            xr_s[s, rows, :] = hr
            xi_s[s, rows, :] = hi
        hr_s[:, sl] = hr
        hi_s[:, sl] = hi
        y_s[:, s * LANES:(s + 1) * LANES] = (
            jnp.dot(xr_s[s].astype(BF16), wcr_ref[s], preferred_element_type=F32)
            - jnp.dot(xi_s[s].astype(BF16), wci_ref[s], preferred_element_type=F32))
    y = y_s[...] + d_ref[...] * u
    z = _bdot(jax.nn.gelu(y), wglu_ref[...])
    o_ref[...] = x + z[:, :D_MODEL] * _sigmoid(z[:, D_MODEL:])
    cr, ci = cfr_ref[...], cfi_ref[...]
    hr_out[...] = cr * hr_s[...] - ci * hi_s[...]
    hi_out[...] = cr * hi_s[...] + ci * hr_s[...]


def _s5(x, g, prm, h0r, h0i, *, B, TM, batch_major_in):
    n = x.shape[0] * x.shape[1] if batch_major_in else x.shape[0]
    body = functools.partial(_s5_body, B=B, TM=TM)
    consts = [g, prm['wbr'], prm['wbi'], prm['cfr'], prm['cfi'], prm['abr'], prm['abi'],
              prm['wcr'], prm['wci'], prm['d'], prm['wglu'], h0r, h0i]
    return pl.pallas_call(
        body,
        grid=(n // TM,),
        in_specs=([_rows_or_batch_spec(TM, B, batch_major_in)]
                  + [_const_spec(a.shape) for a in consts]),
        out_specs=[_row_spec(TM, D_MODEL), _const_spec((B, S5_NSTATE)), _const_spec((B, S5_NSTATE))],
        out_shape=[jax.ShapeDtypeStruct((n, D_MODEL), F32),
                   jax.ShapeDtypeStruct((B, S5_NSTATE), F32),
                   jax.ShapeDtypeStruct((B, S5_NSTATE), F32)],
        scratch_shapes=[pltpu.VMEM((S5_SLABS, TM, S5_SLAB_STATE), F32),
                        pltpu.VMEM((S5_SLABS, TM, S5_SLAB_STATE), F32),
                        pltpu.VMEM((B, S5_NSTATE), F32), pltpu.VMEM((B, S5_NSTATE), F32),
                        pltpu.VMEM((TM, D_MODEL), F32)] + _stage_scratch(TM, batch_major_in),
        compiler_params=_cparams(),
        name="s5_mixer",
    )(x, *consts)


def _s5_params(p):
    lam_re = jnp.minimum(p['s5_a_re'], -1e-4)
    lam_im = p['s5_a_im']
    dt = jnp.exp(p['s5_log_dt'])[:, None]
    mag = jnp.exp(lam_re * dt)
    ab_re = mag * jnp.cos(lam_im * dt)
    ab_im = mag * jnp.sin(lam_im * dt)
    den = lam_re * lam_re + lam_im * lam_im
    coef_re = ((ab_re - 1.0) * lam_re + ab_im * lam_im) / den
    coef_im = (ab_im * lam_re - (ab_re - 1.0) * lam_im) / den
    gps = S5_GROUPS // S5_SLABS
    eye = jnp.eye(gps, dtype=F32)

    def slab_in(b):
        b4 = b.reshape(S5_SLABS, gps, S5_STATE, S5_GROUP)
        w = jnp.einsum('sgpc,gh->sgchp', b4, eye)
        return w.reshape(S5_SLABS, gps * S5_GROUP, gps * S5_STATE).astype(BF16)

    def slab_out(c):
        c4 = c.reshape(S5_SLABS, gps, S5_GROUP, S5_STATE)
        w = jnp.einsum('sgcp,gh->sgphc', c4, eye)
        return w.reshape(S5_SLABS, gps * S5_STATE, gps * S5_GROUP).astype(BF16)

    flat = lambda a: a.reshape(1, S5_NSTATE)
    c_re, c_im = p['s5_c_re'], p['s5_c_im']
    cc_re = c_re * coef_re[:, None, :] - c_im * coef_im[:, None, :]
    cc_im = c_re * coef_im[:, None, :] + c_im * coef_re[:, None, :]
    return dict(wbr=slab_in(p['s5_b_re']), wbi=slab_in(p['s5_b_im']),
                wcr=slab_out(cc_re), wci=slab_out(cc_im),
                cfr=flat(coef_re), cfi=flat(coef_im), abr=flat(ab_re), abi=flat(ab_im),
                d=p['s5_d'].reshape(1, D_MODEL), wglu=p['s5_w_glu'].astype(BF16))


def _lru_body(x_ref, g_ref, win_ref, cw_ref, cb_ref, wrg_ref, brg_ref, wig_ref, big_ref, lam_ref,
              wout_ref, c0_ref, h0_ref, o_ref, cnew_ref, hnew_ref,
              carry_s, h_s, a_s, bx_s, *, B, TM):
    hist = (LRU_CONV - 1) * B

    @pl.when(pl.program_id(0) == 0)
    def _():
        carry_s[...] = c0_ref[...]
        h_s[...] = h0_ref[...]

    x = x_ref[...]
    xn = _rms(x, g_ref[...])
    gy = _bdot(xn, win_ref[...])
    gate_br = jax.nn.gelu(gy[:, :D_RNN])
    ext = jnp.concatenate([carry_s[...], gy[:, D_RNN:]], axis=0)
    u = cb_ref[...] + ext[0:TM] * cw_ref[0:1, :]
    for k in range(1, LRU_CONV):
        u = u + ext[k * B:k * B + TM] * cw_ref[k:k + 1, :]
    carry_s[...] = ext[TM:TM + hist]
    ub = u.astype(BF16)
    rg_parts, ig_parts = [], []
    for nb in range(LRU_BLOCKS):
        blk = ub[:, nb * LRU_BLOCK:(nb + 1) * LRU_BLOCK]
        rg_parts.append(jnp.dot(blk, wrg_ref[nb], preferred_element_type=F32))
        ig_parts.append(jnp.dot(blk, wig_ref[nb], preferred_element_type=F32))
    rg = _sigmoid(jnp.concatenate(rg_parts, axis=1) + brg_ref[...])
    ig = _sigmoid(jnp.concatenate(ig_parts, axis=1) + big_ref[...])
    log_sig = -_softplus(-lam_ref[...])
    log_a = LRU_C * rg * log_sig
    a = jnp.exp(log_a)
    a_s[...] = a
    bx_s[...] = jnp.sqrt(1.0 - a * a) * ig * u

    def step(t, h):
        rows = pl.ds(pl.multiple_of(t * B, B), B)
        h = a_s[rows, :] * h + bx_s[rows, :]
        bx_s[rows, :] = h
        return h

    h_last = lax.fori_loop(0, TM // B, step, h_s[...])
    h_s[...] = h_last
    o_ref[...] = x + _bdot(bx_s[...] * gate_br, wout_ref[...])
    cnew_ref[...] = carry_s[...]
    hnew_ref[...] = h_last


def _lru(x, g, prm, c0, h0, *, B, TM):
    n = x.shape[0]
    hist = (LRU_CONV - 1) * B
    body = functools.partial(_lru_body, B=B, TM=TM)
    consts = [g, prm['win'], prm['cw'], prm['cb'], prm['wrg'], prm['brg'], prm['wig'], prm['big'],
              prm['lam'], prm['wout'], c0, h0]
    return pl.pallas_call(
        body,
        grid=(n // TM,),
        in_specs=[_row_spec(TM, D_MODEL)] + [_const_spec(a.shape) for a in consts],
        out_specs=[_row_spec(TM, D_MODEL), _const_spec((hist, D_RNN)), _const_spec((B, D_RNN))],
        out_shape=[jax.ShapeDtypeStruct((n, D_MODEL), F32),
                   jax.ShapeDtypeStruct((hist, D_RNN), F32),
                   jax.ShapeDtypeStruct((B, D_RNN), F32)],
        scratch_shapes=[pltpu.VMEM((hist, D_RNN), F32), pltpu.VMEM((B, D_RNN), F32),
                        pltpu.VMEM((TM, D_RNN), F32), pltpu.VMEM((TM, D_RNN), F32)],
        compiler_params=_cparams(),
        name="rglru_mixer",
    )(x, *consts)


def _lru_params(p):
    row = lambda a: a.reshape(1, D_RNN)
    return dict(win=p['lru_w_in'].astype(BF16), cw=p['lru_conv_w'], cb=row(p['lru_conv_b']),
                wrg=p['lru_w_rg'].astype(BF16), brg=row(p['lru_b_rg']),
                wig=p['lru_w_ig'].astype(BF16), big=row(p['lru_b_ig']),
                lam=row(p['lru_lambda']), wout=p['lru_w_out'].astype(BF16))


NCOL = D_MODEL // LANES


def _emit_cols(o_ref, val):
    for j in range(NCOL):
        o_ref[j] = val[:, j * LANES:(j + 1) * LANES]


def _load_cols(ref):
    return jnp.concatenate([ref[j] for j in range(NCOL)], axis=1)


def _rw_pre_body(x_ref, g_ref, mu_ref, wrkv_ref, w0_ref, w1_ref, w2_ref, a0_ref, a1_ref, a2_ref,
                 g1_ref, g2_ref, kk_ref, ka_ref, sel_ref, selt_ref, sh0_ref,
                 r_o, cum_o, lw_o, k_o, v_o, kk_o, b_o, g_o, sh_o, sh_s, *, B, T):
    TM = T * B

    @pl.when(pl.program_id(0) == 0)
    def _():
        sh_s[...] = sh0_ref[...]

    def emit(o_ref, val):
        if T == 1:
            o_ref[...] = val
        else:
            _emit_cols(o_ref, val)

    xn = _rms(x_ref[...], g_ref[...])
    if TM > B:
        prev = jnp.concatenate([sh_s[...], xn[:TM - B]], axis=0)
    else:
        prev = sh_s[...]
    sh_s[...] = xn[TM - B:]
    xx = prev - xn
    mix = lambda n: xn + xx * mu_ref[n:n + 1, :]
    emit(r_o, _bdot(mix(0), wrkv_ref[0]))
    emit(v_o, _bdot(mix(2), wrkv_ref[2]))
    emit(g_o, _bdot(_sigmoid(_bdot(mix(5), g1_ref[...])), g2_ref[...]))
    wl = w0_ref[...] + _bdot(jnp.tanh(_bdot(mix(3), w1_ref[...])), w2_ref[...])
    lw = -jnp.exp(-_softplus(-wl) - 0.5)
    emit(lw_o, lw)
    cum = lw
    sh = B
    while sh < TM:
        cum = cum + jnp.concatenate([jnp.zeros((sh, D_MODEL), F32), cum[:TM - sh]], axis=0)
        sh *= 2
    emit(cum_o, cum)
    a = _sigmoid(a0_ref[...] + _bdot(_bdot(mix(4), a1_ref[...]), a2_ref[...]))
    k = _bdot(mix(1), wrkv_ref[1])
    kk = k * kk_ref[...]
    n2 = _head_sum(kk * kk, sel_ref[...], selt_ref[...])
    kk = kk * lax.rsqrt(jnp.maximum(n2, 1e-24))
    emit(kk_o, kk)
    emit(b_o, kk * a)
    emit(k_o, k * (1.0 + (a - 1.0) * ka_ref[...]))
    sh_o[...] = sh_s[...]


def _cols_spec(tm):
    return pl.BlockSpec((NCOL, tm, LANES), lambda i: (0, i, 0))


def _rw_pre(x, g, prm, sh0, *, B, T):
    n = x.shape[0]
    tm = T * B
    body = functools.partial(_rw_pre_body, B=B, T=T)
    consts = [g, prm['mu'], prm['wrkv'], prm['w0'], prm['w1'], prm['w2'], prm['a0'], prm['a1'],
              prm['a2'], prm['g1'], prm['g2'], prm['k_k'], prm['k_a'], prm['sel'], prm['sel_t'], sh0]
    if T == 1:
        big, big_spec = jax.ShapeDtypeStruct((n, D_MODEL), F32), _row_spec(tm, D_MODEL)
    else:
        big, big_spec = jax.ShapeDtypeStruct((NCOL, n, LANES), F32), _cols_spec(tm)
    return pl.pallas_call(
        body,
        grid=(n // tm,),
        in_specs=[_row_spec(tm, D_MODEL)] + [_const_spec(a.shape) for a in consts],
        out_specs=[big_spec] * 8 + [_const_spec((B, D_MODEL))],
        out_shape=[big] * 8 + [jax.ShapeDtypeStruct((B, D_MODEL), F32)],
        scratch_shapes=[pltpu.VMEM((B, D_MODEL), F32)],
        compiler_params=_cparams(),
        name="rwkv_project",
    )(x, *consts)


def _rw_post_body(x_ref, o_ref, r_ref, k_ref, v_ref, g_ref, rk_ref, lnw_ref, lnb_ref, sel_ref,
                  selt_ref, wo_ref, out_ref, *, T):
    load = (lambda ref: ref[...]) if T == 1 else _load_cols
    o, r, k, v, g = (load(ref) for ref in (o_ref, r_ref, k_ref, v_ref, g_ref))
    hsum = functools.partial(_head_sum, sel=sel_ref[...], sel_t=selt_ref[...])
    mean = hsum(o) * (1.0 / RWKV_HEAD)
    dlt = o - mean
    var = hsum(dlt * dlt) * (1.0 / RWKV_HEAD)
    on = dlt * lax.rsqrt(var + RWKV_GN_EPS) * lnw_ref[...] + lnb_ref[...]
    bonus = hsum(r * k * rk_ref[...]) * v
    out_ref[...] = x_ref[...] + _bdot((on + bonus) * g, wo_ref[...])


def _rw_post(x, o, r, k, v, g, prm, *, B, T):
    n = x.shape[0]
    tm = T * B
    consts = [prm['r_k'], prm['ln_w'], prm['ln_b'], prm['sel'], prm['sel_t'], prm['wo']]
    big_spec = _row_spec(tm, D_MODEL) if T == 1 else _cols_spec(tm)
    return pl.pallas_call(
        functools.partial(_rw_post_body, T=T),
        grid=(n // tm,),
        in_specs=([_row_spec(tm, D_MODEL)] + [big_spec] * 5
                  + [_const_spec(a.shape) for a in consts]),
        out_specs=_row_spec(tm, D_MODEL),
        out_shape=jax.ShapeDtypeStruct((n, D_MODEL), F32),
        compiler_params=_cparams(),
        name="rwkv_output",
    )(x, o, r, k, v, g, *consts)


def _wkv_chunk_body(r_ref, cum_ref, lw_ref, k_ref, v_ref, kk_ref, b_ref, s0_ref, o_ref, sT_ref,
                    S_s, *, B, T):
    @pl.when(pl.program_id(0) == 0)
    def _():
        S_s[...] = s0_ref[...]

    lane = lax.broadcasted_iota(jnp.int32, (T, LANES), 1)
    trow = lax.broadcasted_iota(jnp.int32, (T, LANES), 0)
    first = lane < RWKV_HEAD
    strict = trow > (lane & (T - 1))
    incl = trow >= (lane & (T - 1))
    ri = lax.broadcasted_iota(jnp.int32, (LANES, LANES), 0)
    ci = lax.broadcasted_iota(jnp.int32, (LANES, LANES), 1)
    same_head = (ri < RWKV_HEAD) == (ci < RWKV_HEAD)

    def stack(y):
        return jnp.concatenate([jnp.where(first, y, 0.0), jnp.where(first, 0.0, y)],
                               axis=0).astype(BF16)

    nt = (((1,), (1,)), ((), ()))
    tn = (((0,), (0,)), ((), ()))
    nsteps = T.bit_length() - 1

    def per_group(g, carry):
        chains = [(g * WKV_ROWS + i, hp) for i in range(WKV_ROWS) for hp in range(RWKV_PAIRS)]
        idx = range(len(chains))
        at_chain = [(hp, pl.ds(b, T, stride=B), slice(None)) for b, hp in chains]
        s_old = [S_s[b, hp] for b, hp in chains]
        ks, vs, bbs, ends, e_ends, ps, sps = [], [], [], [], [], [], []
        for c in idx:
            at_c = at_chain[c]
            cm = cum_ref[at_c]
            k = k_ref[at_c]
            bb = b_ref[at_c]
            mid = cm[T // 2 - 1:T // 2, :]
            end = cm[T - 1:T, :]
            at = -kk_ref[at_c] * jnp.exp((cm - lw_ref[at_c]) - mid)
            rt = r_ref[at_c] * jnp.exp(cm - mid)
            e_neg = jnp.exp(mid - cm)
            lhs_f = jnp.concatenate([at, rt], axis=0)
            rhs = jnp.concatenate([stack(bb * e_neg), stack(k * e_neg)], axis=0)
            ps.append(lax.dot_general(lhs_f.astype(BF16), rhs, nt, preferred_element_type=F32))
            sps.append(lax.dot_general((lhs_f * jnp.exp(mid)).astype(BF16), s_old[c].astype(BF16),
                                       nt, preferred_element_type=F32))
            ks.append(k)
            vs.append(v_ref[at_c])
            bbs.append(bb)
            ends.append(end)
            e_ends.append(jnp.exp(end - cm))
        vst = [stack(v) for v in vs]
        ys = [sps[c][0:T] + jnp.dot(jnp.where(strict, ps[c][0:T, 2 * T:4 * T], 0.0).astype(BF16),
                                    vst[c], preferred_element_type=F32) for c in idx]
        ms = [jnp.where(strict, ps[c][0:T, 0:2 * T], 0.0) for c in idx]
        for it in range(nsteps):
            if it + 1 < nsteps:
                both = [jnp.dot(ms[c].astype(BF16),
                                jnp.concatenate([stack(ys[c]), stack(ms[c])], axis=1),
                                preferred_element_type=F32) for c in idx]
                ys = [ys[c] + both[c][:, 0:LANES] for c in idx]
                ms = [both[c][:, LANES:2 * LANES] for c in idx]
            else:
                ys = [ys[c] + jnp.dot(ms[c].astype(BF16), stack(ys[c]), preferred_element_type=F32)
                      for c in idx]
        outs, news = [], []
        for c in idx:
            rbk = jnp.where(jnp.concatenate([incl, incl], axis=1), ps[c][T:2 * T, :], 0.0)
            outs.append(sps[c][T:2 * T]
                        + jnp.dot(rbk.astype(BF16), jnp.concatenate([stack(ys[c]), vst[c]], axis=0),
                                  preferred_element_type=F32))
            upd = lax.dot_general(
                jnp.concatenate([ys[c], vs[c]], axis=0).astype(BF16),
                jnp.concatenate([bbs[c] * e_ends[c], ks[c] * e_ends[c]], axis=0).astype(BF16),
                tn, preferred_element_type=F32)
            news.append(s_old[c] * jnp.exp(ends[c]) + jnp.where(same_head, upd, 0.0))
        for c in idx:
            o_ref[at_chain[c]] = outs[c]
            S_s[chains[c][0], chains[c][1]] = news[c]
        return carry

    lax.fori_loop(0, B // WKV_ROWS, per_group, 0)

    @pl.when(pl.program_id(0) == pl.num_programs(0) - 1)
    def _():
        sT_ref[...] = S_s[...]


def _wkv_chunk(r, cum, lw, k, v, kk, bvec, s0, *, B, T):
    n = r.shape[1]
    tm = T * B
    body = functools.partial(_wkv_chunk_body, B=B, T=T)
    st_shape = (B, RWKV_PAIRS, LANES, LANES)
    return pl.pallas_call(
        body,
        grid=(n // tm,),
        in_specs=[_cols_spec(tm)] * 7 + [_const_spec(st_shape)],
        out_specs=[_cols_spec(tm), _const_spec(st_shape)],
        out_shape=[jax.ShapeDtypeStruct((NCOL, n, LANES), F32),
                   jax.ShapeDtypeStruct(st_shape, F32)],
        scratch_shapes=[pltpu.VMEM(st_shape, F32)],
        compiler_params=_cparams(),
        name="wkv_chunked",
    )(r, cum, lw, k, v, kk, bvec, s0)


def _wkv_step_body(s_ref, r_ref, lw_ref, k_ref, kk_ref, b_ref, v_ref, o_ref, sn_ref):
    w = jnp.exp(lw_ref[...])
    kk, bv, k, r = kk_ref[...], b_ref[...], k_ref[...], r_ref[...]

    def per_value_row(i, carry):
        s = s_ref[i]
        sa = -jnp.sum(s * kk, axis=0, keepdims=True)
        sn = s * w + sa * bv + v_ref[pl.ds(i, 1), :] * k
        sn_ref[i] = sn
        o_ref[pl.ds(i, 1), :] = jnp.sum(sn * r, axis=0, keepdims=True)
        return carry

    lax.fori_loop(0, RWKV_HEAD, per_value_row, 0)


def _wkv_step(s0, r, lw, k, kk, bvec, v):
    bsz = s0.shape[-1]
    vec = lambda t: t.T.reshape(RWKV_HEADS, RWKV_HEAD, bsz)
    s_spec = pl.BlockSpec((None, RWKV_HEAD, RWKV_HEAD, bsz), lambda h: (h, 0, 0, 0))
    v_spec = pl.BlockSpec((None, RWKV_HEAD, bsz), lambda h: (h, 0, 0))
    o, sn = pl.pallas_call(
        _wkv_step_body,
        grid=(RWKV_HEADS,),
        in_specs=[s_spec] + [v_spec] * 6,
        out_specs=[v_spec, s_spec],
        out_shape=[jax.ShapeDtypeStruct((RWKV_HEADS, RWKV_HEAD, bsz), F32),
                   jax.ShapeDtypeStruct(s0.shape, F32)],
        compiler_params=_cparams(),
        name="wkv_step",
    )(s0, vec(r), vec(lw), vec(k), vec(kk), vec(bvec), vec(v))
    return o.reshape(D_MODEL, bsz).T, sn


def _rw_params(p):
    row = lambda a: a.reshape(1, D_MODEL)
    pad_c = lambda w: jnp.pad(w, ((0, 0), (0, LORA_PAD - w.shape[1]))).astype(BF16)
    pad_r = lambda w: jnp.pad(w, ((0, LORA_PAD - w.shape[0]), (0, 0))).astype(BF16)
    sel = (jnp.arange(D_MODEL)[:, None] // RWKV_HEAD == jnp.arange(LANES)[None, :]).astype(BF16)
    return dict(mu=p['rw_mu'], wrkv=p['rw_w_rkv'].astype(BF16), w0=row(p['rw_w0']),
                w1=pad_c(p['rw_w1']), w2=pad_r(p['rw_w2']), a0=row(p['rw_a0']),
                a1=pad_c(p['rw_a1']), a2=pad_r(p['rw_a2']), g1=pad_c(p['rw_g1']),
                g2=pad_r(p['rw_g2']), k_k=row(p['rw_k_k']), k_a=row(p['rw_k_a']),
                r_k=row(p['rw_r_k']), ln_w=row(p['rw_ln_w']), ln_b=row(p['rw_ln_b']),
                wo=p['rw_w_o'].astype(BF16), sel=sel, sel_t=sel.T)


def _pair_states(s):
    bsz = s.shape[0]
    s5 = s.reshape(bsz, RWKV_PAIRS, 2, RWKV_HEAD, RWKV_HEAD)
    eye = jnp.eye(2, dtype=s.dtype)
    bd = jnp.einsum('bpqij,qr->bpqirj', s5, eye)
    return bd.reshape(bsz, RWKV_PAIRS, LANES, LANES)


def _unpair_states(bd):
    bsz = bd.shape[0]
    s6 = bd.reshape(bsz, RWKV_PAIRS, 2, RWKV_HEAD, 2, RWKV_HEAD)
    s = jnp.stack([s6[:, :, 0, :, 0, :], s6[:, :, 1, :, 1, :]], axis=2)
    return s.reshape(bsz, RWKV_HEADS, RWKV_HEAD, RWKV_HEAD)


def _rwkv(x, g, prm, sh0, s0, *, B, L):
    T = min(L, WKV_CHUNK)
    r, cum, lw, k, v, kk, bvec, gt, sh_new = _rw_pre(x, g, prm, sh0, B=B, T=T)
    if L == 1:
        o, s_t = _wkv_step(jnp.transpose(s0, (1, 2, 3, 0)), r, lw, k, kk, bvec, v)
        s_new = jnp.transpose(s_t, (3, 0, 1, 2))
    else:
        o, s_bd = _wkv_chunk(r, cum, lw, k, v, kk, bvec, _pair_states(s0), B=B, T=T)
        s_new = _unpair_states(s_bd)
    y = _rw_post(x, o, r, k, v, gt, prm, B=B, T=T)
    return y, sh_new, s_new


def _trunk(x, st, w, *, B, L):
    TM = B * min(L, WKV_CHUNK)
    TM_FFN = B * min(L, FFN_STEPS)
    regroup = L > 1
    if not regroup:
        x = x.reshape(B, D_MODEL)
    row = lambda a: a.reshape(1, -1)
    new = {k: [] for k in ('s5_re', 's5_im', 'rw_wkv', 'rw_shift', 'lru_h', 'lru_conv', 'ffn_conv')}
    for i in range(DEPTH):
        kind, j = i % N_MIXERS, i // N_MIXERS
        g = row(w['norm_mix'][i])
        if kind == 0:
            x, hr, hi = _s5(x, g, w['s5'][j], st['s5_re'][j], st['s5_im'][j], B=B, TM=TM,
                            batch_major_in=(regroup and i == 0))
            new['s5_re'].append(hr)
            new['s5_im'].append(hi)
        elif kind == 1:
            x, sh, s = _rwkv(x, g, w['rw'][j], st['rw_shift'][j], st['rw_wkv'][j], B=B, L=L)
            new['rw_shift'].append(sh)
            new['rw_wkv'].append(s)
        else:
            x, cb, hl = _lru(x, g, w['lru'][j], st['lru_conv'][j], st['lru_h'][j], B=B, TM=TM)
            new['lru_conv'].append(cb)
            new['lru_h'].append(hl)
        x, cb = _ffn(x, row(w['norm_ffn'][i]), w['ffn_w_in'], w['ffn_conv_w'][i],
                     row(w['ffn_conv_b'][i]), w['ffn_w_out'], st['ffn_conv'][i],
                     row(w['norm_final']), layer=i, B=B, TM=TM_FFN, final_norm=(i == DEPTH - 1),
                     batch_major_out=(regroup and i == DEPTH - 1))
        new['ffn_conv'].append(cb)
    return x.reshape(B, L, D_MODEL), new


def _stack(parts):
    return parts[0][None] if len(parts) == 1 else jnp.stack(parts)


def _time_major_hist(buf):
    n, bsz, wm1, c = buf.shape
    t = jnp.transpose(buf, (0, 2, 1, 3)).reshape(n, wm1 * bsz, c)
    return [t[j] for j in range(n)]


def _batch_major_hist(rows, bsz):
    t = _stack(rows)
    n, _, c = t.shape
    return jnp.transpose(t.reshape(n, -1, bsz, c), (0, 2, 1, 3))


def _run(x, st, w):
    bsz, length, _ = x.shape
    flat = lambda a: [a[j].reshape(bsz, -1) for j in range(a.shape[0])]
    stt = dict(s5_re=flat(st['s5_re']), s5_im=flat(st['s5_im']),
               rw_wkv=[st['rw_wkv'][j] for j in range(st['rw_wkv'].shape[0])],
               rw_shift=flat(st['rw_shift']), lru_h=flat(st['lru_h']),
               lru_conv=_time_major_hist(st['lru_conv']), ffn_conv=_time_major_hist(st['ffn_conv']))
    y, new = _trunk(x, stt, w, B=bsz, L=length)
    n5 = len(new['s5_re'])
    out = dict(
        s5_re=_stack(new['s5_re']).reshape(n5, bsz, S5_GROUPS, S5_STATE),
        s5_im=_stack(new['s5_im']).reshape(n5, bsz, S5_GROUPS, S5_STATE),
        rw_wkv=_stack(new['rw_wkv']), rw_shift=_stack(new['rw_shift']),
        lru_h=_stack(new['lru_h']), lru_conv=_batch_major_hist(new['lru_conv'], bsz),
        ffn_conv=_batch_major_hist(new['ffn_conv'], bsz))
    return y, out


def _prepare_weights(w):
    n_s5 = w['s5_a_re'].shape[0]
    n_rw = w['rw_mu'].shape[0]
    n_lru = w['lru_w_in'].shape[0]
    sub = lambda prefix, j: {k: v[j] for k, v in w.items() if k.startswith(prefix)}
    return dict(
        norm_mix=w['norm_mix'], norm_ffn=w['norm_ffn'], norm_final=w['norm_final'],
        s5=[_s5_params(sub('s5_', j)) for j in range(n_s5)],
        rw=[_rw_params(sub('rw_', j)) for j in range(n_rw)],
        lru=[_lru_params(sub('lru_', j)) for j in range(n_lru)],
        ffn_w_in=w['ffn_w_in'].astype(BF16), ffn_conv_w=w['ffn_conv_w'],
        ffn_conv_b=w['ffn_conv_b'], ffn_w_out=w['ffn_w_out'].astype(BF16))


def kernel(x_prompt, x_sample, state_s5_re, state_s5_im, state_rwkv_wkv, state_rwkv_shift, state_lru_h, state_lru_conv, state_ffn_conv, norm_mix, norm_ffn, norm_final, s5_a_re, s5_a_im, s5_log_dt, s5_b_re, s5_b_im, s5_c_re, s5_c_im, s5_d, s5_w_glu, rw_mu, rw_w_rkv, rw_w0, rw_w1, rw_w2, rw_a0, rw_a1, rw_a2, rw_g1, rw_g2, rw_k_k, rw_k_a, rw_r_k, rw_ln_w, rw_ln_b, rw_w_o, lru_w_in, lru_conv_w, lru_conv_b, lru_w_rg, lru_b_rg, lru_w_ig, lru_b_ig, lru_lambda, lru_w_out, ffn_w_in, ffn_conv_w, ffn_conv_b, ffn_w_out):
    w = _prepare_weights(dict(
        norm_mix=norm_mix, norm_ffn=norm_ffn, norm_final=norm_final,
        s5_a_re=s5_a_re, s5_a_im=s5_a_im, s5_log_dt=s5_log_dt, s5_b_re=s5_b_re, s5_b_im=s5_b_im,
        s5_c_re=s5_c_re, s5_c_im=s5_c_im, s5_d=s5_d, s5_w_glu=s5_w_glu,
        rw_mu=rw_mu, rw_w_rkv=rw_w_rkv, rw_w0=rw_w0, rw_w1=rw_w1, rw_w2=rw_w2, rw_a0=rw_a0,
        rw_a1=rw_a1, rw_a2=rw_a2, rw_g1=rw_g1, rw_g2=rw_g2, rw_k_k=rw_k_k, rw_k_a=rw_k_a,
        rw_r_k=rw_r_k, rw_ln_w=rw_ln_w, rw_ln_b=rw_ln_b, rw_w_o=rw_w_o,
        lru_w_in=lru_w_in, lru_conv_w=lru_conv_w, lru_conv_b=lru_conv_b, lru_w_rg=lru_w_rg,
        lru_b_rg=lru_b_rg, lru_w_ig=lru_w_ig, lru_b_ig=lru_b_ig, lru_lambda=lru_lambda,
        lru_w_out=lru_w_out, ffn_w_in=ffn_w_in, ffn_conv_w=ffn_conv_w, ffn_conv_b=ffn_conv_b,
        ffn_w_out=ffn_w_out))
    bsz, dt = x_prompt.shape[0], x_prompt.dtype
    n_s5, n_rw, n_lru = state_s5_re.shape[0], state_rwkv_wkv.shape[0], state_lru_h.shape[0]
    st_prompt = dict(
        s5_re=jnp.zeros((n_s5, bsz, S5_GROUPS, S5_STATE), dt),
        s5_im=jnp.zeros((n_s5, bsz, S5_GROUPS, S5_STATE), dt),
        rw_wkv=jnp.zeros((n_rw, bsz, RWKV_HEADS, RWKV_HEAD, RWKV_HEAD), dt),
        rw_shift=jnp.zeros((n_rw, bsz, D_MODEL), dt),
        lru_h=jnp.zeros((n_lru, bsz, D_RNN), dt),
        lru_conv=jnp.zeros((n_lru, bsz, LRU_CONV - 1, D_RNN), dt),
        ffn_conv=jnp.zeros((DEPTH, bsz, FFN_CONV - 1, D_FF), dt))
    st_sample = dict(s5_re=state_s5_re, s5_im=state_s5_im, rw_wkv=state_rwkv_wkv,
                     rw_shift=state_rwkv_shift, lru_h=state_lru_h, lru_conv=state_lru_conv,
                     ffn_conv=state_ffn_conv)
    y_p, new_p = _run(x_prompt, st_prompt, w)
    y_s, new_s = _run(x_sample, st_sample, w)
    return (y_p, y_s, new_p['s5_re'], new_s['s5_re'], new_p['s5_im'], new_s['s5_im'],
            new_p['rw_wkv'], new_s['rw_wkv'], new_p['rw_shift'], new_s['rw_shift'],
            new_p['lru_h'], new_s['lru_h'], new_p['lru_conv'], new_s['lru_conv'],
            new_p['ffn_conv'], new_s['ffn_conv'])
```

```python
import functools

import jax
import jax.numpy as jnp
from jax import lax
from jax.experimental import pallas as pl
from jax.experimental.pallas import tpu as pltpu

F32 = jnp.float32
BF16 = jnp.bfloat16

D_MODEL = 1024
DEPTH = 4
N_MIXERS = 3
RMS_EPS = 1e-6

S5_GROUP = 16
S5_GROUPS = D_MODEL // S5_GROUP
S5_STATE = 64
S5_SLABS = 8
S5_SLAB_STATE = (S5_GROUPS // S5_SLABS) * S5_STATE
S5_NSTATE = S5_GROUPS * S5_STATE

RWKV_HEAD = 64
RWKV_HEADS = D_MODEL // RWKV_HEAD
RWKV_PAIRS = RWKV_HEADS // 2
RWKV_GN_EPS = 64e-5
LORA_PAD = 128

D_RNN = D_MODEL
LRU_BLOCKS = 4
LRU_BLOCK = D_RNN // LRU_BLOCKS
LRU_C = 8.0
LRU_CONV = 4

D_FF = 2816
FFN_CONV = 3
FFN_CHUNK = 256
FFN_NCHUNK = D_FF // FFN_CHUNK
FFN_STEPS = 64

LANES = 128
WKV_CHUNK = 64
WKV_ROWS = 4
VMEM_LIMIT = 60 * 1024 * 1024


def _cparams():
    return pltpu.CompilerParams(dimension_semantics=("arbitrary",), vmem_limit_bytes=VMEM_LIMIT)


def _const_spec(shape):
    nd = len(shape)
    return pl.BlockSpec(shape, lambda i, _n=nd: (0,) * _n, pipeline_mode=pl.Buffered(1))


def _row_spec(tm, width):
    return pl.BlockSpec((tm, width), lambda i: (i, 0))


def _rms(x, g):
    ms = jnp.mean(x * x, axis=-1, keepdims=True)
    return x * lax.rsqrt(ms + RMS_EPS) * g


def _bdot(a, w):
    return jnp.dot(a.astype(BF16), w, preferred_element_type=F32)


def _softplus(z):
    return jnp.maximum(z, 0.0) + jnp.log1p(jnp.exp(-jnp.abs(z)))


def _sigmoid(x):
    return 0.5 * jnp.tanh(0.5 * x) + 0.5


def _head_sum(x, sel, sel_t):
    return _bdot(_bdot(x, sel), sel_t)


def _load_rows(x_ref, stage, *, B, TM):
    if not stage:
        return x_ref[...]
    T = TM // B
    for j in range(D_MODEL // LANES):
        for b in range(B):
            stage[0][j, pl.ds(b, T, stride=B), :] = x_ref[b, :, j * LANES:(j + 1) * LANES]
    return jnp.concatenate([stage[0][j] for j in range(D_MODEL // LANES)], axis=1)


def _store_rows(o_ref, stage, y, *, B, TM):
    if not stage:
        o_ref[...] = y
        return
    T = TM // B
    for j in range(D_MODEL // LANES):
        stage[0][j] = y[:, j * LANES:(j + 1) * LANES]
        for b in range(B):
            o_ref[b, :, j * LANES:(j + 1) * LANES] = stage[0][j, pl.ds(b, T, stride=B), :]


def _stage_scratch(tm, batch_major):
    return [pltpu.VMEM((D_MODEL // LANES, tm, LANES), F32)] if batch_major else []


def _rows_or_batch_spec(tm, B, batch_major):
    if batch_major:
        return pl.BlockSpec((B, tm // B, D_MODEL), lambda i: (0, i, 0))
    return _row_spec(tm, D_MODEL)


def _ffn_body(x_ref, g_ref, win_ref, cw_ref, cb_ref, wout_ref, c0_ref, gf_ref,
              o_ref, cnew_ref, carry_ref, *stage, B, TM, final_norm):
    hist = (FFN_CONV - 1) * B

    @pl.when(pl.program_id(0) == 0)
    def _():
        carry_ref[...] = c0_ref[...]

    x = x_ref[...]
    h = _rms(x, g_ref[...]).astype(BF16)
    acc = jnp.zeros((TM, D_MODEL), F32)
    for c in range(FFN_NCHUNK):
        lo = c * FFN_CHUNK
        gate = jnp.dot(h, win_ref[:, lo:lo + FFN_CHUNK], preferred_element_type=F32)
        up = jnp.dot(h, win_ref[:, D_FF + lo:D_FF + lo + FFN_CHUNK], preferred_element_type=F32)
        ext = jnp.concatenate([carry_ref[:, lo:lo + FFN_CHUNK], gate], axis=0)
        conv = cb_ref[:, lo:lo + FFN_CHUNK] + ext[0:TM] * cw_ref[0:1, lo:lo + FFN_CHUNK]
        for k in range(1, FFN_CONV):
            conv = conv + ext[k * B:k * B + TM] * cw_ref[k:k + 1, lo:lo + FFN_CHUNK]
        carry_ref[:, lo:lo + FFN_CHUNK] = ext[TM:TM + hist]
        act = (conv * _sigmoid(conv)) * up
        acc = acc + jnp.dot(act.astype(BF16), wout_ref[lo:lo + FFN_CHUNK, :],
                            preferred_element_type=F32)
    y = x + acc
    if final_norm:
        y = _rms(y, gf_ref[...])
    _store_rows(o_ref, stage, y, B=B, TM=TM)
    cnew_ref[...] = carry_ref[...]


def _layer_spec(shape, layer):
    nd = len(shape) - 1
    return pl.BlockSpec((None,) + tuple(shape[1:]), lambda i, _l=layer, _n=nd: (_l,) + (0,) * _n,
                        pipeline_mode=pl.Buffered(1))


def _ffn(x, g, win_all, cw, cb, wout_all, c0, gf, *, layer, B, TM, final_norm, batch_major_out):
    n = x.shape[0]
    out_rows = (jax.ShapeDtypeStruct((B, n // B, D_MODEL), F32) if batch_major_out
                else jax.ShapeDtypeStruct((n, D_MODEL), F32))
    hist = (FFN_CONV - 1) * B
    body = functools.partial(_ffn_body, B=B, TM=TM, final_norm=final_norm)
    return pl.pallas_call(
        body,
        grid=(n // TM,),
        in_specs=[_row_spec(TM, D_MODEL), _const_spec((1, D_MODEL)), _layer_spec(win_all.shape, layer),
                  _const_spec(cw.shape), _const_spec(cb.shape), _layer_spec(wout_all.shape, layer),
                  _const_spec(c0.shape), _const_spec((1, D_MODEL))],
        out_specs=[_rows_or_batch_spec(TM, B, batch_major_out), _const_spec((hist, D_FF))],
        out_shape=[out_rows, jax.ShapeDtypeStruct((hist, D_FF), F32)],
        scratch_shapes=[pltpu.VMEM((hist, D_FF), F32)] + _stage_scratch(TM, batch_major_out),
        compiler_params=_cparams(),
        name="conv_ffn",
    )(x, g, win_all, cw, cb, wout_all, c0, gf)


def _s5_body(x_ref, g_ref, wbr_ref, wbi_ref, cfr_ref, cfi_ref, abr_ref, abi_ref, wcr_ref, wci_ref,
             d_ref, wglu_ref, h0r_ref, h0i_ref, o_ref, hr_out, hi_out,
             xr_s, xi_s, hr_s, hi_s, y_s, *stage, B, TM):
    @pl.when(pl.program_id(0) == 0)
    def _():
        cr, ci = cfr_ref[...], cfi_ref[...]
        h0r, h0i = h0r_ref[...], h0i_ref[...]
        den = cr * cr + ci * ci
        hr_s[...] = (h0r * cr + h0i * ci) / den
        hi_s[...] = (h0i * cr - h0r * ci) / den

    x = _load_rows(x_ref, stage, B=B, TM=TM)
    u = _rms(x, g_ref[...])
    ub = u.astype(BF16)
    steps = TM // B
    for s in range(S5_SLABS):
        sl = slice(s * S5_SLAB_STATE, (s + 1) * S5_SLAB_STATE)
        us = ub[:, s * LANES:(s + 1) * LANES]
        xr_s[s] = jnp.dot(us, wbr_ref[s], preferred_element_type=F32)
        xi_s[s] = jnp.dot(us, wbi_ref[s], preferred_element_type=F32)
        ar = jnp.broadcast_to(abr_ref[:, sl], (B, S5_SLAB_STATE))
        ai = jnp.broadcast_to(abi_ref[:, sl], (B, S5_SLAB_STATE))

        hr, hi = hr_s[:, sl], hi_s[:, sl]
        for t in range(steps):
            rows = slice(t * B, (t + 1) * B)
            hr, hi = ((ar * hr - ai * hi) + xr_s[s, rows, :],
                      (ar * hi + ai * hr) + xi_s[s, rows, :])
            xr_s[s, rows, :] = hr
            xi_s[s, rows, :] = hi
        hr_s[:, sl] = hr
        hi_s[:, sl] = hi
        y_s[:, s * LANES:(s + 1) * LANES] = (
            jnp.dot(xr_s[s].astype(BF16), wcr_ref[s], preferred_element_type=F32)
            - jnp.dot(xi_s[s].astype(BF16), wci_ref[s], preferred_element_type=F32))
    y = y_s[...] + d_ref[...] * u
    z = _bdot(jax.nn.gelu(y), wglu_ref[...])
    o_ref[...] = x + z[:, :D_MODEL] * _sigmoid(z[:, D_MODEL:])
    cr, ci = cfr_ref[...], cfi_ref[...]
    hr_out[...] = cr * hr_s[...] - ci * hi_s[...]
    hi_out[...] = cr * hi_s[...] + ci * hr_s[...]


def _s5(x, g, prm, wglu_all, h0r, h0i, *, layer, B, TM, batch_major_in):
    n = x.shape[0] * x.shape[1] if batch_major_in else x.shape[0]
    body = functools.partial(_s5_body, B=B, TM=TM)
    consts = [g, prm['wbr'], prm['wbi'], prm['cfr'], prm['cfi'], prm['abr'], prm['abi'],
              prm['wcr'], prm['wci'], prm['d'], wglu_all, h0r, h0i]
    specs = [_layer_spec(a.shape, layer) if a is wglu_all else _const_spec(a.shape) for a in consts]
    return pl.pallas_call(
        body,
        grid=(n // TM,),
        in_specs=[_rows_or_batch_spec(TM, B, batch_major_in)] + specs,
        out_specs=[_row_spec(TM, D_MODEL), _const_spec((B, S5_NSTATE)), _const_spec((B, S5_NSTATE))],
        out_shape=[jax.ShapeDtypeStruct((n, D_MODEL), F32),
                   jax.ShapeDtypeStruct((B, S5_NSTATE), F32),
                   jax.ShapeDtypeStruct((B, S5_NSTATE), F32)],
        scratch_shapes=[pltpu.VMEM((S5_SLABS, TM, S5_SLAB_STATE), F32),
                        pltpu.VMEM((S5_SLABS, TM, S5_SLAB_STATE), F32),
                        pltpu.VMEM((B, S5_NSTATE), F32), pltpu.VMEM((B, S5_NSTATE), F32),
                        pltpu.VMEM((TM, D_MODEL), F32)] + _stage_scratch(TM, batch_major_in),
        compiler_params=_cparams(),
        name="s5_mixer",
    )(x, *consts)


def _s5_params(p):
    lam_re = jnp.minimum(p['s5_a_re'], -1e-4)
    lam_im = p['s5_a_im']
    dt = jnp.exp(p['s5_log_dt'])[:, None]
    mag = jnp.exp(lam_re * dt)
    ab_re = mag * jnp.cos(lam_im * dt)
    ab_im = mag * jnp.sin(lam_im * dt)
    den = lam_re * lam_re + lam_im * lam_im
    coef_re = ((ab_re - 1.0) * lam_re + ab_im * lam_im) / den
    coef_im = (ab_im * lam_re - (ab_re - 1.0) * lam_im) / den
    gps = S5_GROUPS // S5_SLABS
    same_group = (jnp.arange(gps * S5_GROUP)[:, None] // S5_GROUP
                  == jnp.arange(gps * S5_STATE)[None, :] // S5_STATE)

    def slab_in(b):
        rows = jnp.transpose(b.reshape(S5_SLABS, gps, S5_STATE, S5_GROUP), (0, 1, 3, 2))
        rows = rows.reshape(S5_SLABS, gps * S5_GROUP, S5_STATE)
        return jnp.where(same_group, jnp.tile(rows, (1, 1, gps)), 0.0).astype(BF16)

    def slab_out(c):
        cols = jnp.transpose(c.reshape(S5_SLABS, gps, S5_GROUP, S5_STATE), (0, 3, 1, 2))
        cols = cols.reshape(S5_SLABS, S5_STATE, gps * S5_GROUP)
        return jnp.where(same_group.T, jnp.tile(cols, (1, gps, 1)), 0.0).astype(BF16)

    flat = lambda a: a.reshape(1, S5_NSTATE)
    c_re, c_im = p['s5_c_re'], p['s5_c_im']
    cc_re = c_re * coef_re[:, None, :] - c_im * coef_im[:, None, :]
    cc_im = c_re * coef_im[:, None, :] + c_im * coef_re[:, None, :]
    return dict(wbr=slab_in(p['s5_b_re']), wbi=slab_in(p['s5_b_im']),
                wcr=slab_out(cc_re), wci=slab_out(cc_im),
                cfr=flat(coef_re), cfi=flat(coef_im), abr=flat(ab_re), abi=flat(ab_im),
                d=p['s5_d'].reshape(1, D_MODEL))


def _lru_body(x_ref, g_ref, win_ref, cw_ref, cb_ref, wrg_ref, brg_ref, wig_ref, big_ref, lam_ref,
              wout_ref, c0_ref, h0_ref, o_ref, cnew_ref, hnew_ref,
              carry_s, h_s, a_s, bx_s, *, B, TM):
    hist = (LRU_CONV - 1) * B

    @pl.when(pl.program_id(0) == 0)
    def _():
        carry_s[...] = c0_ref[...]
        h_s[...] = h0_ref[...]

    x = x_ref[...]
    xn = _rms(x, g_ref[...])
    gy = _bdot(xn, win_ref[...])
    gate_br = jax.nn.gelu(gy[:, :D_RNN])
    ext = jnp.concatenate([carry_s[...], gy[:, D_RNN:]], axis=0)
    u = cb_ref[...] + ext[0:TM] * cw_ref[0:1, :]
    for k in range(1, LRU_CONV):
        u = u + ext[k * B:k * B + TM] * cw_ref[k:k + 1, :]
    carry_s[...] = ext[TM:TM + hist]
    ub = u.astype(BF16)
    rg_parts, ig_parts = [], []
    for nb in range(LRU_BLOCKS):
        blk = ub[:, nb * LRU_BLOCK:(nb + 1) * LRU_BLOCK]
        rg_parts.append(jnp.dot(blk, wrg_ref[nb], preferred_element_type=F32))
        ig_parts.append(jnp.dot(blk, wig_ref[nb], preferred_element_type=F32))
    rg = _sigmoid(jnp.concatenate(rg_parts, axis=1) + brg_ref[...])
    ig = _sigmoid(jnp.concatenate(ig_parts, axis=1) + big_ref[...])
    log_sig = -_softplus(-lam_ref[...])
    log_a = LRU_C * rg * log_sig
    a = jnp.exp(log_a)
    a_s[...] = a
    bx_s[...] = jnp.sqrt(1.0 - a * a) * ig * u

    def step(t, h):
        rows = pl.ds(pl.multiple_of(t * B, B), B)
        h = a_s[rows, :] * h + bx_s[rows, :]
        bx_s[rows, :] = h
        return h

    h_last = lax.fori_loop(0, TM // B, step, h_s[...])
    h_s[...] = h_last
    o_ref[...] = x + _bdot(bx_s[...] * gate_br, wout_ref[...])
    cnew_ref[...] = carry_s[...]
    hnew_ref[...] = h_last


def _lru(x, g, prm, c0, h0, *, B, TM):
    n = x.shape[0]
    hist = (LRU_CONV - 1) * B
    body = functools.partial(_lru_body, B=B, TM=TM)
    consts = [g, prm['win'], prm['cw'], prm['cb'], prm['wrg'], prm['brg'], prm['wig'], prm['big'],
              prm['lam'], prm['wout'], c0, h0]
    return pl.pallas_call(
        body,
        grid=(n // TM,),
        in_specs=[_row_spec(TM, D_MODEL)] + [_const_spec(a.shape) for a in consts],
        out_specs=[_row_spec(TM, D_MODEL), _const_spec((hist, D_RNN)), _const_spec((B, D_RNN))],
        out_shape=[jax.ShapeDtypeStruct((n, D_MODEL), F32),
                   jax.ShapeDtypeStruct((hist, D_RNN), F32),
                   jax.ShapeDtypeStruct((B, D_RNN), F32)],
        scratch_shapes=[pltpu.VMEM((hist, D_RNN), F32), pltpu.VMEM((B, D_RNN), F32),
                        pltpu.VMEM((TM, D_RNN), F32), pltpu.VMEM((TM, D_RNN), F32)],
        compiler_params=_cparams(),
        name="rglru_mixer",
    )(x, *consts)


def _lru_params(p):
    row = lambda a: a.reshape(1, D_RNN)
    return dict(win=p['lru_w_in'].astype(BF16), cw=p['lru_conv_w'], cb=row(p['lru_conv_b']),
                wrg=p['lru_w_rg'].astype(BF16), brg=row(p['lru_b_rg']),
                wig=p['lru_w_ig'].astype(BF16), big=row(p['lru_b_ig']),
                lam=row(p['lru_lambda']), wout=p['lru_w_out'].astype(BF16))


NCOL = D_MODEL // LANES


def _emit_cols(o_ref, val):
    for j in range(NCOL):
        o_ref[j] = val[:, j * LANES:(j + 1) * LANES]


def _load_cols(ref):
    return jnp.concatenate([ref[j] for j in range(NCOL)], axis=1)


def _rw_pre_body(x_ref, g_ref, mu_ref, wrkv_ref, w0_ref, w1_ref, w2_ref, a0_ref, a1_ref, a2_ref,
                 g1_ref, g2_ref, kk_ref, ka_ref, sel_ref, selt_ref, sh0_ref,
                 r_o, cum_o, lw_o, k_o, v_o, kk_o, b_o, g_o, sh_o, sh_s, *, B, T):
    TM = T * B

    @pl.when(pl.program_id(0) == 0)
    def _():
        sh_s[...] = sh0_ref[...]

    def emit(o_ref, val):
        if T == 1:
            o_ref[...] = val
        else:
            _emit_cols(o_ref, val)

    xn = _rms(x_ref[...], g_ref[...])
    if TM > B:
        prev = jnp.concatenate([sh_s[...], xn[:TM - B]], axis=0)
    else:
        prev = sh_s[...]
    sh_s[...] = xn[TM - B:]
    xx = prev - xn
    mix = lambda n: xn + xx * mu_ref[n:n + 1, :]
    emit(r_o, _bdot(mix(0), wrkv_ref[0]))
    emit(v_o, _bdot(mix(2), wrkv_ref[2]))
    emit(g_o, _bdot(_sigmoid(_bdot(mix(5), g1_ref[...])), g2_ref[...]))
    wl = w0_ref[...] + _bdot(jnp.tanh(_bdot(mix(3), w1_ref[...])), w2_ref[...])
    lw = -jnp.exp(-_softplus(-wl) - 0.5)
    emit(lw_o, lw)
    cum = lw
    sh = B
    while sh < TM:
        cum = cum + jnp.concatenate([jnp.zeros((sh, D_MODEL), F32), cum[:TM - sh]], axis=0)
        sh *= 2
    emit(cum_o, cum)
    a = _sigmoid(a0_ref[...] + _bdot(_bdot(mix(4), a1_ref[...]), a2_ref[...]))
    k = _bdot(mix(1), wrkv_ref[1])
    kk = k * kk_ref[...]
    n2 = _head_sum(kk * kk, sel_ref[...], selt_ref[...])
    kk = kk * lax.rsqrt(jnp.maximum(n2, 1e-24))
    emit(kk_o, kk)
    emit(b_o, kk * a)
    emit(k_o, k * (1.0 + (a - 1.0) * ka_ref[...]))
    sh_o[...] = sh_s[...]


def _cols_spec(tm):
    return pl.BlockSpec((NCOL, tm, LANES), lambda i: (0, i, 0))


def _rw_pre(x, g, prm, sh0, *, B, T):
    n = x.shape[0]
    tm = T * B
    body = functools.partial(_rw_pre_body, B=B, T=T)
    consts = [g, prm['mu'], prm['wrkv'], prm['w0'], prm['w1'], prm['w2'], prm['a0'], prm['a1'],
              prm['a2'], prm['g1'], prm['g2'], prm['k_k'], prm['k_a'], prm['sel'], prm['sel_t'], sh0]
    if T == 1:
        big, big_spec = jax.ShapeDtypeStruct((n, D_MODEL), F32), _row_spec(tm, D_MODEL)
    else:
        big, big_spec = jax.ShapeDtypeStruct((NCOL, n, LANES), F32), _cols_spec(tm)
    return pl.pallas_call(
        body,
        grid=(n // tm,),
        in_specs=[_row_spec(tm, D_MODEL)] + [_const_spec(a.shape) for a in consts],
        out_specs=[big_spec] * 8 + [_const_spec((B, D_MODEL))],
        out_shape=[big] * 8 + [jax.ShapeDtypeStruct((B, D_MODEL), F32)],
        scratch_shapes=[pltpu.VMEM((B, D_MODEL), F32)],
        compiler_params=_cparams(),
        name="rwkv_project",
    )(x, *consts)


def _rw_post_body(x_ref, o_ref, r_ref, k_ref, v_ref, g_ref, rk_ref, lnw_ref, lnb_ref, sel_ref,
                  selt_ref, wo_ref, out_ref, *, T):
    load = (lambda ref: ref[...]) if T == 1 else _load_cols
    o, r, k, v, g = (load(ref) for ref in (o_ref, r_ref, k_ref, v_ref, g_ref))
    hsum = functools.partial(_head_sum, sel=sel_ref[...], sel_t=selt_ref[...])
    mean = hsum(o) * (1.0 / RWKV_HEAD)
    dlt = o - mean
    var = hsum(dlt * dlt) * (1.0 / RWKV_HEAD)
    on = dlt * lax.rsqrt(var + RWKV_GN_EPS) * lnw_ref[...] + lnb_ref[...]
    bonus = hsum(r * k * rk_ref[...]) * v
    out_ref[...] = x_ref[...] + _bdot((on + bonus) * g, wo_ref[...])


def _rw_post(x, o, r, k, v, g, prm, *, B, T):
    n = x.shape[0]
    tm = T * B
    consts = [prm['r_k'], prm['ln_w'], prm['ln_b'], prm['sel'], prm['sel_t'], prm['wo']]
    big_spec = _row_spec(tm, D_MODEL) if T == 1 else _cols_spec(tm)
    return pl.pallas_call(
        functools.partial(_rw_post_body, T=T),
        grid=(n // tm,),
        in_specs=([_row_spec(tm, D_MODEL)] + [big_spec] * 5
                  + [_const_spec(a.shape) for a in consts]),
        out_specs=_row_spec(tm, D_MODEL),
        out_shape=jax.ShapeDtypeStruct((n, D_MODEL), F32),
        compiler_params=_cparams(),
        name="rwkv_output",
    )(x, o, r, k, v, g, *consts)


def _wkv_chunk_body(r_ref, cum_ref, lw_ref, k_ref, v_ref, kk_ref, b_ref, s0_ref, o_ref, sT_ref,
                    S_s, *, B, T):
    @pl.when(pl.program_id(0) == 0)
    def _():
        S_s[...] = s0_ref[...]

    lane = lax.broadcasted_iota(jnp.int32, (T, LANES), 1)
    trow = lax.broadcasted_iota(jnp.int32, (T, LANES), 0)
    first = lane < RWKV_HEAD
    strict = trow > (lane & (T - 1))
    incl = trow >= (lane & (T - 1))
    ri = lax.broadcasted_iota(jnp.int32, (LANES, LANES), 0)
    ci = lax.broadcasted_iota(jnp.int32, (LANES, LANES), 1)
    same_head = (ri < RWKV_HEAD) == (ci < RWKV_HEAD)

    def stack(y):
        return jnp.concatenate([jnp.where(first, y, 0.0), jnp.where(first, 0.0, y)],
                               axis=0).astype(BF16)

    nt = (((1,), (1,)), ((), ()))
    tn = (((0,), (0,)), ((), ()))
    nsteps = T.bit_length() - 1

    def per_group(g, carry):
        chains = [(g * WKV_ROWS + i, hp) for i in range(WKV_ROWS) for hp in range(RWKV_PAIRS)]
        idx = range(len(chains))
        at_chain = [(hp, pl.ds(b, T, stride=B), slice(None)) for b, hp in chains]
        s_old = [S_s[b, hp] for b, hp in chains]
        ks, vs, bbs, ends, e_ends, ps, sps = [], [], [], [], [], [], []
        for c in idx:
            at_c = at_chain[c]
            cm = cum_ref[at_c]
            k = k_ref[at_c]
            bb = b_ref[at_c]
            mid = cm[T // 2 - 1:T // 2, :]
            end = cm[T - 1:T, :]
            at = -kk_ref[at_c] * jnp.exp((cm - lw_ref[at_c]) - mid)
            rt = r_ref[at_c] * jnp.exp(cm - mid)
            e_neg = jnp.exp(mid - cm)
            lhs_f = jnp.concatenate([at, rt], axis=0)
            rhs = jnp.concatenate([stack(bb * e_neg), stack(k * e_neg)], axis=0)
            ps.append(lax.dot_general(lhs_f.astype(BF16), rhs, nt, preferred_element_type=F32))
            sps.append(lax.dot_general((lhs_f * jnp.exp(mid)).astype(BF16), s_old[c].astype(BF16),
                                       nt, preferred_element_type=F32))
            ks.append(k)
            vs.append(v_ref[at_c])
            bbs.append(bb)
            ends.append(end)
            e_ends.append(jnp.exp(end - cm))
        vst = [stack(v) for v in vs]
        ys = [sps[c][0:T] + jnp.dot(jnp.where(strict, ps[c][0:T, 2 * T:4 * T], 0.0).astype(BF16),
                                    vst[c], preferred_element_type=F32) for c in idx]
        ms = [jnp.where(strict, ps[c][0:T, 0:2 * T], 0.0) for c in idx]
        for it in range(nsteps):
            if it + 1 < nsteps:
                both = [jnp.dot(ms[c].astype(BF16),
                                jnp.concatenate([stack(ys[c]), stack(ms[c])], axis=1),
                                preferred_element_type=F32) for c in idx]
                ys = [ys[c] + both[c][:, 0:LANES] for c in idx]
                ms = [both[c][:, LANES:2 * LANES] for c in idx]
            else:
                ys = [ys[c] + jnp.dot(ms[c].astype(BF16), stack(ys[c]), preferred_element_type=F32)
                      for c in idx]
        outs, news = [], []
        for c in idx:
            rbk = jnp.where(jnp.concatenate([incl, incl], axis=1), ps[c][T:2 * T, :], 0.0)
            outs.append(sps[c][T:2 * T]
                        + jnp.dot(rbk.astype(BF16), jnp.concatenate([stack(ys[c]), vst[c]], axis=0),
                                  preferred_element_type=F32))
            upd = lax.dot_general(
                jnp.concatenate([ys[c], vs[c]], axis=0).astype(BF16),
                jnp.concatenate([bbs[c] * e_ends[c], ks[c] * e_ends[c]], axis=0).astype(BF16),
                tn, preferred_element_type=F32)
            news.append(s_old[c] * jnp.exp(ends[c]) + jnp.where(same_head, upd, 0.0))
        for c in idx:
            o_ref[at_chain[c]] = outs[c]
            S_s[chains[c][0], chains[c][1]] = news[c]
        return carry

    lax.fori_loop(0, B // WKV_ROWS, per_group, 0)

    @pl.when(pl.program_id(0) == pl.num_programs(0) - 1)
    def _():
        for b in range(B):
            for hp in range(RWKV_PAIRS):
                blk = S_s[b, hp]
                sT_ref[b, 2 * hp] = blk[0:RWKV_HEAD, 0:RWKV_HEAD]
                sT_ref[b, 2 * hp + 1] = pltpu.roll(blk, RWKV_HEAD, axis=1)[RWKV_HEAD:, 0:RWKV_HEAD]


def _wkv_chunk(r, cum, lw, k, v, kk, bvec, s0, *, B, T):
    n = r.shape[1]
    tm = T * B
    body = functools.partial(_wkv_chunk_body, B=B, T=T)
    st_shape = (B, RWKV_PAIRS, LANES, LANES)
    head_shape = (B, RWKV_HEADS, RWKV_HEAD, RWKV_HEAD)
    return pl.pallas_call(
        body,
        grid=(n // tm,),
        in_specs=[_cols_spec(tm)] * 7 + [_const_spec(st_shape)],
        out_specs=[_cols_spec(tm), _const_spec(head_shape)],
        out_shape=[jax.ShapeDtypeStruct((NCOL, n, LANES), F32),
                   jax.ShapeDtypeStruct(head_shape, F32)],
        scratch_shapes=[pltpu.VMEM(st_shape, F32)],
        compiler_params=_cparams(),
        name="wkv_chunked",
    )(r, cum, lw, k, v, kk, bvec, s0)


def _wkv_step_body(s_ref, r_ref, lw_ref, k_ref, kk_ref, b_ref, v_ref, o_ref, sn_ref):
    w = jnp.exp(lw_ref[...])
    kk, bv, k, r = kk_ref[...], b_ref[...], k_ref[...], r_ref[...]

    def per_value_row(i, carry):
        s = s_ref[i]
        sa = -jnp.sum(s * kk, axis=0, keepdims=True)
        sn = s * w + sa * bv + v_ref[pl.ds(i, 1), :] * k
        sn_ref[i] = sn
        o_ref[pl.ds(i, 1), :] = jnp.sum(sn * r, axis=0, keepdims=True)
        return carry

    lax.fori_loop(0, RWKV_HEAD, per_value_row, 0)


def _wkv_step(s0, r, lw, k, kk, bvec, v):
    bsz = s0.shape[-1]
    vec = lambda t: t.T.reshape(RWKV_HEADS, RWKV_HEAD, bsz)
    s_spec = pl.BlockSpec((None, RWKV_HEAD, RWKV_HEAD, bsz), lambda h: (h, 0, 0, 0))
    v_spec = pl.BlockSpec((None, RWKV_HEAD, bsz), lambda h: (h, 0, 0))
    o, sn = pl.pallas_call(
        _wkv_step_body,
        grid=(RWKV_HEADS,),
        in_specs=[s_spec] + [v_spec] * 6,
        out_specs=[v_spec, s_spec],
        out_shape=[jax.ShapeDtypeStruct((RWKV_HEADS, RWKV_HEAD, bsz), F32),
                   jax.ShapeDtypeStruct(s0.shape, F32)],
        compiler_params=_cparams(),
        name="wkv_step",
    )(s0, vec(r), vec(lw), vec(k), vec(kk), vec(bvec), vec(v))
    return o.reshape(D_MODEL, bsz).T, sn


def _rw_params(p):
    row = lambda a: a.reshape(1, D_MODEL)
    pad_c = lambda w: jnp.pad(w, ((0, 0), (0, LORA_PAD - w.shape[1]))).astype(BF16)
    pad_r = lambda w: jnp.pad(w, ((0, LORA_PAD - w.shape[0]), (0, 0))).astype(BF16)
    sel = (jnp.arange(D_MODEL)[:, None] // RWKV_HEAD == jnp.arange(LANES)[None, :]).astype(BF16)
    return dict(mu=p['rw_mu'], wrkv=p['rw_w_rkv'].astype(BF16), w0=row(p['rw_w0']),
                w1=pad_c(p['rw_w1']), w2=pad_r(p['rw_w2']), a0=row(p['rw_a0']),
                a1=pad_c(p['rw_a1']), a2=pad_r(p['rw_a2']), g1=pad_c(p['rw_g1']),
                g2=pad_r(p['rw_g2']), k_k=row(p['rw_k_k']), k_a=row(p['rw_k_a']),
                r_k=row(p['rw_r_k']), ln_w=row(p['rw_ln_w']), ln_b=row(p['rw_ln_b']),
                wo=p['rw_w_o'].astype(BF16), sel=sel, sel_t=sel.T)


def _pair_states(s):
    bsz = s.shape[0]
    s5 = s.reshape(bsz, RWKV_PAIRS, 2, RWKV_HEAD, RWKV_HEAD)
    eye = jnp.eye(2, dtype=s.dtype)
    bd = jnp.einsum('bpqij,qr->bpqirj', s5, eye)
    return bd.reshape(bsz, RWKV_PAIRS, LANES, LANES)


def _rwkv(x, g, prm, sh0, s0, *, B, L):
    T = min(L, WKV_CHUNK)
    r, cum, lw, k, v, kk, bvec, gt, sh_new = _rw_pre(x, g, prm, sh0, B=B, T=T)
    if L == 1:
        o, s_t = _wkv_step(jnp.transpose(s0, (1, 2, 3, 0)), r, lw, k, kk, bvec, v)
        s_new = jnp.transpose(s_t, (3, 0, 1, 2))
    else:
        o, s_new = _wkv_chunk(r, cum, lw, k, v, kk, bvec, _pair_states(s0), B=B, T=T)
    y = _rw_post(x, o, r, k, v, gt, prm, B=B, T=T)
    return y, sh_new, s_new


def _trunk(x, st, w, *, B, L):
    TM = B * min(L, WKV_CHUNK)
    TM_FFN = B * min(L, FFN_STEPS)
    regroup = L > 1
    if not regroup:
        x = x.reshape(B, D_MODEL)
    row = lambda a: a.reshape(1, -1)
    new = {k: [] for k in ('s5_re', 's5_im', 'rw_wkv', 'rw_shift', 'lru_h', 'lru_conv', 'ffn_conv')}
    for i in range(DEPTH):
        kind, j = i % N_MIXERS, i // N_MIXERS
        g = row(w['norm_mix'][i])
        if kind == 0:
            x, hr, hi = _s5(x, g, w['s5'][j], w['s5_w_glu'], st['s5_re'][j], st['s5_im'][j],
                            layer=j, B=B, TM=TM, batch_major_in=(regroup and i == 0))
            new['s5_re'].append(hr)
            new['s5_im'].append(hi)
        elif kind == 1:
            x, sh, s = _rwkv(x, g, w['rw'][j], st['rw_shift'][j], st['rw_wkv'][j], B=B, L=L)
            new['rw_shift'].append(sh)
            new['rw_wkv'].append(s)
        else:
            x, cb, hl = _lru(x, g, w['lru'][j], st['lru_conv'][j], st['lru_h'][j], B=B, TM=TM)
            new['lru_conv'].append(cb)
            new['lru_h'].append(hl)
        x, cb = _ffn(x, row(w['norm_ffn'][i]), w['ffn_w_in'], w['ffn_conv_w'][i],
                     row(w['ffn_conv_b'][i]), w['ffn_w_out'], st['ffn_conv'][i],
                     row(w['norm_final']), layer=i, B=B, TM=TM_FFN, final_norm=(i == DEPTH - 1),
                     batch_major_out=(regroup and i == DEPTH - 1))
        new['ffn_conv'].append(cb)
    return x.reshape(B, L, D_MODEL), new


def _stack(parts):
    return parts[0][None] if len(parts) == 1 else jnp.stack(parts)


def _time_major_hist(buf):
    n, bsz, wm1, c = buf.shape
    t = jnp.transpose(buf, (0, 2, 1, 3)).reshape(n, wm1 * bsz, c)
    return [t[j] for j in range(n)]


def _batch_major_hist(rows, bsz):
    t = _stack(rows)
    n, _, c = t.shape
    return jnp.transpose(t.reshape(n, -1, bsz, c), (0, 2, 1, 3))


def _run(x, st, w):
    bsz, length, _ = x.shape
    flat = lambda a: [a[j].reshape(bsz, -1) for j in range(a.shape[0])]
    stt = dict(s5_re=flat(st['s5_re']), s5_im=flat(st['s5_im']),
               rw_wkv=[st['rw_wkv'][j] for j in range(st['rw_wkv'].shape[0])],
               rw_shift=flat(st['rw_shift']), lru_h=flat(st['lru_h']),
               lru_conv=_time_major_hist(st['lru_conv']), ffn_conv=_time_major_hist(st['ffn_conv']))
    y, new = _trunk(x, stt, w, B=bsz, L=length)
    n5 = len(new['s5_re'])
    out = dict(
        s5_re=_stack(new['s5_re']).reshape(n5, bsz, S5_GROUPS, S5_STATE),
        s5_im=_stack(new['s5_im']).reshape(n5, bsz, S5_GROUPS, S5_STATE),
        rw_wkv=_stack(new['rw_wkv']), rw_shift=_stack(new['rw_shift']),
        lru_h=_stack(new['lru_h']), lru_conv=_batch_major_hist(new['lru_conv'], bsz),
        ffn_conv=_batch_major_hist(new['ffn_conv'], bsz))
    return y, out


def _prepare_weights(w):
    n_s5 = w['s5_a_re'].shape[0]
    n_rw = w['rw_mu'].shape[0]
    n_lru = w['lru_w_in'].shape[0]
    sub = lambda prefix, j: {k: v[j] for k, v in w.items() if k.startswith(prefix)}
    return dict(
        norm_mix=w['norm_mix'], norm_ffn=w['norm_ffn'], norm_final=w['norm_final'],
        s5=[_s5_params(sub('s5_', j)) for j in range(n_s5)], s5_w_glu=w['s5_w_glu'].astype(BF16),
        rw=[_rw_params(sub('rw_', j)) for j in range(n_rw)],
        lru=[_lru_params(sub('lru_', j)) for j in range(n_lru)],
        ffn_w_in=w['ffn_w_in'].astype(BF16), ffn_conv_w=w['ffn_conv_w'],
        ffn_conv_b=w['ffn_conv_b'], ffn_w_out=w['ffn_w_out'].astype(BF16))


def kernel(x_prompt, x_sample, state_s5_re, state_s5_im, state_rwkv_wkv, state_rwkv_shift, state_lru_h, state_lru_conv, state_ffn_conv, norm_mix, norm_ffn, norm_final, s5_a_re, s5_a_im, s5_log_dt, s5_b_re, s5_b_im, s5_c_re, s5_c_im, s5_d, s5_w_glu, rw_mu, rw_w_rkv, rw_w0, rw_w1, rw_w2, rw_a0, rw_a1, rw_a2, rw_g1, rw_g2, rw_k_k, rw_k_a, rw_r_k, rw_ln_w, rw_ln_b, rw_w_o, lru_w_in, lru_conv_w, lru_conv_b, lru_w_rg, lru_b_rg, lru_w_ig, lru_b_ig, lru_lambda, lru_w_out, ffn_w_in, ffn_conv_w, ffn_conv_b, ffn_w_out):
    w = _prepare_weights(dict(
        norm_mix=norm_mix, norm_ffn=norm_ffn, norm_final=norm_final,
        s5_a_re=s5_a_re, s5_a_im=s5_a_im, s5_log_dt=s5_log_dt, s5_b_re=s5_b_re, s5_b_im=s5_b_im,
        s5_c_re=s5_c_re, s5_c_im=s5_c_im, s5_d=s5_d, s5_w_glu=s5_w_glu,
        rw_mu=rw_mu, rw_w_rkv=rw_w_rkv, rw_w0=rw_w0, rw_w1=rw_w1, rw_w2=rw_w2, rw_a0=rw_a0,
        rw_a1=rw_a1, rw_a2=rw_a2, rw_g1=rw_g1, rw_g2=rw_g2, rw_k_k=rw_k_k, rw_k_a=rw_k_a,
        rw_r_k=rw_r_k, rw_ln_w=rw_ln_w, rw_ln_b=rw_ln_b, rw_w_o=rw_w_o,
        lru_w_in=lru_w_in, lru_conv_w=lru_conv_w, lru_conv_b=lru_conv_b, lru_w_rg=lru_w_rg,
        lru_b_rg=lru_b_rg, lru_w_ig=lru_w_ig, lru_b_ig=lru_b_ig, lru_lambda=lru_lambda,
        lru_w_out=lru_w_out, ffn_w_in=ffn_w_in, ffn_conv_w=ffn_conv_w, ffn_conv_b=ffn_conv_b,
        ffn_w_out=ffn_w_out))
    bsz, dt = x_prompt.shape[0], x_prompt.dtype
    n_s5, n_rw, n_lru = state_s5_re.shape[0], state_rwkv_wkv.shape[0], state_lru_h.shape[0]
    st_prompt = dict(
        s5_re=jnp.zeros((n_s5, bsz, S5_GROUPS, S5_STATE), dt),
        s5_im=jnp.zeros((n_s5, bsz, S5_GROUPS, S5_STATE), dt),
        rw_wkv=jnp.zeros((n_rw, bsz, RWKV_HEADS, RWKV_HEAD, RWKV_HEAD), dt),
        rw_shift=jnp.zeros((n_rw, bsz, D_MODEL), dt),
        lru_h=jnp.zeros((n_lru, bsz, D_RNN), dt),
        lru_conv=jnp.zeros((n_lru, bsz, LRU_CONV - 1, D_RNN), dt),
        ffn_conv=jnp.zeros((DEPTH, bsz, FFN_CONV - 1, D_FF), dt))
    st_sample = dict(s5_re=state_s5_re, s5_im=state_s5_im, rw_wkv=state_rwkv_wkv,
                     rw_shift=state_rwkv_shift, lru_h=state_lru_h, lru_conv=state_lru_conv,
                     ffn_conv=state_ffn_conv)
    y_p, new_p = _run(x_prompt, st_prompt, w)
    y_s, new_s = _run(x_sample, st_sample, w)
    return (y_p, y_s, new_p['s5_re'], new_s['s5_re'], new_p['s5_im'], new_s['s5_im'],
            new_p['rw_wkv'], new_s['rw_wkv'], new_p['rw_shift'], new_s['rw_shift'],
            new_p['lru_h'], new_s['lru_h'], new_p['lru_conv'], new_s['lru_conv'],
            new_p['ffn_conv'], new_s['ffn_conv'])
```

```python
import functools

import jax
import jax.numpy as jnp
from jax import lax
from jax.experimental import pallas as pl
from jax.experimental.pallas import tpu as pltpu

F32 = jnp.float32
BF16 = jnp.bfloat16

D_MODEL = 1024
DEPTH = 4
N_MIXERS = 3
RMS_EPS = 1e-6

S5_GROUP = 16
S5_GROUPS = D_MODEL // S5_GROUP
S5_STATE = 64
S5_SLABS = 8
S5_SLAB_STATE = (S5_GROUPS // S5_SLABS) * S5_STATE
S5_NSTATE = S5_GROUPS * S5_STATE

RWKV_HEAD = 64
RWKV_HEADS = D_MODEL // RWKV_HEAD
RWKV_PAIRS = RWKV_HEADS // 2
RWKV_GN_EPS = 64e-5
LORA_PAD = 128

D_RNN = D_MODEL
LRU_BLOCKS = 4
LRU_BLOCK = D_RNN // LRU_BLOCKS
LRU_C = 8.0
LRU_CONV = 4

D_FF = 2816
FFN_CONV = 3
FFN_CHUNK = 256
FFN_NCHUNK = D_FF // FFN_CHUNK
FFN_STEPS = 64

LANES = 128
WKV_CHUNK = 64
WKV_ROWS = 4
VMEM_LIMIT = 60 * 1024 * 1024


def _cparams():
    return pltpu.CompilerParams(dimension_semantics=("arbitrary",), vmem_limit_bytes=VMEM_LIMIT)


def _const_spec(shape):
    nd = len(shape)
    return pl.BlockSpec(shape, lambda i, _n=nd: (0,) * _n, pipeline_mode=pl.Buffered(1))


def _row_spec(tm, width):
    return pl.BlockSpec((tm, width), lambda i: (i, 0))


def _rms(x, g):
    ms = jnp.mean(x * x, axis=-1, keepdims=True)
    return x * lax.rsqrt(ms + RMS_EPS) * g


def _bdot(a, w):
    return jnp.dot(a.astype(BF16), w, preferred_element_type=F32)


def _softplus(z):
    return jnp.maximum(z, 0.0) + jnp.log1p(jnp.exp(-jnp.abs(z)))


def _sigmoid(x):
    return 0.5 * jnp.tanh(0.5 * x) + 0.5


def _head_sum(x, sel, sel_t):
    return _bdot(_bdot(x, sel), sel_t)


def _load_rows(x_ref, stage, *, B, TM):
    if not stage:
        return x_ref[...]
    T = TM // B
    for j in range(D_MODEL // LANES):
        for b in range(B):
            stage[0][j, pl.ds(b, T, stride=B), :] = x_ref[b, :, j * LANES:(j + 1) * LANES]
    return jnp.concatenate([stage[0][j] for j in range(D_MODEL // LANES)], axis=1)


def _store_rows(o_ref, stage, y, *, B, TM):
    if not stage:
        o_ref[...] = y
        return
    T = TM // B
    for j in range(D_MODEL // LANES):
        stage[0][j] = y[:, j * LANES:(j + 1) * LANES]
        for b in range(B):
            o_ref[b, :, j * LANES:(j + 1) * LANES] = stage[0][j, pl.ds(b, T, stride=B), :]


def _stage_scratch(tm, batch_major):
    return [pltpu.VMEM((D_MODEL // LANES, tm, LANES), F32)] if batch_major else []


def _rows_or_batch_spec(tm, B, batch_major):
    if batch_major:
        return pl.BlockSpec((B, tm // B, D_MODEL), lambda i: (0, i, 0))
    return _row_spec(tm, D_MODEL)


def _ffn_body(x_ref, g_ref, win_ref, cw_ref, cb_ref, wout_ref, c0_ref, gf_ref,
              o_ref, cnew_ref, carry_ref, *stage, B, TM, final_norm):
    hist = (FFN_CONV - 1) * B

    @pl.when(pl.program_id(0) == 0)
    def _():
        carry_ref[...] = c0_ref[...]

    x = x_ref[...]
    h = _rms(x, g_ref[...]).astype(BF16)
    acc = jnp.zeros((TM, D_MODEL), F32)
    for c in range(FFN_NCHUNK):
        lo = c * FFN_CHUNK
        gate = jnp.dot(h, win_ref[:, lo:lo + FFN_CHUNK], preferred_element_type=F32)
        up = jnp.dot(h, win_ref[:, D_FF + lo:D_FF + lo + FFN_CHUNK], preferred_element_type=F32)
        ext = jnp.concatenate([carry_ref[:, lo:lo + FFN_CHUNK], gate], axis=0)
        conv = cb_ref[:, lo:lo + FFN_CHUNK] + ext[0:TM] * cw_ref[0:1, lo:lo + FFN_CHUNK]
        for k in range(1, FFN_CONV):
            conv = conv + ext[k * B:k * B + TM] * cw_ref[k:k + 1, lo:lo + FFN_CHUNK]
        carry_ref[:, lo:lo + FFN_CHUNK] = ext[TM:TM + hist]
        act = (conv * _sigmoid(conv)) * up
        acc = acc + jnp.dot(act.astype(BF16), wout_ref[lo:lo + FFN_CHUNK, :],
                            preferred_element_type=F32)
    y = x + acc
    if final_norm:
        y = _rms(y, gf_ref[...])
    _store_rows(o_ref, stage, y, B=B, TM=TM)
    cnew_ref[...] = carry_ref[...]


def _layer_spec(shape, layer):
    nd = len(shape) - 1
    return pl.BlockSpec((None,) + tuple(shape[1:]), lambda i, _l=layer, _n=nd: (_l,) + (0,) * _n,
                        pipeline_mode=pl.Buffered(1))


def _ffn(x, g, win_all, cw, cb, wout_all, c0, gf, *, layer, B, TM, final_norm, batch_major_out):
    n = x.shape[0]
    out_rows = (jax.ShapeDtypeStruct((B, n // B, D_MODEL), F32) if batch_major_out
                else jax.ShapeDtypeStruct((n, D_MODEL), F32))
    hist = (FFN_CONV - 1) * B
    body = functools.partial(_ffn_body, B=B, TM=TM, final_norm=final_norm)
    return pl.pallas_call(
        body,
        grid=(n // TM,),
        in_specs=[_row_spec(TM, D_MODEL), _const_spec((1, D_MODEL)), _layer_spec(win_all.shape, layer),
                  _const_spec(cw.shape), _const_spec(cb.shape), _layer_spec(wout_all.shape, layer),
                  _const_spec(c0.shape), _const_spec((1, D_MODEL))],
        out_specs=[_rows_or_batch_spec(TM, B, batch_major_out), _const_spec((hist, D_FF))],
        out_shape=[out_rows, jax.ShapeDtypeStruct((hist, D_FF), F32)],
        scratch_shapes=[pltpu.VMEM((hist, D_FF), F32)] + _stage_scratch(TM, batch_major_out),
        compiler_params=_cparams(),
        name="conv_ffn",
    )(x, g, win_all, cw, cb, wout_all, c0, gf)


def _s5_body(x_ref, g_ref, wbr_ref, wbi_ref, cfr_ref, cfi_ref, abr_ref, abi_ref, wcr_ref, wci_ref,
             d_ref, wglu_ref, h0r_ref, h0i_ref, o_ref, hr_out, hi_out,
             xr_s, xi_s, hr_s, hi_s, y_s, *stage, B, TM, state_t):
    @pl.when(pl.program_id(0) == 0)
    def _():
        cr, ci = cfr_ref[...], cfi_ref[...]
        h0r, h0i = h0r_ref[...], h0i_ref[...]
        if state_t:
            h0r, h0i = h0r.T, h0i.T
        den = cr * cr + ci * ci
        hr_s[...] = (h0r * cr + h0i * ci) / den
        hi_s[...] = (h0i * cr - h0r * ci) / den

    x = _load_rows(x_ref, stage, B=B, TM=TM)
    u = _rms(x, g_ref[...])
    ub = u.astype(BF16)
    steps = TM // B
    for s in range(S5_SLABS):
        sl = slice(s * S5_SLAB_STATE, (s + 1) * S5_SLAB_STATE)
        us = ub[:, s * LANES:(s + 1) * LANES]
        xr_s[s] = jnp.dot(us, wbr_ref[s], preferred_element_type=F32)
        xi_s[s] = jnp.dot(us, wbi_ref[s], preferred_element_type=F32)
        ar = jnp.broadcast_to(abr_ref[:, sl], (B, S5_SLAB_STATE))
        ai = jnp.broadcast_to(abi_ref[:, sl], (B, S5_SLAB_STATE))

        hr, hi = hr_s[:, sl], hi_s[:, sl]
        for t in range(steps):
            rows = slice(t * B, (t + 1) * B)
            hr, hi = ((ar * hr - ai * hi) + xr_s[s, rows, :],
                      (ar * hi + ai * hr) + xi_s[s, rows, :])
            xr_s[s, rows, :] = hr
            xi_s[s, rows, :] = hi
        hr_s[:, sl] = hr
        hi_s[:, sl] = hi
        y_s[:, s * LANES:(s + 1) * LANES] = (
            jnp.dot(xr_s[s].astype(BF16), wcr_ref[s], preferred_element_type=F32)
            - jnp.dot(xi_s[s].astype(BF16), wci_ref[s], preferred_element_type=F32))
    y = y_s[...] + d_ref[...] * u
    z = _bdot(jax.nn.gelu(y), wglu_ref[...])
    o_ref[...] = x + z[:, :D_MODEL] * _sigmoid(z[:, D_MODEL:])
    cr, ci = cfr_ref[...], cfi_ref[...]
    hr_new = cr * hr_s[...] - ci * hi_s[...]
    hi_new = cr * hi_s[...] + ci * hr_s[...]
    hr_out[...] = hr_new.T if state_t else hr_new
    hi_out[...] = hi_new.T if state_t else hi_new


def _s5(x, g, prm, wglu_all, h0r, h0i, *, layer, B, TM, batch_major_in, state_t):
    n = x.shape[0] * x.shape[1] if batch_major_in else x.shape[0]
    body = functools.partial(_s5_body, B=B, TM=TM, state_t=state_t)
    st_shape = (S5_NSTATE, B) if state_t else (B, S5_NSTATE)
    consts = [g, prm['wbr'], prm['wbi'], prm['cfr'], prm['cfi'], prm['abr'], prm['abi'],
              prm['wcr'], prm['wci'], prm['d'], wglu_all, h0r, h0i]
    specs = [_layer_spec(a.shape, layer) if a is wglu_all else _const_spec(a.shape) for a in consts]
    return pl.pallas_call(
        body,
        grid=(n // TM,),
        in_specs=[_rows_or_batch_spec(TM, B, batch_major_in)] + specs,
        out_specs=[_row_spec(TM, D_MODEL), _const_spec(st_shape), _const_spec(st_shape)],
        out_shape=[jax.ShapeDtypeStruct((n, D_MODEL), F32),
                   jax.ShapeDtypeStruct(st_shape, F32), jax.ShapeDtypeStruct(st_shape, F32)],
        scratch_shapes=[pltpu.VMEM((S5_SLABS, TM, S5_SLAB_STATE), F32),
                        pltpu.VMEM((S5_SLABS, TM, S5_SLAB_STATE), F32),
                        pltpu.VMEM((B, S5_NSTATE), F32), pltpu.VMEM((B, S5_NSTATE), F32),
                        pltpu.VMEM((TM, D_MODEL), F32)] + _stage_scratch(TM, batch_major_in),
        compiler_params=_cparams(),
        name="s5_mixer",
    )(x, *consts)


def _s5_params(p):
    lam_re = jnp.minimum(p['s5_a_re'], -1e-4)
    lam_im = p['s5_a_im']
    dt = jnp.exp(p['s5_log_dt'])[:, None]
    mag = jnp.exp(lam_re * dt)
    ab_re = mag * jnp.cos(lam_im * dt)
    ab_im = mag * jnp.sin(lam_im * dt)
    den = lam_re * lam_re + lam_im * lam_im
    coef_re = ((ab_re - 1.0) * lam_re + ab_im * lam_im) / den
    coef_im = (ab_im * lam_re - (ab_re - 1.0) * lam_im) / den
    gps = S5_GROUPS // S5_SLABS
    same_group = (jnp.arange(gps * S5_GROUP)[:, None] // S5_GROUP
                  == jnp.arange(gps * S5_STATE)[None, :] // S5_STATE)

    def slab_in(b):
        rows = jnp.transpose(b.reshape(S5_SLABS, gps, S5_STATE, S5_GROUP), (0, 1, 3, 2))
        rows = rows.reshape(S5_SLABS, gps * S5_GROUP, S5_STATE)
        return jnp.where(same_group, jnp.tile(rows, (1, 1, gps)), 0.0).astype(BF16)

    def slab_out(c):
        cols = jnp.transpose(c.reshape(S5_SLABS, gps, S5_GROUP, S5_STATE), (0, 3, 1, 2))
        cols = cols.reshape(S5_SLABS, S5_STATE, gps * S5_GROUP)
        return jnp.where(same_group.T, jnp.tile(cols, (1, gps, 1)), 0.0).astype(BF16)

    flat = lambda a: a.reshape(1, S5_NSTATE)
    c_re, c_im = p['s5_c_re'], p['s5_c_im']
    cc_re = c_re * coef_re[:, None, :] - c_im * coef_im[:, None, :]
    cc_im = c_re * coef_im[:, None, :] + c_im * coef_re[:, None, :]
    return dict(wbr=slab_in(p['s5_b_re']), wbi=slab_in(p['s5_b_im']),
                wcr=slab_out(cc_re), wci=slab_out(cc_im),
                cfr=flat(coef_re), cfi=flat(coef_im), abr=flat(ab_re), abi=flat(ab_im),
                d=p['s5_d'].reshape(1, D_MODEL))


def _lru_body(x_ref, g_ref, win_ref, cw_ref, cb_ref, wrg_ref, brg_ref, wig_ref, big_ref, lam_ref,
              wout_ref, c0_ref, h0_ref, o_ref, cnew_ref, hnew_ref,
              carry_s, h_s, a_s, bx_s, *, B, TM):
    hist = (LRU_CONV - 1) * B

    @pl.when(pl.program_id(0) == 0)
    def _():
        carry_s[...] = c0_ref[...]
        h_s[...] = h0_ref[...]

    x = x_ref[...]
    xn = _rms(x, g_ref[...])
    gy = _bdot(xn, win_ref[...])
    gate_br = jax.nn.gelu(gy[:, :D_RNN])
    ext = jnp.concatenate([carry_s[...], gy[:, D_RNN:]], axis=0)
    u = cb_ref[...] + ext[0:TM] * cw_ref[0:1, :]
    for k in range(1, LRU_CONV):
        u = u + ext[k * B:k * B + TM] * cw_ref[k:k + 1, :]
    carry_s[...] = ext[TM:TM + hist]
    ub = u.astype(BF16)
    rg_parts, ig_parts = [], []
    for nb in range(LRU_BLOCKS):
        blk = ub[:, nb * LRU_BLOCK:(nb + 1) * LRU_BLOCK]
        rg_parts.append(jnp.dot(blk, wrg_ref[nb], preferred_element_type=F32))
        ig_parts.append(jnp.dot(blk, wig_ref[nb], preferred_element_type=F32))
    rg = _sigmoid(jnp.concatenate(rg_parts, axis=1) + brg_ref[...])
    ig = _sigmoid(jnp.concatenate(ig_parts, axis=1) + big_ref[...])
    log_sig = -_softplus(-lam_ref[...])
    log_a = LRU_C * rg * log_sig
    a = jnp.exp(log_a)
    a_s[...] = a
    bx_s[...] = jnp.sqrt(1.0 - a * a) * ig * u

    def step(t, h):
        rows = pl.ds(pl.multiple_of(t * B, B), B)
        h = a_s[rows, :] * h + bx_s[rows, :]
        bx_s[rows, :] = h
        return h

    h_last = lax.fori_loop(0, TM // B, step, h_s[...])
    h_s[...] = h_last
    o_ref[...] = x + _bdot(bx_s[...] * gate_br, wout_ref[...])
    cnew_ref[...] = carry_s[...]
    hnew_ref[...] = h_last


def _lru(x, g, prm, c0, h0, *, B, TM):
    n = x.shape[0]
    hist = (LRU_CONV - 1) * B
    body = functools.partial(_lru_body, B=B, TM=TM)
    consts = [g, prm['win'], prm['cw'], prm['cb'], prm['wrg'], prm['brg'], prm['wig'], prm['big'],
              prm['lam'], prm['wout'], c0, h0]
    return pl.pallas_call(
        body,
        grid=(n // TM,),
        in_specs=[_row_spec(TM, D_MODEL)] + [_const_spec(a.shape) for a in consts],
        out_specs=[_row_spec(TM, D_MODEL), _const_spec((hist, D_RNN)), _const_spec((B, D_RNN))],
        out_shape=[jax.ShapeDtypeStruct((n, D_MODEL), F32),
                   jax.ShapeDtypeStruct((hist, D_RNN), F32),
                   jax.ShapeDtypeStruct((B, D_RNN), F32)],
        scratch_shapes=[pltpu.VMEM((hist, D_RNN), F32), pltpu.VMEM((B, D_RNN), F32),
                        pltpu.VMEM((TM, D_RNN), F32), pltpu.VMEM((TM, D_RNN), F32)],
        compiler_params=_cparams(),
        name="rglru_mixer",
    )(x, *consts)


def _lru_params(p):
    row = lambda a: a.reshape(1, D_RNN)
    return dict(win=p['lru_w_in'].astype(BF16), cw=p['lru_conv_w'], cb=row(p['lru_conv_b']),
                wrg=p['lru_w_rg'].astype(BF16), brg=row(p['lru_b_rg']),
                wig=p['lru_w_ig'].astype(BF16), big=row(p['lru_b_ig']),
                lam=row(p['lru_lambda']), wout=p['lru_w_out'].astype(BF16))


NCOL = D_MODEL // LANES


def _emit_cols(o_ref, val):
    for j in range(NCOL):
        o_ref[j] = val[:, j * LANES:(j + 1) * LANES]


def _load_cols(ref):
    return jnp.concatenate([ref[j] for j in range(NCOL)], axis=1)


def _rw_pre_body(x_ref, g_ref, mu_ref, wrkv_ref, w0_ref, w1_ref, w2_ref, a0_ref, a1_ref, a2_ref,
                 g1_ref, g2_ref, kk_ref, ka_ref, sel_ref, selt_ref, sh0_ref,
                 r_o, cum_o, lw_o, k_o, v_o, kk_o, b_o, g_o, sh_o, sh_s, *, B, T):
    TM = T * B

    @pl.when(pl.program_id(0) == 0)
    def _():
        sh_s[...] = sh0_ref[...]

    def emit(o_ref, val):
        if T == 1:
            o_ref[...] = val
        else:
            _emit_cols(o_ref, val)

    xn = _rms(x_ref[...], g_ref[...])
    if TM > B:
        prev = jnp.concatenate([sh_s[...], xn[:TM - B]], axis=0)
    else:
        prev = sh_s[...]
    sh_s[...] = xn[TM - B:]
    xx = prev - xn
    mix = lambda n: xn + xx * mu_ref[n:n + 1, :]
    emit(r_o, _bdot(mix(0), wrkv_ref[0]))
    emit(v_o, _bdot(mix(2), wrkv_ref[2]))
    emit(g_o, _bdot(_sigmoid(_bdot(mix(5), g1_ref[...])), g2_ref[...]))
    wl = w0_ref[...] + _bdot(jnp.tanh(_bdot(mix(3), w1_ref[...])), w2_ref[...])
    lw = -jnp.exp(-_softplus(-wl) - 0.5)
    emit(lw_o, lw)
    cum = lw
    sh = B
    while sh < TM:
        cum = cum + jnp.concatenate([jnp.zeros((sh, D_MODEL), F32), cum[:TM - sh]], axis=0)
        sh *= 2
    emit(cum_o, cum)
    a = _sigmoid(a0_ref[...] + _bdot(_bdot(mix(4), a1_ref[...]), a2_ref[...]))
    k = _bdot(mix(1), wrkv_ref[1])
    kk = k * kk_ref[...]
    n2 = _head_sum(kk * kk, sel_ref[...], selt_ref[...])
    kk = kk * lax.rsqrt(jnp.maximum(n2, 1e-24))
    emit(kk_o, kk)
    emit(b_o, kk * a)
    emit(k_o, k * (1.0 + (a - 1.0) * ka_ref[...]))
    sh_o[...] = sh_s[...]


def _cols_spec(tm):
    return pl.BlockSpec((NCOL, tm, LANES), lambda i: (0, i, 0))


def _rw_pre(x, g, prm, sh0, *, B, T):
    n = x.shape[0]
    tm = T * B
    body = functools.partial(_rw_pre_body, B=B, T=T)
    consts = [g, prm['mu'], prm['wrkv'], prm['w0'], prm['w1'], prm['w2'], prm['a0'], prm['a1'],
              prm['a2'], prm['g1'], prm['g2'], prm['k_k'], prm['k_a'], prm['sel'], prm['sel_t'], sh0]
    if T == 1:
        big, big_spec = jax.ShapeDtypeStruct((n, D_MODEL), F32), _row_spec(tm, D_MODEL)
    else:
        big, big_spec = jax.ShapeDtypeStruct((NCOL, n, LANES), F32), _cols_spec(tm)
    return pl.pallas_call(
        body,
        grid=(n // tm,),
        in_specs=[_row_spec(tm, D_MODEL)] + [_const_spec(a.shape) for a in consts],
        out_specs=[big_spec] * 8 + [_const_spec((B, D_MODEL))],
        out_shape=[big] * 8 + [jax.ShapeDtypeStruct((B, D_MODEL), F32)],
        scratch_shapes=[pltpu.VMEM((B, D_MODEL), F32)],
        compiler_params=_cparams(),
        name="rwkv_project",
    )(x, *consts)


def _rw_post_body(x_ref, o_ref, r_ref, k_ref, v_ref, g_ref, rk_ref, lnw_ref, lnb_ref, sel_ref,
                  selt_ref, wo_ref, out_ref, *, T):
    load = (lambda ref: ref[...]) if T == 1 else _load_cols
    o, r, k, v, g = (load(ref) for ref in (o_ref, r_ref, k_ref, v_ref, g_ref))
    hsum = functools.partial(_head_sum, sel=sel_ref[...], sel_t=selt_ref[...])
    mean = hsum(o) * (1.0 / RWKV_HEAD)
    dlt = o - mean
    var = hsum(dlt * dlt) * (1.0 / RWKV_HEAD)
    on = dlt * lax.rsqrt(var + RWKV_GN_EPS) * lnw_ref[...] + lnb_ref[...]
    bonus = hsum(r * k * rk_ref[...]) * v
    out_ref[...] = x_ref[...] + _bdot((on + bonus) * g, wo_ref[...])


def _rw_post(x, o, r, k, v, g, prm, *, B, T):
    n = x.shape[0]
    tm = T * B
    consts = [prm['r_k'], prm['ln_w'], prm['ln_b'], prm['sel'], prm['sel_t'], prm['wo']]
    big_spec = _row_spec(tm, D_MODEL) if T == 1 else _cols_spec(tm)
    return pl.pallas_call(
        functools.partial(_rw_post_body, T=T),
        grid=(n // tm,),
        in_specs=([_row_spec(tm, D_MODEL)] + [big_spec] * 5
                  + [_const_spec(a.shape) for a in consts]),
        out_specs=_row_spec(tm, D_MODEL),
        out_shape=jax.ShapeDtypeStruct((n, D_MODEL), F32),
        compiler_params=_cparams(),
        name="rwkv_output",
    )(x, o, r, k, v, g, *consts)


def _wkv_chunk_body(r_ref, cum_ref, lw_ref, k_ref, v_ref, kk_ref, b_ref, s0_ref, o_ref, sT_ref,
                    S_s, *, B, T):
    @pl.when(pl.program_id(0) == 0)
    def _():
        S_s[...] = s0_ref[...]

    lane = lax.broadcasted_iota(jnp.int32, (T, LANES), 1)
    trow = lax.broadcasted_iota(jnp.int32, (T, LANES), 0)
    first = lane < RWKV_HEAD
    strict = trow > (lane & (T - 1))
    incl = trow >= (lane & (T - 1))
    ri = lax.broadcasted_iota(jnp.int32, (LANES, LANES), 0)
    ci = lax.broadcasted_iota(jnp.int32, (LANES, LANES), 1)
    same_head = (ri < RWKV_HEAD) == (ci < RWKV_HEAD)

    def stack(y):
        return jnp.concatenate([jnp.where(first, y, 0.0), jnp.where(first, 0.0, y)],
                               axis=0).astype(BF16)

    nt = (((1,), (1,)), ((), ()))
    tn = (((0,), (0,)), ((), ()))
    nsteps = T.bit_length() - 1

    def per_group(g, carry):
        chains = [(g * WKV_ROWS + i, hp) for i in range(WKV_ROWS) for hp in range(RWKV_PAIRS)]
        idx = range(len(chains))
        at_chain = [(hp, pl.ds(b, T, stride=B), slice(None)) for b, hp in chains]
        s_old = [S_s[b, hp] for b, hp in chains]
        ks, vs, bbs, ends, e_ends, ps, sps = [], [], [], [], [], [], []
        for c in idx:
            at_c = at_chain[c]
            cm = cum_ref[at_c]
            k = k_ref[at_c]
            bb = b_ref[at_c]
            mid = cm[T // 2 - 1:T // 2, :]
            end = cm[T - 1:T, :]
            at = -kk_ref[at_c] * jnp.exp((cm - lw_ref[at_c]) - mid)
            rt = r_ref[at_c] * jnp.exp(cm - mid)
            e_neg = jnp.exp(mid - cm)
            lhs_f = jnp.concatenate([at, rt], axis=0)
            rhs = jnp.concatenate([stack(bb * e_neg), stack(k * e_neg)], axis=0)
            ps.append(lax.dot_general(lhs_f.astype(BF16), rhs, nt, preferred_element_type=F32))
            sps.append(lax.dot_general((lhs_f * jnp.exp(mid)).astype(BF16), s_old[c].astype(BF16),
                                       nt, preferred_element_type=F32))
            ks.append(k)
            vs.append(v_ref[at_c])
            bbs.append(bb)
            ends.append(end)
            e_ends.append(jnp.exp(end - cm))
        vst = [stack(v) for v in vs]
        ys = [sps[c][0:T] + jnp.dot(jnp.where(strict, ps[c][0:T, 2 * T:4 * T], 0.0).astype(BF16),
                                    vst[c], preferred_element_type=F32) for c in idx]
        ms = [jnp.where(strict, ps[c][0:T, 0:2 * T], 0.0) for c in idx]
        for it in range(nsteps):
            if it + 1 < nsteps:
                both = [jnp.dot(ms[c].astype(BF16),
                                jnp.concatenate([stack(ys[c]), stack(ms[c])], axis=1),
                                preferred_element_type=F32) for c in idx]
                ys = [ys[c] + both[c][:, 0:LANES] for c in idx]
                ms = [both[c][:, LANES:2 * LANES] for c in idx]
            else:
                ys = [ys[c] + jnp.dot(ms[c].astype(BF16), stack(ys[c]), preferred_element_type=F32)
                      for c in idx]
        outs, news = [], []
        for c in idx:
            rbk = jnp.where(jnp.concatenate([incl, incl], axis=1), ps[c][T:2 * T, :], 0.0)
            outs.append(sps[c][T:2 * T]
                        + jnp.dot(rbk.astype(BF16), jnp.concatenate([stack(ys[c]), vst[c]], axis=0),
                                  preferred_element_type=F32))
            upd = lax.dot_general(
                jnp.concatenate([ys[c], vs[c]], axis=0).astype(BF16),
                jnp.concatenate([bbs[c] * e_ends[c], ks[c] * e_ends[c]], axis=0).astype(BF16),
                tn, preferred_element_type=F32)
            news.append(s_old[c] * jnp.exp(ends[c]) + jnp.where(same_head, upd, 0.0))
        for c in idx:
            o_ref[at_chain[c]] = outs[c]
            S_s[chains[c][0], chains[c][1]] = news[c]
        return carry

    lax.fori_loop(0, B // WKV_ROWS, per_group, 0)

    @pl.when(pl.program_id(0) == pl.num_programs(0) - 1)
    def _():
        for b in range(B):
            for hp in range(RWKV_PAIRS):
                blk = S_s[b, hp]
                sT_ref[b, 2 * hp] = blk[0:RWKV_HEAD, 0:RWKV_HEAD]
                sT_ref[b, 2 * hp + 1] = pltpu.roll(blk, RWKV_HEAD, axis=1)[RWKV_HEAD:, 0:RWKV_HEAD]


def _wkv_chunk(r, cum, lw, k, v, kk, bvec, s0, *, B, T):
    n = r.shape[1]
    tm = T * B
    body = functools.partial(_wkv_chunk_body, B=B, T=T)
    st_shape = (B, RWKV_PAIRS, LANES, LANES)
    head_shape = (B, RWKV_HEADS, RWKV_HEAD, RWKV_HEAD)
    return pl.pallas_call(
        body,
        grid=(n // tm,),
        in_specs=[_cols_spec(tm)] * 7 + [_const_spec(st_shape)],
        out_specs=[_cols_spec(tm), _const_spec(head_shape)],
        out_shape=[jax.ShapeDtypeStruct((NCOL, n, LANES), F32),
                   jax.ShapeDtypeStruct(head_shape, F32)],
        scratch_shapes=[pltpu.VMEM(st_shape, F32)],
        compiler_params=_cparams(),
        name="wkv_chunked",
    )(r, cum, lw, k, v, kk, bvec, s0)


def _wkv_step_body(s_ref, r_ref, lw_ref, k_ref, kk_ref, b_ref, v_ref, o_ref, sn_ref):
    w = jnp.exp(lw_ref[...])
    kk, bv, k, r = kk_ref[...], b_ref[...], k_ref[...], r_ref[...]

    def per_value_row(i, carry):
        s = s_ref[i]
        sa = -jnp.sum(s * kk, axis=0, keepdims=True)
        sn = s * w + sa * bv + v_ref[pl.ds(i, 1), :] * k
        sn_ref[i] = sn
        o_ref[pl.ds(i, 1), :] = jnp.sum(sn * r, axis=0, keepdims=True)
        return carry

    lax.fori_loop(0, RWKV_HEAD, per_value_row, 0, unroll=8)


def _wkv_step(s0, r, lw, k, kk, bvec, v):
    bsz = s0.shape[-1]
    vec = lambda t: t.T.reshape(RWKV_HEADS, RWKV_HEAD, bsz)
    s_spec = pl.BlockSpec((None, RWKV_HEAD, RWKV_HEAD, bsz), lambda h: (h, 0, 0, 0))
    v_spec = pl.BlockSpec((None, RWKV_HEAD, bsz), lambda h: (h, 0, 0))
    o, sn = pl.pallas_call(
        _wkv_step_body,
        grid=(RWKV_HEADS,),
        in_specs=[s_spec] + [v_spec] * 6,
        out_specs=[v_spec, s_spec],
        out_shape=[jax.ShapeDtypeStruct((RWKV_HEADS, RWKV_HEAD, bsz), F32),
                   jax.ShapeDtypeStruct(s0.shape, F32)],
        compiler_params=_cparams(),
        name="wkv_step",
    )(s0, vec(r), vec(lw), vec(k), vec(kk), vec(bvec), vec(v))
    return o.reshape(D_MODEL, bsz).T, sn


def _rw_params(p):
    row = lambda a: a.reshape(1, D_MODEL)
    pad_c = lambda w: jnp.pad(w, ((0, 0), (0, LORA_PAD - w.shape[1]))).astype(BF16)
    pad_r = lambda w: jnp.pad(w, ((0, LORA_PAD - w.shape[0]), (0, 0))).astype(BF16)
    sel = (jnp.arange(D_MODEL)[:, None] // RWKV_HEAD == jnp.arange(LANES)[None, :]).astype(BF16)
    return dict(mu=p['rw_mu'], wrkv=p['rw_w_rkv'].astype(BF16), w0=row(p['rw_w0']),
                w1=pad_c(p['rw_w1']), w2=pad_r(p['rw_w2']), a0=row(p['rw_a0']),
                a1=pad_c(p['rw_a1']), a2=pad_r(p['rw_a2']), g1=pad_c(p['rw_g1']),
                g2=pad_r(p['rw_g2']), k_k=row(p['rw_k_k']), k_a=row(p['rw_k_a']),
                r_k=row(p['rw_r_k']), ln_w=row(p['rw_ln_w']), ln_b=row(p['rw_ln_b']),
                wo=p['rw_w_o'].astype(BF16), sel=sel, sel_t=sel.T)


def _pair_states(s):
    bsz = s.shape[0]
    s5 = s.reshape(bsz, RWKV_PAIRS, 2, RWKV_HEAD, RWKV_HEAD)
    eye = jnp.eye(2, dtype=s.dtype)
    bd = jnp.einsum('bpqij,qr->bpqirj', s5, eye)
    return bd.reshape(bsz, RWKV_PAIRS, LANES, LANES)


def _rwkv(x, g, prm, sh0, s0, *, B, L):
    T = min(L, WKV_CHUNK)
    r, cum, lw, k, v, kk, bvec, gt, sh_new = _rw_pre(x, g, prm, sh0, B=B, T=T)
    if L == 1:
        o, s_t = _wkv_step(jnp.transpose(s0, (1, 2, 3, 0)), r, lw, k, kk, bvec, v)
        s_new = jnp.transpose(s_t, (3, 0, 1, 2))
    else:
        o, s_new = _wkv_chunk(r, cum, lw, k, v, kk, bvec, _pair_states(s0), B=B, T=T)
    y = _rw_post(x, o, r, k, v, gt, prm, B=B, T=T)
    return y, sh_new, s_new


def _trunk(x, st, w, *, B, L):
    TM = B * min(L, WKV_CHUNK)
    TM_FFN = B * min(L, FFN_STEPS)
    regroup = L > 1
    if not regroup:
        x = x.reshape(B, D_MODEL)
    row = lambda a: a.reshape(1, -1)
    new = {k: [] for k in ('s5_re', 's5_im', 'rw_wkv', 'rw_shift', 'lru_h', 'lru_conv', 'ffn_conv')}
    for i in range(DEPTH):
        kind, j = i % N_MIXERS, i // N_MIXERS
        g = row(w['norm_mix'][i])
        if kind == 0:
            x, hr, hi = _s5(x, g, w['s5'][j], w['s5_w_glu'], st['s5_re'][j], st['s5_im'][j],
                            layer=j, B=B, TM=TM, batch_major_in=(regroup and i == 0),
                            state_t=(L == 1))
            new['s5_re'].append(hr)
            new['s5_im'].append(hi)
        elif kind == 1:
            x, sh, s = _rwkv(x, g, w['rw'][j], st['rw_shift'][j], st['rw_wkv'][j], B=B, L=L)
            new['rw_shift'].append(sh)
            new['rw_wkv'].append(s)
        else:
            x, cb, hl = _lru(x, g, w['lru'][j], st['lru_conv'][j], st['lru_h'][j], B=B, TM=TM)
            new['lru_conv'].append(cb)
            new['lru_h'].append(hl)
        x, cb = _ffn(x, row(w['norm_ffn'][i]), w['ffn_w_in'], w['ffn_conv_w'][i],
                     row(w['ffn_conv_b'][i]), w['ffn_w_out'], st['ffn_conv'][i],
                     row(w['norm_final']), layer=i, B=B, TM=TM_FFN, final_norm=(i == DEPTH - 1),
                     batch_major_out=(regroup and i == DEPTH - 1))
        new['ffn_conv'].append(cb)
    return x.reshape(B, L, D_MODEL), new


def _stack(parts):
    return parts[0][None] if len(parts) == 1 else jnp.stack(parts)


def _time_major_hist(buf):
    n, bsz, wm1, c = buf.shape
    t = jnp.transpose(buf, (0, 2, 1, 3)).reshape(n, wm1 * bsz, c)
    return [t[j] for j in range(n)]


def _batch_major_hist(rows, bsz):
    t = _stack(rows)
    n, _, c = t.shape
    return jnp.transpose(t.reshape(n, -1, bsz, c), (0, 2, 1, 3))


def _run(x, st, w):
    bsz, length, _ = x.shape
    flat = lambda a: [a[j].reshape(bsz, -1) for j in range(a.shape[0])]
    if length == 1:
        s5_in = lambda a: [jnp.transpose(a[j], (1, 2, 0)).reshape(S5_NSTATE, bsz)
                           for j in range(a.shape[0])]
        s5_out = lambda parts: jnp.transpose(
            _stack(parts).reshape(len(parts), S5_GROUPS, S5_STATE, bsz), (0, 3, 1, 2))
    else:
        s5_in = flat
        s5_out = lambda parts: _stack(parts).reshape(len(parts), bsz, S5_GROUPS, S5_STATE)
    stt = dict(s5_re=s5_in(st['s5_re']), s5_im=s5_in(st['s5_im']),
               rw_wkv=[st['rw_wkv'][j] for j in range(st['rw_wkv'].shape[0])],
               rw_shift=flat(st['rw_shift']), lru_h=flat(st['lru_h']),
               lru_conv=_time_major_hist(st['lru_conv']), ffn_conv=_time_major_hist(st['ffn_conv']))
    y, new = _trunk(x, stt, w, B=bsz, L=length)
    out = dict(
        s5_re=s5_out(new['s5_re']), s5_im=s5_out(new['s5_im']),
        rw_wkv=_stack(new['rw_wkv']), rw_shift=_stack(new['rw_shift']),
        lru_h=_stack(new['lru_h']), lru_conv=_batch_major_hist(new['lru_conv'], bsz),
        ffn_conv=_batch_major_hist(new['ffn_conv'], bsz))
    return y, out


def _prepare_weights(w):
    n_s5 = w['s5_a_re'].shape[0]
    n_rw = w['rw_mu'].shape[0]
    n_lru = w['lru_w_in'].shape[0]
    sub = lambda prefix, j: {k: v[j] for k, v in w.items() if k.startswith(prefix)}
    return dict(
        norm_mix=w['norm_mix'], norm_ffn=w['norm_ffn'], norm_final=w['norm_final'],
        s5=[_s5_params(sub('s5_', j)) for j in range(n_s5)], s5_w_glu=w['s5_w_glu'].astype(BF16),
        rw=[_rw_params(sub('rw_', j)) for j in range(n_rw)],
        lru=[_lru_params(sub('lru_', j)) for j in range(n_lru)],
        ffn_w_in=w['ffn_w_in'].astype(BF16), ffn_conv_w=w['ffn_conv_w'],
        ffn_conv_b=w['ffn_conv_b'], ffn_w_out=w['ffn_w_out'].astype(BF16))


def kernel(x_prompt, x_sample, state_s5_re, state_s5_im, state_rwkv_wkv, state_rwkv_shift, state_lru_h, state_lru_conv, state_ffn_conv, norm_mix, norm_ffn, norm_final, s5_a_re, s5_a_im, s5_log_dt, s5_b_re, s5_b_im, s5_c_re, s5_c_im, s5_d, s5_w_glu, rw_mu, rw_w_rkv, rw_w0, rw_w1, rw_w2, rw_a0, rw_a1, rw_a2, rw_g1, rw_g2, rw_k_k, rw_k_a, rw_r_k, rw_ln_w, rw_ln_b, rw_w_o, lru_w_in, lru_conv_w, lru_conv_b, lru_w_rg, lru_b_rg, lru_w_ig, lru_b_ig, lru_lambda, lru_w_out, ffn_w_in, ffn_conv_w, ffn_conv_b, ffn_w_out):
    w = _prepare_weights(dict(
        norm_mix=norm_mix, norm_ffn=norm_ffn, norm_final=norm_final,
        s5_a_re=s5_a_re, s5_a_im=s5_a_im, s5_log_dt=s5_log_dt, s5_b_re=s5_b_re, s5_b_im=s5_b_im,
        s5_c_re=s5_c_re, s5_c_im=s5_c_im, s5_d=s5_d, s5_w_glu=s5_w_glu,
        rw_mu=rw_mu, rw_w_rkv=rw_w_rkv, rw_w0=rw_w0, rw_w1=rw_w1, rw_w2=rw_w2, rw_a0=rw_a0,
        rw_a1=rw_a1, rw_a2=rw_a2, rw_g1=rw_g1, rw_g2=rw_g2, rw_k_k=rw_k_k, rw_k_a=rw_k_a,
        rw_r_k=rw_r_k, rw_ln_w=rw_ln_w, rw_ln_b=rw_ln_b, rw_w_o=rw_w_o,
        lru_w_in=lru_w_in, lru_conv_w=lru_conv_w, lru_conv_b=lru_conv_b, lru_w_rg=lru_w_rg,
        lru_b_rg=lru_b_rg, lru_w_ig=lru_w_ig, lru_b_ig=lru_b_ig, lru_lambda=lru_lambda,
        lru_w_out=lru_w_out, ffn_w_in=ffn_w_in, ffn_conv_w=ffn_conv_w, ffn_conv_b=ffn_conv_b,
        ffn_w_out=ffn_w_out))
    bsz, dt = x_prompt.shape[0], x_prompt.dtype
    n_s5, n_rw, n_lru = state_s5_re.shape[0], state_rwkv_wkv.shape[0], state_lru_h.shape[0]
    st_prompt = dict(
        s5_re=jnp.zeros((n_s5, bsz, S5_GROUPS, S5_STATE), dt),
        s5_im=jnp.zeros((n_s5, bsz, S5_GROUPS, S5_STATE), dt),
        rw_wkv=jnp.zeros((n_rw, bsz, RWKV_HEADS, RWKV_HEAD, RWKV_HEAD), dt),
        rw_shift=jnp.zeros((n_rw, bsz, D_MODEL), dt),
        lru_h=jnp.zeros((n_lru, bsz, D_RNN), dt),
        lru_conv=jnp.zeros((n_lru, bsz, LRU_CONV - 1, D_RNN), dt),
        ffn_conv=jnp.zeros((DEPTH, bsz, FFN_CONV - 1, D_FF), dt))
    st_sample = dict(s5_re=state_s5_re, s5_im=state_s5_im, rw_wkv=state_rwkv_wkv,
                     rw_shift=state_rwkv_shift, lru_h=state_lru_h, lru_conv=state_lru_conv,
                     ffn_conv=state_ffn_conv)
    y_p, new_p = _run(x_prompt, st_prompt, w)
    y_s, new_s = _run(x_sample, st_sample, w)
    return (y_p, y_s, new_p['s5_re'], new_s['s5_re'], new_p['s5_im'], new_s['s5_im'],
            new_p['rw_wkv'], new_s['rw_wkv'], new_p['rw_shift'], new_s['rw_shift'],
            new_p['lru_h'], new_s['lru_h'], new_p['lru_conv'], new_s['lru_conv'],
            new_p['ffn_conv'], new_s['ffn_conv'])
```

```python
import functools

import jax
import jax.numpy as jnp
from jax import lax
from jax.experimental import pallas as pl
from jax.experimental.pallas import tpu as pltpu

F32 = jnp.float32
BF16 = jnp.bfloat16

D_MODEL = 1024
DEPTH = 4
N_MIXERS = 3
RMS_EPS = 1e-6

S5_GROUP = 16
S5_GROUPS = D_MODEL // S5_GROUP
S5_STATE = 64
S5_SLABS = 8
S5_SLAB_STATE = (S5_GROUPS // S5_SLABS) * S5_STATE
S5_NSTATE = S5_GROUPS * S5_STATE

RWKV_HEAD = 64
RWKV_HEADS = D_MODEL // RWKV_HEAD
RWKV_PAIRS = RWKV_HEADS // 2
RWKV_GN_EPS = 64e-5
LORA_PAD = 128

D_RNN = D_MODEL
LRU_BLOCKS = 4
LRU_BLOCK = D_RNN // LRU_BLOCKS
LRU_C = 8.0
LRU_CONV = 4

D_FF = 2816
FFN_CONV = 3
FFN_CHUNK = 256
FFN_NCHUNK = D_FF // FFN_CHUNK
FFN_STEPS = 64

LANES = 128
WKV_CHUNK = 64
WKV_ROWS = 4
VMEM_LIMIT = 60 * 1024 * 1024


def _cparams():
    return pltpu.CompilerParams(dimension_semantics=("arbitrary",), vmem_limit_bytes=VMEM_LIMIT)


def _const_spec(shape):
    nd = len(shape)
    return pl.BlockSpec(shape, lambda i, _n=nd: (0,) * _n, pipeline_mode=pl.Buffered(1))


def _row_spec(tm, width):
    return pl.BlockSpec((tm, width), lambda i: (i, 0))


def _rms(x, g):
    ms = jnp.mean(x * x, axis=-1, keepdims=True)
    return x * lax.rsqrt(ms + RMS_EPS) * g


def _bdot(a, w):
    return jnp.dot(a.astype(BF16), w, preferred_element_type=F32)


def _softplus(z):
    return jnp.maximum(z, 0.0) + jnp.log1p(jnp.exp(-jnp.abs(z)))


def _sigmoid(x):
    return 0.5 * jnp.tanh(0.5 * x) + 0.5


def _head_sum(x, sel, sel_t):
    return _bdot(_bdot(x, sel), sel_t)


def _load_rows(x_ref, stage, *, B, TM):
    if not stage:
        return x_ref[...]
    T = TM // B
    for j in range(D_MODEL // LANES):
        for b in range(B):
            stage[0][j, pl.ds(b, T, stride=B), :] = x_ref[b, :, j * LANES:(j + 1) * LANES]
    return jnp.concatenate([stage[0][j] for j in range(D_MODEL // LANES)], axis=1)


def _store_rows(o_ref, stage, y, *, B, TM):
    if not stage:
        o_ref[...] = y
        return
    T = TM // B
    for j in range(D_MODEL // LANES):
        stage[0][j] = y[:, j * LANES:(j + 1) * LANES]
        for b in range(B):
            o_ref[b, :, j * LANES:(j + 1) * LANES] = stage[0][j, pl.ds(b, T, stride=B), :]


def _stage_scratch(tm, batch_major):
    return [pltpu.VMEM((D_MODEL // LANES, tm, LANES), F32)] if batch_major else []


def _rows_or_batch_spec(tm, B, batch_major):
    if batch_major:
        return pl.BlockSpec((B, tm // B, D_MODEL), lambda i: (0, i, 0))
    return _row_spec(tm, D_MODEL)


def _ffn_body(x_ref, g_ref, win_ref, cw_ref, cb_ref, wout_ref, c0_ref, gf_ref,
              o_ref, cnew_ref, carry_ref, *stage, B, TM, final_norm):
    hist = (FFN_CONV - 1) * B

    @pl.when(pl.program_id(0) == 0)
    def _():
        carry_ref[...] = c0_ref[...]

    x = x_ref[...]
    h = _rms(x, g_ref[...]).astype(BF16)
    acc = jnp.zeros((TM, D_MODEL), F32)
    for c in range(FFN_NCHUNK):
        lo = c * FFN_CHUNK
        gate = jnp.dot(h, win_ref[:, lo:lo + FFN_CHUNK], preferred_element_type=F32)
        up = jnp.dot(h, win_ref[:, D_FF + lo:D_FF + lo + FFN_CHUNK], preferred_element_type=F32)
        ext = jnp.concatenate([carry_ref[:, lo:lo + FFN_CHUNK], gate], axis=0)
        conv = cb_ref[:, lo:lo + FFN_CHUNK] + ext[0:TM] * cw_ref[0:1, lo:lo + FFN_CHUNK]
        for k in range(1, FFN_CONV):
            conv = conv + ext[k * B:k * B + TM] * cw_ref[k:k + 1, lo:lo + FFN_CHUNK]
        carry_ref[:, lo:lo + FFN_CHUNK] = ext[TM:TM + hist]
        act = (conv * _sigmoid(conv)) * up
        acc = acc + jnp.dot(act.astype(BF16), wout_ref[lo:lo + FFN_CHUNK, :],
                            preferred_element_type=F32)
    y = x + acc
    if final_norm:
        y = _rms(y, gf_ref[...])
    _store_rows(o_ref, stage, y, B=B, TM=TM)
    cnew_ref[...] = carry_ref[...]


def _layer_spec(shape, layer):
    nd = len(shape) - 1
    return pl.BlockSpec((None,) + tuple(shape[1:]), lambda i, _l=layer, _n=nd: (_l,) + (0,) * _n,
                        pipeline_mode=pl.Buffered(1))


def _ffn(x, g, win_all, cw, cb, wout_all, c0, gf, *, layer, B, TM, final_norm, batch_major_out):
    n = x.shape[0]
    out_rows = (jax.ShapeDtypeStruct((B, n // B, D_MODEL), F32) if batch_major_out
                else jax.ShapeDtypeStruct((n, D_MODEL), F32))
    hist = (FFN_CONV - 1) * B
    body = functools.partial(_ffn_body, B=B, TM=TM, final_norm=final_norm)
    return pl.pallas_call(
        body,
        grid=(n // TM,),
        in_specs=[_row_spec(TM, D_MODEL), _const_spec((1, D_MODEL)), _layer_spec(win_all.shape, layer),
                  _const_spec(cw.shape), _const_spec(cb.shape), _layer_spec(wout_all.shape, layer),
                  _const_spec(c0.shape), _const_spec((1, D_MODEL))],
        out_specs=[_rows_or_batch_spec(TM, B, batch_major_out), _const_spec((hist, D_FF))],
        out_shape=[out_rows, jax.ShapeDtypeStruct((hist, D_FF), F32)],
        scratch_shapes=[pltpu.VMEM((hist, D_FF), F32)] + _stage_scratch(TM, batch_major_out),
        compiler_params=_cparams(),
        name="conv_ffn",
    )(x, g, win_all, cw, cb, wout_all, c0, gf)


def _s5_body(x_ref, g_ref, wbr_ref, wbi_ref, cfr_ref, cfi_ref, abr_ref, abi_ref, wcr_ref, wci_ref,
             d_ref, wglu_ref, h0r_ref, h0i_ref, o_ref, hr_out, hi_out,
             xr_s, xi_s, hr_s, hi_s, y_s, *stage, B, TM, state_t):
    @pl.when(pl.program_id(0) == 0)
    def _():
        cr, ci = cfr_ref[...], cfi_ref[...]
        h0r, h0i = h0r_ref[...], h0i_ref[...]
        if state_t:
            h0r, h0i = h0r.T, h0i.T
        den = cr * cr + ci * ci
        hr_s[...] = (h0r * cr + h0i * ci) / den
        hi_s[...] = (h0i * cr - h0r * ci) / den

    x = _load_rows(x_ref, stage, B=B, TM=TM)
    u = _rms(x, g_ref[...])
    ub = u.astype(BF16)
    steps = TM // B
    for s in range(S5_SLABS):
        us = ub[:, s * LANES:(s + 1) * LANES]
        xr_s[s] = jnp.dot(us, wbr_ref[s], preferred_element_type=F32)
        xi_s[s] = jnp.dot(us, wbi_ref[s], preferred_element_type=F32)
    for s in range(S5_SLABS):
        sl = slice(s * S5_SLAB_STATE, (s + 1) * S5_SLAB_STATE)
        ar = jnp.broadcast_to(abr_ref[:, sl], (B, S5_SLAB_STATE))
        ai = jnp.broadcast_to(abi_ref[:, sl], (B, S5_SLAB_STATE))
        hr, hi = hr_s[:, sl], hi_s[:, sl]
        for t in range(steps):
            rows = slice(t * B, (t + 1) * B)
            hr, hi = ((ar * hr - ai * hi) + xr_s[s, rows, :],
                      (ar * hi + ai * hr) + xi_s[s, rows, :])
            xr_s[s, rows, :] = hr
            xi_s[s, rows, :] = hi
        hr_s[:, sl] = hr
        hi_s[:, sl] = hi
        y_s[:, s * LANES:(s + 1) * LANES] = (
            jnp.dot(xr_s[s].astype(BF16), wcr_ref[s], preferred_element_type=F32)
            - jnp.dot(xi_s[s].astype(BF16), wci_ref[s], preferred_element_type=F32))
    y = y_s[...] + d_ref[...] * u
    z = _bdot(jax.nn.gelu(y), wglu_ref[...])
    o_ref[...] = x + z[:, :D_MODEL] * _sigmoid(z[:, D_MODEL:])
    cr, ci = cfr_ref[...], cfi_ref[...]
    hr_new = cr * hr_s[...] - ci * hi_s[...]
    hi_new = cr * hi_s[...] + ci * hr_s[...]
    hr_out[...] = hr_new.T if state_t else hr_new
    hi_out[...] = hi_new.T if state_t else hi_new


def _s5(x, g, prm, wglu_all, h0r, h0i, *, layer, B, TM, batch_major_in, state_t):
    n = x.shape[0] * x.shape[1] if batch_major_in else x.shape[0]
    body = functools.partial(_s5_body, B=B, TM=TM, state_t=state_t)
    st_shape = (S5_NSTATE, B) if state_t else (B, S5_NSTATE)
    consts = [g, prm['wbr'], prm['wbi'], prm['cfr'], prm['cfi'], prm['abr'], prm['abi'],
              prm['wcr'], prm['wci'], prm['d'], wglu_all, h0r, h0i]
    specs = [_layer_spec(a.shape, layer) if a is wglu_all else _const_spec(a.shape) for a in consts]
    return pl.pallas_call(
        body,
        grid=(n // TM,),
        in_specs=[_rows_or_batch_spec(TM, B, batch_major_in)] + specs,
        out_specs=[_row_spec(TM, D_MODEL), _const_spec(st_shape), _const_spec(st_shape)],
        out_shape=[jax.ShapeDtypeStruct((n, D_MODEL), F32),
                   jax.ShapeDtypeStruct(st_shape, F32), jax.ShapeDtypeStruct(st_shape, F32)],
        scratch_shapes=[pltpu.VMEM((S5_SLABS, TM, S5_SLAB_STATE), F32),
                        pltpu.VMEM((S5_SLABS, TM, S5_SLAB_STATE), F32),
                        pltpu.VMEM((B, S5_NSTATE), F32), pltpu.VMEM((B, S5_NSTATE), F32),
                        pltpu.VMEM((TM, D_MODEL), F32)] + _stage_scratch(TM, batch_major_in),
        compiler_params=_cparams(),
        name="s5_mixer",
    )(x, *consts)


def _s5_params(p):
    lam_re = jnp.minimum(p['s5_a_re'], -1e-4)
    lam_im = p['s5_a_im']
    dt = jnp.exp(p['s5_log_dt'])[:, None]
    mag = jnp.exp(lam_re * dt)
    ab_re = mag * jnp.cos(lam_im * dt)
    ab_im = mag * jnp.sin(lam_im * dt)
    den = lam_re * lam_re + lam_im * lam_im
    coef_re = ((ab_re - 1.0) * lam_re + ab_im * lam_im) / den
    coef_im = (ab_im * lam_re - (ab_re - 1.0) * lam_im) / den
    gps = S5_GROUPS // S5_SLABS
    same_group = (jnp.arange(gps * S5_GROUP)[:, None] // S5_GROUP
                  == jnp.arange(gps * S5_STATE)[None, :] // S5_STATE)

    def slab_in(b):
        rows = jnp.transpose(b.reshape(S5_SLABS, gps, S5_STATE, S5_GROUP), (0, 1, 3, 2))
        rows = rows.reshape(S5_SLABS, gps * S5_GROUP, S5_STATE)
        return jnp.where(same_group, jnp.tile(rows, (1, 1, gps)), 0.0).astype(BF16)

    def slab_out(c):
        cols = jnp.transpose(c.reshape(S5_SLABS, gps, S5_GROUP, S5_STATE), (0, 3, 1, 2))
        cols = cols.reshape(S5_SLABS, S5_STATE, gps * S5_GROUP)
        return jnp.where(same_group.T, jnp.tile(cols, (1, gps, 1)), 0.0).astype(BF16)

    flat = lambda a: a.reshape(1, S5_NSTATE)
    c_re, c_im = p['s5_c_re'], p['s5_c_im']
    cc_re = c_re * coef_re[:, None, :] - c_im * coef_im[:, None, :]
    cc_im = c_re * coef_im[:, None, :] + c_im * coef_re[:, None, :]
    return dict(wbr=slab_in(p['s5_b_re']), wbi=slab_in(p['s5_b_im']),
                wcr=slab_out(cc_re), wci=slab_out(cc_im),
                cfr=flat(coef_re), cfi=flat(coef_im), abr=flat(ab_re), abi=flat(ab_im),
                d=p['s5_d'].reshape(1, D_MODEL))


def _lru_body(x_ref, g_ref, win_ref, cw_ref, cb_ref, wrg_ref, brg_ref, wig_ref, big_ref, lam_ref,
              wout_ref, c0_ref, h0_ref, o_ref, cnew_ref, hnew_ref,
              carry_s, h_s, a_s, bx_s, *, B, TM):
    hist = (LRU_CONV - 1) * B

    @pl.when(pl.program_id(0) == 0)
    def _():
        carry_s[...] = c0_ref[...]
        h_s[...] = h0_ref[...]

    x = x_ref[...]
    xn = _rms(x, g_ref[...])
    gy = _bdot(xn, win_ref[...])
    gate_br = jax.nn.gelu(gy[:, :D_RNN])
    ext = jnp.concatenate([carry_s[...], gy[:, D_RNN:]], axis=0)
    u = cb_ref[...] + ext[0:TM] * cw_ref[0:1, :]
    for k in range(1, LRU_CONV):
        u = u + ext[k * B:k * B + TM] * cw_ref[k:k + 1, :]
    carry_s[...] = ext[TM:TM + hist]
    ub = u.astype(BF16)
    rg_parts, ig_parts = [], []
    for nb in range(LRU_BLOCKS):
        blk = ub[:, nb * LRU_BLOCK:(nb + 1) * LRU_BLOCK]
        rg_parts.append(jnp.dot(blk, wrg_ref[nb], preferred_element_type=F32))
        ig_parts.append(jnp.dot(blk, wig_ref[nb], preferred_element_type=F32))
    rg = _sigmoid(jnp.concatenate(rg_parts, axis=1) + brg_ref[...])
    ig = _sigmoid(jnp.concatenate(ig_parts, axis=1) + big_ref[...])
    log_sig = -_softplus(-lam_ref[...])
    log_a = LRU_C * rg * log_sig
    a = jnp.exp(log_a)
    a_s[...] = a
    bx_s[...] = jnp.sqrt(1.0 - a * a) * ig * u

    h_last = h_s[...]
    for t in range(TM // B):
        rows = slice(t * B, (t + 1) * B)
        h_last = a_s[rows, :] * h_last + bx_s[rows, :]
        bx_s[rows, :] = h_last
    h_s[...] = h_last
    o_ref[...] = x + _bdot(bx_s[...] * gate_br, wout_ref[...])
    cnew_ref[...] = carry_s[...]
    hnew_ref[...] = h_last


def _lru(x, g, prm, c0, h0, *, B, TM):
    n = x.shape[0]
    hist = (LRU_CONV - 1) * B
    body = functools.partial(_lru_body, B=B, TM=TM)
    consts = [g, prm['win'], prm['cw'], prm['cb'], prm['wrg'], prm['brg'], prm['wig'], prm['big'],
              prm['lam'], prm['wout'], c0, h0]
    return pl.pallas_call(
        body,
        grid=(n // TM,),
        in_specs=[_row_spec(TM, D_MODEL)] + [_const_spec(a.shape) for a in consts],
        out_specs=[_row_spec(TM, D_MODEL), _const_spec((hist, D_RNN)), _const_spec((B, D_RNN))],
        out_shape=[jax.ShapeDtypeStruct((n, D_MODEL), F32),
                   jax.ShapeDtypeStruct((hist, D_RNN), F32),
                   jax.ShapeDtypeStruct((B, D_RNN), F32)],
        scratch_shapes=[pltpu.VMEM((hist, D_RNN), F32), pltpu.VMEM((B, D_RNN), F32),
                        pltpu.VMEM((TM, D_RNN), F32), pltpu.VMEM((TM, D_RNN), F32)],
        compiler_params=_cparams(),
        name="rglru_mixer",
    )(x, *consts)


def _lru_params(p):
    row = lambda a: a.reshape(1, D_RNN)
    return dict(win=p['lru_w_in'].astype(BF16), cw=p['lru_conv_w'], cb=row(p['lru_conv_b']),
                wrg=p['lru_w_rg'].astype(BF16), brg=row(p['lru_b_rg']),
                wig=p['lru_w_ig'].astype(BF16), big=row(p['lru_b_ig']),
                lam=row(p['lru_lambda']), wout=p['lru_w_out'].astype(BF16))


NCOL = D_MODEL // LANES


def _emit_cols(o_ref, val):
    for j in range(NCOL):
        o_ref[j] = val[:, j * LANES:(j + 1) * LANES]


def _load_cols(ref):
    return jnp.concatenate([ref[j] for j in range(NCOL)], axis=1)


def _rw_pre_body(x_ref, g_ref, mu_ref, wrkv_ref, w0_ref, w1_ref, w2_ref, a0_ref, a1_ref, a2_ref,
                 g1_ref, g2_ref, kk_ref, ka_ref, sel_ref, selt_ref, sh0_ref,
                 r_o, cum_o, lw_o, k_o, v_o, kk_o, b_o, g_o, sh_o, sh_s, *, B, T):
    TM = T * B

    @pl.when(pl.program_id(0) == 0)
    def _():
        sh_s[...] = sh0_ref[...]

    def emit(o_ref, val):
        if T == 1:
            o_ref[...] = val
        else:
            _emit_cols(o_ref, val)

    xn = _rms(x_ref[...], g_ref[...])
    if TM > B:
        prev = jnp.concatenate([sh_s[...], xn[:TM - B]], axis=0)
    else:
        prev = sh_s[...]
    sh_s[...] = xn[TM - B:]
    xx = prev - xn
    mix = lambda n: xn + xx * mu_ref[n:n + 1, :]
    emit(r_o, _bdot(mix(0), wrkv_ref[0]))
    emit(v_o, _bdot(mix(2), wrkv_ref[2]))
    emit(g_o, _bdot(_sigmoid(_bdot(mix(5), g1_ref[...])), g2_ref[...]))
    wl = w0_ref[...] + _bdot(jnp.tanh(_bdot(mix(3), w1_ref[...])), w2_ref[...])
    lw = -jnp.exp(-_softplus(-wl) - 0.5)
    emit(lw_o, lw)
    cum = lw
    sh = B
    while sh < TM:
        cum = cum + jnp.concatenate([jnp.zeros((sh, D_MODEL), F32), cum[:TM - sh]], axis=0)
        sh *= 2
    emit(cum_o, cum)
    a = _sigmoid(a0_ref[...] + _bdot(_bdot(mix(4), a1_ref[...]), a2_ref[...]))
    k = _bdot(mix(1), wrkv_ref[1])
    kk = k * kk_ref[...]
    n2 = _head_sum(kk * kk, sel_ref[...], selt_ref[...])
    kk = kk * lax.rsqrt(jnp.maximum(n2, 1e-24))
    emit(kk_o, kk)
    emit(b_o, kk * a)
    emit(k_o, k * (1.0 + (a - 1.0) * ka_ref[...]))
    sh_o[...] = sh_s[...]


def _cols_spec(tm):
    return pl.BlockSpec((NCOL, tm, LANES), lambda i: (0, i, 0))


def _rw_pre(x, g, prm, sh0, *, B, T):
    n = x.shape[0]
    tm = T * B
    body = functools.partial(_rw_pre_body, B=B, T=T)
    consts = [g, prm['mu'], prm['wrkv'], prm['w0'], prm['w1'], prm['w2'], prm['a0'], prm['a1'],
              prm['a2'], prm['g1'], prm['g2'], prm['k_k'], prm['k_a'], prm['sel'], prm['sel_t'], sh0]
    if T == 1:
        big, big_spec = jax.ShapeDtypeStruct((n, D_MODEL), F32), _row_spec(tm, D_MODEL)
    else:
        big, big_spec = jax.ShapeDtypeStruct((NCOL, n, LANES), F32), _cols_spec(tm)
    return pl.pallas_call(
        body,
        grid=(n // tm,),
        in_specs=[_row_spec(tm, D_MODEL)] + [_const_spec(a.shape) for a in consts],
        out_specs=[big_spec] * 8 + [_const_spec((B, D_MODEL))],
        out_shape=[big] * 8 + [jax.ShapeDtypeStruct((B, D_MODEL), F32)],
        scratch_shapes=[pltpu.VMEM((B, D_MODEL), F32)],
        compiler_params=_cparams(),
        name="rwkv_project",
    )(x, *consts)


def _rw_post_body(x_ref, o_ref, r_ref, k_ref, v_ref, g_ref, rk_ref, lnw_ref, lnb_ref, sel_ref,
                  selt_ref, wo_ref, out_ref, *, T):
    load = (lambda ref: ref[...]) if T == 1 else _load_cols
    o, r, k, v, g = (load(ref) for ref in (o_ref, r_ref, k_ref, v_ref, g_ref))
    hsum = functools.partial(_head_sum, sel=sel_ref[...], sel_t=selt_ref[...])
    mean = hsum(o) * (1.0 / RWKV_HEAD)
    dlt = o - mean
    var = hsum(dlt * dlt) * (1.0 / RWKV_HEAD)
    on = dlt * lax.rsqrt(var + RWKV_GN_EPS) * lnw_ref[...] + lnb_ref[...]
    bonus = hsum(r * k * rk_ref[...]) * v
    out_ref[...] = x_ref[...] + _bdot((on + bonus) * g, wo_ref[...])


def _rw_post(x, o, r, k, v, g, prm, *, B, T):
    n = x.shape[0]
    tm = T * B
    consts = [prm['r_k'], prm['ln_w'], prm['ln_b'], prm['sel'], prm['sel_t'], prm['wo']]
    big_spec = _row_spec(tm, D_MODEL) if T == 1 else _cols_spec(tm)
    return pl.pallas_call(
        functools.partial(_rw_post_body, T=T),
        grid=(n // tm,),
        in_specs=([_row_spec(tm, D_MODEL)] + [big_spec] * 5
                  + [_const_spec(a.shape) for a in consts]),
        out_specs=_row_spec(tm, D_MODEL),
        out_shape=jax.ShapeDtypeStruct((n, D_MODEL), F32),
        compiler_params=_cparams(),
        name="rwkv_output",
    )(x, o, r, k, v, g, *consts)


def _wkv_chunk_body(r_ref, cum_ref, lw_ref, k_ref, v_ref, kk_ref, b_ref, s0_ref, o_ref, sT_ref,
                    S_s, *, B, T):
    @pl.when(pl.program_id(0) == 0)
    def _():
        S_s[...] = s0_ref[...]

    lane = lax.broadcasted_iota(jnp.int32, (T, LANES), 1)
    trow = lax.broadcasted_iota(jnp.int32, (T, LANES), 0)
    first = lane < RWKV_HEAD
    strict = trow > (lane & (T - 1))
    incl = trow >= (lane & (T - 1))
    ri = lax.broadcasted_iota(jnp.int32, (LANES, LANES), 0)
    ci = lax.broadcasted_iota(jnp.int32, (LANES, LANES), 1)
    same_head = (ri < RWKV_HEAD) == (ci < RWKV_HEAD)

    def stack(y):
        return jnp.concatenate([jnp.where(first, y, 0.0), jnp.where(first, 0.0, y)],
                               axis=0).astype(BF16)

    nt = (((1,), (1,)), ((), ()))
    tn = (((0,), (0,)), ((), ()))
    nsteps = T.bit_length() - 1

    def per_group(g, carry):
        chains = [(g * WKV_ROWS + i, hp) for i in range(WKV_ROWS) for hp in range(RWKV_PAIRS)]
        idx = range(len(chains))
        at_chain = [(hp, pl.ds(b, T, stride=B), slice(None)) for b, hp in chains]
        s_old = [S_s[b, hp] for b, hp in chains]
        ks, vs, bbs, ends, e_ends, ps, sps = [], [], [], [], [], [], []
        for c in idx:
            at_c = at_chain[c]
            cm = cum_ref[at_c]
            k = k_ref[at_c]
            bb = b_ref[at_c]
            mid = cm[T // 2 - 1:T // 2, :]
            end = cm[T - 1:T, :]
            at = -kk_ref[at_c] * jnp.exp((cm - lw_ref[at_c]) - mid)
            rt = r_ref[at_c] * jnp.exp(cm - mid)
            e_neg = jnp.exp(mid - cm)
            lhs_f = jnp.concatenate([at, rt], axis=0)
            rhs = jnp.concatenate([stack(bb * e_neg), stack(k * e_neg)], axis=0)
            ps.append(lax.dot_general(lhs_f.astype(BF16), rhs, nt, preferred_element_type=F32))
            sps.append(lax.dot_general((lhs_f * jnp.exp(mid)).astype(BF16), s_old[c].astype(BF16),
                                       nt, preferred_element_type=F32))
            ks.append(k)
            vs.append(v_ref[at_c])
            bbs.append(bb)
            ends.append(end)
            e_ends.append(jnp.exp(end - cm))
        vst = [stack(v) for v in vs]
        ys = [sps[c][0:T] + jnp.dot(jnp.where(strict, ps[c][0:T, 2 * T:4 * T], 0.0).astype(BF16),
                                    vst[c], preferred_element_type=F32) for c in idx]
        ms = [jnp.where(strict, ps[c][0:T, 0:2 * T], 0.0) for c in idx]
        for it in range(nsteps):
            if it + 1 < nsteps:
                both = [jnp.dot(ms[c].astype(BF16),
                                jnp.concatenate([stack(ys[c]), stack(ms[c])], axis=1),
                                preferred_element_type=F32) for c in idx]
                ys = [ys[c] + both[c][:, 0:LANES] for c in idx]
                ms = [both[c][:, LANES:2 * LANES] for c in idx]
            else:
                ys = [ys[c] + jnp.dot(ms[c].astype(BF16), stack(ys[c]), preferred_element_type=F32)
                      for c in idx]
        outs, news = [], []
        for c in idx:
            rbk = jnp.where(jnp.concatenate([incl, incl], axis=1), ps[c][T:2 * T, :], 0.0)
            outs.append(sps[c][T:2 * T]
                        + jnp.dot(rbk.astype(BF16), jnp.concatenate([stack(ys[c]), vst[c]], axis=0),
                                  preferred_element_type=F32))
            upd = lax.dot_general(
                jnp.concatenate([ys[c], vs[c]], axis=0).astype(BF16),
                jnp.concatenate([bbs[c] * e_ends[c], ks[c] * e_ends[c]], axis=0).astype(BF16),
                tn, preferred_element_type=F32)
            news.append(s_old[c] * jnp.exp(ends[c]) + jnp.where(same_head, upd, 0.0))
        for c in idx:
            o_ref[at_chain[c]] = outs[c]
            S_s[chains[c][0], chains[c][1]] = news[c]
        return carry

    lax.fori_loop(0, B // WKV_ROWS, per_group, 0)

    @pl.when(pl.program_id(0) == pl.num_programs(0) - 1)
    def _():
        for b in range(B):
            for hp in range(RWKV_PAIRS):
                blk = S_s[b, hp]
                sT_ref[b, 2 * hp] = blk[0:RWKV_HEAD, 0:RWKV_HEAD]
                sT_ref[b, 2 * hp + 1] = pltpu.roll(blk, RWKV_HEAD, axis=1)[RWKV_HEAD:, 0:RWKV_HEAD]


def _wkv_chunk(r, cum, lw, k, v, kk, bvec, s0, *, B, T):
    n = r.shape[1]
    tm = T * B
    body = functools.partial(_wkv_chunk_body, B=B, T=T)
    st_shape = (B, RWKV_PAIRS, LANES, LANES)
    head_shape = (B, RWKV_HEADS, RWKV_HEAD, RWKV_HEAD)
    return pl.pallas_call(
        body,
        grid=(n // tm,),
        in_specs=[_cols_spec(tm)] * 7 + [_const_spec(st_shape)],
        out_specs=[_cols_spec(tm), _const_spec(head_shape)],
        out_shape=[jax.ShapeDtypeStruct((NCOL, n, LANES), F32),
                   jax.ShapeDtypeStruct(head_shape, F32)],
        scratch_shapes=[pltpu.VMEM(st_shape, F32)],
        compiler_params=_cparams(),
        name="wkv_chunked",
    )(r, cum, lw, k, v, kk, bvec, s0)


def _wkv_step_body(s_ref, r_ref, lw_ref, k_ref, kk_ref, b_ref, v_ref, o_ref, sn_ref):
    w = jnp.exp(lw_ref[...])
    kk, bv, k, r = kk_ref[...], b_ref[...], k_ref[...], r_ref[...]

    def per_value_row(i, carry):
        s = s_ref[i]
        sa = -jnp.sum(s * kk, axis=0, keepdims=True)
        sn = s * w + sa * bv + v_ref[pl.ds(i, 1), :] * k
        sn_ref[i] = sn
        o_ref[pl.ds(i, 1), :] = jnp.sum(sn * r, axis=0, keepdims=True)
        return carry

    lax.fori_loop(0, RWKV_HEAD, per_value_row, 0, unroll=8)


def _wkv_step(s0, r, lw, k, kk, bvec, v):
    bsz = s0.shape[-1]
    vec = lambda t: t.T.reshape(RWKV_HEADS, RWKV_HEAD, bsz)
    s_spec = pl.BlockSpec((None, RWKV_HEAD, RWKV_HEAD, bsz), lambda h: (h, 0, 0, 0))
    v_spec = pl.BlockSpec((None, RWKV_HEAD, bsz), lambda h: (h, 0, 0))
    o, sn = pl.pallas_call(
        _wkv_step_body,
        grid=(RWKV_HEADS,),
        in_specs=[s_spec] + [v_spec] * 6,
        out_specs=[v_spec, s_spec],
        out_shape=[jax.ShapeDtypeStruct((RWKV_HEADS, RWKV_HEAD, bsz), F32),
                   jax.ShapeDtypeStruct(s0.shape, F32)],
        compiler_params=_cparams(),
        name="wkv_step",
    )(s0, vec(r), vec(lw), vec(k), vec(kk), vec(bvec), vec(v))
    return o.reshape(D_MODEL, bsz).T, sn


def _rw_params(p):
    row = lambda a: a.reshape(1, D_MODEL)
    pad_c = lambda w: jnp.pad(w, ((0, 0), (0, LORA_PAD - w.shape[1]))).astype(BF16)
    pad_r = lambda w: jnp.pad(w, ((0, LORA_PAD - w.shape[0]), (0, 0))).astype(BF16)
    sel = (jnp.arange(D_MODEL)[:, None] // RWKV_HEAD == jnp.arange(LANES)[None, :]).astype(BF16)
    return dict(mu=p['rw_mu'], wrkv=p['rw_w_rkv'].astype(BF16), w0=row(p['rw_w0']),
                w1=pad_c(p['rw_w1']), w2=pad_r(p['rw_w2']), a0=row(p['rw_a0']),
                a1=pad_c(p['rw_a1']), a2=pad_r(p['rw_a2']), g1=pad_c(p['rw_g1']),
                g2=pad_r(p['rw_g2']), k_k=row(p['rw_k_k']), k_a=row(p['rw_k_a']),
                r_k=row(p['rw_r_k']), ln_w=row(p['rw_ln_w']), ln_b=row(p['rw_ln_b']),
                wo=p['rw_w_o'].astype(BF16), sel=sel, sel_t=sel.T)


def _pair_states(s):
    bsz = s.shape[0]
    s5 = s.reshape(bsz, RWKV_PAIRS, 2, RWKV_HEAD, RWKV_HEAD)
    eye = jnp.eye(2, dtype=s.dtype)
    bd = jnp.einsum('bpqij,qr->bpqirj', s5, eye)
    return bd.reshape(bsz, RWKV_PAIRS, LANES, LANES)


def _rwkv(x, g, prm, sh0, s0, *, B, L):
    T = min(L, WKV_CHUNK)
    r, cum, lw, k, v, kk, bvec, gt, sh_new = _rw_pre(x, g, prm, sh0, B=B, T=T)
    if L == 1:
        o, s_t = _wkv_step(jnp.transpose(s0, (1, 2, 3, 0)), r, lw, k, kk, bvec, v)
        s_new = jnp.transpose(s_t, (3, 0, 1, 2))
    else:
        o, s_new = _wkv_chunk(r, cum, lw, k, v, kk, bvec, _pair_states(s0), B=B, T=T)
    y = _rw_post(x, o, r, k, v, gt, prm, B=B, T=T)
    return y, sh_new, s_new


def _trunk(x, st, w, *, B, L):
    TM = B * min(L, WKV_CHUNK)
    TM_FFN = B * min(L, FFN_STEPS)
    regroup = L > 1
    if not regroup:
        x = x.reshape(B, D_MODEL)
    row = lambda a: a.reshape(1, -1)
    new = {k: [] for k in ('s5_re', 's5_im', 'rw_wkv', 'rw_shift', 'lru_h', 'lru_conv', 'ffn_conv')}
    for i in range(DEPTH):
        kind, j = i % N_MIXERS, i // N_MIXERS
        g = row(w['norm_mix'][i])
        if kind == 0:
            x, hr, hi = _s5(x, g, w['s5'][j], w['s5_w_glu'], st['s5_re'][j], st['s5_im'][j],
                            layer=j, B=B, TM=TM, batch_major_in=(regroup and i == 0),
                            state_t=(L == 1))
            new['s5_re'].append(hr)
            new['s5_im'].append(hi)
        elif kind == 1:
            x, sh, s = _rwkv(x, g, w['rw'][j], st['rw_shift'][j], st['rw_wkv'][j], B=B, L=L)
            new['rw_shift'].append(sh)
            new['rw_wkv'].append(s)
        else:
            x, cb, hl = _lru(x, g, w['lru'][j], st['lru_conv'][j], st['lru_h'][j], B=B, TM=TM)
            new['lru_conv'].append(cb)
            new['lru_h'].append(hl)
        x, cb = _ffn(x, row(w['norm_ffn'][i]), w['ffn_w_in'], w['ffn_conv_w'][i],
                     row(w['ffn_conv_b'][i]), w['ffn_w_out'], st['ffn_conv'][i],
                     row(w['norm_final']), layer=i, B=B, TM=TM_FFN, final_norm=(i == DEPTH - 1),
                     batch_major_out=(regroup and i == DEPTH - 1))
        new['ffn_conv'].append(cb)
    return x.reshape(B, L, D_MODEL), new


def _stack(parts):
    return parts[0][None] if len(parts) == 1 else jnp.stack(parts)


def _time_major_hist(buf):
    n, bsz, wm1, c = buf.shape
    t = jnp.transpose(buf, (0, 2, 1, 3)).reshape(n, wm1 * bsz, c)
    return [t[j] for j in range(n)]


def _batch_major_hist(rows, bsz):
    t = _stack(rows)
    n, _, c = t.shape
    return jnp.transpose(t.reshape(n, -1, bsz, c), (0, 2, 1, 3))


def _run(x, st, w):
    bsz, length, _ = x.shape
    flat = lambda a: [a[j].reshape(bsz, -1) for j in range(a.shape[0])]
    if length == 1:
        s5_in = lambda a: [jnp.transpose(a[j], (1, 2, 0)).reshape(S5_NSTATE, bsz)
                           for j in range(a.shape[0])]
        s5_out = lambda parts: jnp.transpose(
            _stack(parts).reshape(len(parts), S5_GROUPS, S5_STATE, bsz), (0, 3, 1, 2))
    else:
        s5_in = flat
        s5_out = lambda parts: _stack(parts).reshape(len(parts), bsz, S5_GROUPS, S5_STATE)
    stt = dict(s5_re=s5_in(st['s5_re']), s5_im=s5_in(st['s5_im']),
               rw_wkv=[st['rw_wkv'][j] for j in range(st['rw_wkv'].shape[0])],
               rw_shift=flat(st['rw_shift']), lru_h=flat(st['lru_h']),
               lru_conv=_time_major_hist(st['lru_conv']), ffn_conv=_time_major_hist(st['ffn_conv']))
    y, new = _trunk(x, stt, w, B=bsz, L=length)
    out = dict(
        s5_re=s5_out(new['s5_re']), s5_im=s5_out(new['s5_im']),
        rw_wkv=_stack(new['rw_wkv']), rw_shift=_stack(new['rw_shift']),
        lru_h=_stack(new['lru_h']), lru_conv=_batch_major_hist(new['lru_conv'], bsz),
        ffn_conv=_batch_major_hist(new['ffn_conv'], bsz))
    return y, out


def _prepare_weights(w):
    n_s5 = w['s5_a_re'].shape[0]
    n_rw = w['rw_mu'].shape[0]
    n_lru = w['lru_w_in'].shape[0]
    sub = lambda prefix, j: {k: v[j] for k, v in w.items() if k.startswith(prefix)}
    return dict(
        norm_mix=w['norm_mix'], norm_ffn=w['norm_ffn'], norm_final=w['norm_final'],
        s5=[_s5_params(sub('s5_', j)) for j in range(n_s5)], s5_w_glu=w['s5_w_glu'].astype(BF16),
        rw=[_rw_params(sub('rw_', j)) for j in range(n_rw)],
        lru=[_lru_params(sub('lru_', j)) for j in range(n_lru)],
        ffn_w_in=w['ffn_w_in'].astype(BF16), ffn_conv_w=w['ffn_conv_w'],
        ffn_conv_b=w['ffn_conv_b'], ffn_w_out=w['ffn_w_out'].astype(BF16))


def kernel(x_prompt, x_sample, state_s5_re, state_s5_im, state_rwkv_wkv, state_rwkv_shift, state_lru_h, state_lru_conv, state_ffn_conv, norm_mix, norm_ffn, norm_final, s5_a_re, s5_a_im, s5_log_dt, s5_b_re, s5_b_im, s5_c_re, s5_c_im, s5_d, s5_w_glu, rw_mu, rw_w_rkv, rw_w0, rw_w1, rw_w2, rw_a0, rw_a1, rw_a2, rw_g1, rw_g2, rw_k_k, rw_k_a, rw_r_k, rw_ln_w, rw_ln_b, rw_w_o, lru_w_in, lru_conv_w, lru_conv_b, lru_w_rg, lru_b_rg, lru_w_ig, lru_b_ig, lru_lambda, lru_w_out, ffn_w_in, ffn_conv_w, ffn_conv_b, ffn_w_out):
    w = _prepare_weights(dict(
        norm_mix=norm_mix, norm_ffn=norm_ffn, norm_final=norm_final,
        s5_a_re=s5_a_re, s5_a_im=s5_a_im, s5_log_dt=s5_log_dt, s5_b_re=s5_b_re, s5_b_im=s5_b_im,
        s5_c_re=s5_c_re, s5_c_im=s5_c_im, s5_d=s5_d, s5_w_glu=s5_w_glu,
        rw_mu=rw_mu, rw_w_rkv=rw_w_rkv, rw_w0=rw_w0, rw_w1=rw_w1, rw_w2=rw_w2, rw_a0=rw_a0,
        rw_a1=rw_a1, rw_a2=rw_a2, rw_g1=rw_g1, rw_g2=rw_g2, rw_k_k=rw_k_k, rw_k_a=rw_k_a,
        rw_r_k=rw_r_k, rw_ln_w=rw_ln_w, rw_ln_b=rw_ln_b, rw_w_o=rw_w_o,
        lru_w_in=lru_w_in, lru_conv_w=lru_conv_w, lru_conv_b=lru_conv_b, lru_w_rg=lru_w_rg,
        lru_b_rg=lru_b_rg, lru_w_ig=lru_w_ig, lru_b_ig=lru_b_ig, lru_lambda=lru_lambda,
        lru_w_out=lru_w_out, ffn_w_in=ffn_w_in, ffn_conv_w=ffn_conv_w, ffn_conv_b=ffn_conv_b,
        ffn_w_out=ffn_w_out))
    bsz, dt = x_prompt.shape[0], x_prompt.dtype
    n_s5, n_rw, n_lru = state_s5_re.shape[0], state_rwkv_wkv.shape[0], state_lru_h.shape[0]
    st_prompt = dict(
        s5_re=jnp.zeros((n_s5, bsz, S5_GROUPS, S5_STATE), dt),
        s5_im=jnp.zeros((n_s5, bsz, S5_GROUPS, S5_STATE), dt),
        rw_wkv=jnp.zeros((n_rw, bsz, RWKV_HEADS, RWKV_HEAD, RWKV_HEAD), dt),
        rw_shift=jnp.zeros((n_rw, bsz, D_MODEL), dt),
        lru_h=jnp.zeros((n_lru, bsz, D_RNN), dt),
        lru_conv=jnp.zeros((n_lru, bsz, LRU_CONV - 1, D_RNN), dt),
        ffn_conv=jnp.zeros((DEPTH, bsz, FFN_CONV - 1, D_FF), dt))
    st_sample = dict(s5_re=state_s5_re, s5_im=state_s5_im, rw_wkv=state_rwkv_wkv,
                     rw_shift=state_rwkv_shift, lru_h=state_lru_h, lru_conv=state_lru_conv,
                     ffn_conv=state_ffn_conv)
    y_p, new_p = _run(x_prompt, st_prompt, w)
    y_s, new_s = _run(x_sample, st_sample, w)
    return (y_p, y_s, new_p['s5_re'], new_s['s5_re'], new_p['s5_im'], new_s['s5_im'],
            new_p['rw_wkv'], new_s['rw_wkv'], new_p['rw_shift'], new_s['rw_shift'],
            new_p['lru_h'], new_s['lru_h'], new_p['lru_conv'], new_s['lru_conv'],
            new_p['ffn_conv'], new_s['ffn_conv'])
```

```python
import functools

import jax
import jax.numpy as jnp
from jax import lax
from jax.experimental import pallas as pl
from jax.experimental.pallas import tpu as pltpu

F32 = jnp.float32
BF16 = jnp.bfloat16

D_MODEL = 1024
DEPTH = 4
N_MIXERS = 3
RMS_EPS = 1e-6

S5_GROUP = 16
S5_GROUPS = D_MODEL // S5_GROUP
S5_STATE = 64
S5_SLABS = 8
S5_SLAB_STATE = (S5_GROUPS // S5_SLABS) * S5_STATE
S5_NSTATE = S5_GROUPS * S5_STATE

RWKV_HEAD = 64
RWKV_HEADS = D_MODEL // RWKV_HEAD
RWKV_PAIRS = RWKV_HEADS // 2
RWKV_GN_EPS = 64e-5
LORA_PAD = 128

D_RNN = D_MODEL
LRU_BLOCKS = 4
LRU_BLOCK = D_RNN // LRU_BLOCKS
LRU_C = 8.0
LRU_CONV = 4

D_FF = 2816
FFN_CONV = 3
FFN_STEPS = 64

LANES = 128
WKV_CHUNK = 64
WKV_ROWS = 4
VMEM_LIMIT = 60 * 1024 * 1024


def _cparams():
    return pltpu.CompilerParams(dimension_semantics=("arbitrary",), vmem_limit_bytes=VMEM_LIMIT)


def _const_spec(shape):
    nd = len(shape)
    return pl.BlockSpec(shape, lambda i, _n=nd: (0,) * _n, pipeline_mode=pl.Buffered(1))


def _row_spec(tm, width):
    return pl.BlockSpec((tm, width), lambda i: (i, 0))


def _rms(x, g):
    ms = jnp.mean(x * x, axis=-1, keepdims=True)
    return x * lax.rsqrt(ms + RMS_EPS) * g


def _bdot(a, w):
    return jnp.dot(a.astype(BF16), w, preferred_element_type=F32)


def _softplus(z):
    return jnp.maximum(z, 0.0) + jnp.log1p(jnp.exp(-jnp.abs(z)))


def _sigmoid(x):
    return 0.5 * jnp.tanh(0.5 * x) + 0.5


def _head_sum(x, sel, sel_t):
    return _bdot(_bdot(x, sel), sel_t)


def _load_rows(x_ref, stage, *, B, TM):
    if not stage:
        return x_ref[...]
    T = TM // B
    for j in range(D_MODEL // LANES):
        for b in range(B):
            stage[0][j, pl.ds(b, T, stride=B), :] = x_ref[b, :, j * LANES:(j + 1) * LANES]
    return jnp.concatenate([stage[0][j] for j in range(D_MODEL // LANES)], axis=1)


def _store_rows(o_ref, stage, y, *, B, TM):
    if not stage:
        o_ref[...] = y
        return
    T = TM // B
    for j in range(D_MODEL // LANES):
        stage[0][j] = y[:, j * LANES:(j + 1) * LANES]
        for b in range(B):
            o_ref[b, :, j * LANES:(j + 1) * LANES] = stage[0][j, pl.ds(b, T, stride=B), :]


def _stage_scratch(tm, batch_major):
    return [pltpu.VMEM((D_MODEL // LANES, tm, LANES), F32)] if batch_major else []


def _rows_or_batch_spec(tm, B, batch_major):
    if batch_major:
        return pl.BlockSpec((B, tm // B, D_MODEL), lambda i: (0, i, 0))
    return _row_spec(tm, D_MODEL)


def _ffn_body(x_ref, g_ref, win_ref, cw_ref, cb_ref, wout_ref, c0_ref, gf_ref,
              o_ref, cnew_ref, carry_ref, *stage, B, TM, final_norm):
    hist = (FFN_CONV - 1) * B

    @pl.when(pl.program_id(0) == 0)
    def _():
        carry_ref[...] = c0_ref[...]

    x = x_ref[...]
    hu = _bdot(_rms(x, g_ref[...]), win_ref[...])
    ext = jnp.concatenate([carry_ref[...], hu[:, :D_FF]], axis=0)
    conv = cb_ref[...] + ext[0:TM] * cw_ref[0:1, :]
    for k in range(1, FFN_CONV):
        conv = conv + ext[k * B:k * B + TM] * cw_ref[k:k + 1, :]
    carry_ref[...] = ext[TM:TM + hist]
    y = x + _bdot((conv * _sigmoid(conv)) * hu[:, D_FF:], wout_ref[...])
    if final_norm:
        y = _rms(y, gf_ref[...])
    _store_rows(o_ref, stage, y, B=B, TM=TM)
    cnew_ref[...] = carry_ref[...]


def _layer_spec(shape, layer):
    nd = len(shape) - 1
    return pl.BlockSpec((None,) + tuple(shape[1:]), lambda i, _l=layer, _n=nd: (_l,) + (0,) * _n,
                        pipeline_mode=pl.Buffered(1))


def _ffn(x, g, win_all, cw, cb, wout_all, c0, gf, *, layer, B, TM, final_norm, batch_major_out):
    n = x.shape[0]
    out_rows = (jax.ShapeDtypeStruct((B, n // B, D_MODEL), F32) if batch_major_out
                else jax.ShapeDtypeStruct((n, D_MODEL), F32))
    hist = (FFN_CONV - 1) * B
    body = functools.partial(_ffn_body, B=B, TM=TM, final_norm=final_norm)
    return pl.pallas_call(
        body,
        grid=(n // TM,),
        in_specs=[_row_spec(TM, D_MODEL), _const_spec((1, D_MODEL)), _layer_spec(win_all.shape, layer),
                  _const_spec(cw.shape), _const_spec(cb.shape), _layer_spec(wout_all.shape, layer),
                  _const_spec(c0.shape), _const_spec((1, D_MODEL))],
        out_specs=[_rows_or_batch_spec(TM, B, batch_major_out), _const_spec((hist, D_FF))],
        out_shape=[out_rows, jax.ShapeDtypeStruct((hist, D_FF), F32)],
        scratch_shapes=[pltpu.VMEM((hist, D_FF), F32)] + _stage_scratch(TM, batch_major_out),
        compiler_params=_cparams(),
        name="conv_ffn",
    )(x, g, win_all, cw, cb, wout_all, c0, gf)


def _s5_body(x_ref, g_ref, wbr_ref, wbi_ref, cfr_ref, cfi_ref, abr_ref, abi_ref, wcr_ref, wci_ref,
             d_ref, wglu_ref, h0r_ref, h0i_ref, o_ref, hr_out, hi_out,
             xr_s, xi_s, hr_s, hi_s, y_s, *stage, B, TM, state_t):
    @pl.when(pl.program_id(0) == 0)
    def _():
        cr, ci = cfr_ref[...], cfi_ref[...]
        h0r, h0i = h0r_ref[...], h0i_ref[...]
        if state_t:
            h0r, h0i = h0r.T, h0i.T
        den = cr * cr + ci * ci
        hr_s[...] = (h0r * cr + h0i * ci) / den
        hi_s[...] = (h0i * cr - h0r * ci) / den

    x = _load_rows(x_ref, stage, B=B, TM=TM)
    u = _rms(x, g_ref[...])
    ub = u.astype(BF16)
    steps = TM // B
    for s in range(S5_SLABS):
        us = ub[:, s * LANES:(s + 1) * LANES]
        xr_s[s] = jnp.dot(us, wbr_ref[s], preferred_element_type=F32)
        xi_s[s] = jnp.dot(us, wbi_ref[s], preferred_element_type=F32)
    for s in range(S5_SLABS):
        sl = slice(s * S5_SLAB_STATE, (s + 1) * S5_SLAB_STATE)
        ar = jnp.broadcast_to(abr_ref[:, sl], (B, S5_SLAB_STATE))
        ai = jnp.broadcast_to(abi_ref[:, sl], (B, S5_SLAB_STATE))
        hr, hi = hr_s[:, sl], hi_s[:, sl]
        for t in range(steps):
            rows = slice(t * B, (t + 1) * B)
            hr, hi = ((ar * hr - ai * hi) + xr_s[s, rows, :],
                      (ar * hi + ai * hr) + xi_s[s, rows, :])
            xr_s[s, rows, :] = hr
            xi_s[s, rows, :] = hi
        hr_s[:, sl] = hr
        hi_s[:, sl] = hi
        y_s[:, s * LANES:(s + 1) * LANES] = (
            jnp.dot(xr_s[s].astype(BF16), wcr_ref[s], preferred_element_type=F32)
            - jnp.dot(xi_s[s].astype(BF16), wci_ref[s], preferred_element_type=F32))
    y = y_s[...] + d_ref[...] * u
    z = _bdot(jax.nn.gelu(y), wglu_ref[...])
    o_ref[...] = x + z[:, :D_MODEL] * _sigmoid(z[:, D_MODEL:])
    cr, ci = cfr_ref[...], cfi_ref[...]
    hr_new = cr * hr_s[...] - ci * hi_s[...]
    hi_new = cr * hi_s[...] + ci * hr_s[...]
    hr_out[...] = hr_new.T if state_t else hr_new
    hi_out[...] = hi_new.T if state_t else hi_new


def _s5(x, g, prm, wglu_all, h0r, h0i, *, layer, B, TM, batch_major_in, state_t):
    n = x.shape[0] * x.shape[1] if batch_major_in else x.shape[0]
    body = functools.partial(_s5_body, B=B, TM=TM, state_t=state_t)
    st_shape = (S5_NSTATE, B) if state_t else (B, S5_NSTATE)
    consts = [g, prm['wbr'], prm['wbi'], prm['cfr'], prm['cfi'], prm['abr'], prm['abi'],
              prm['wcr'], prm['wci'], prm['d'], wglu_all, h0r, h0i]
    specs = [_layer_spec(a.shape, layer) if a is wglu_all else _const_spec(a.shape) for a in consts]
    return pl.pallas_call(
        body,
        grid=(n // TM,),
        in_specs=[_rows_or_batch_spec(TM, B, batch_major_in)] + specs,
        out_specs=[_row_spec(TM, D_MODEL), _const_spec(st_shape), _const_spec(st_shape)],
        out_shape=[jax.ShapeDtypeStruct((n, D_MODEL), F32),
                   jax.ShapeDtypeStruct(st_shape, F32), jax.ShapeDtypeStruct(st_shape, F32)],
        scratch_shapes=[pltpu.VMEM((S5_SLABS, TM, S5_SLAB_STATE), F32),
                        pltpu.VMEM((S5_SLABS, TM, S5_SLAB_STATE), F32),
                        pltpu.VMEM((B, S5_NSTATE), F32), pltpu.VMEM((B, S5_NSTATE), F32),
                        pltpu.VMEM((TM, D_MODEL), F32)] + _stage_scratch(TM, batch_major_in),
        compiler_params=_cparams(),
        name="s5_mixer",
    )(x, *consts)


def _s5_params(p):
    lam_re = jnp.minimum(p['s5_a_re'], -1e-4)
    lam_im = p['s5_a_im']
    dt = jnp.exp(p['s5_log_dt'])[:, None]
    mag = jnp.exp(lam_re * dt)
    ab_re = mag * jnp.cos(lam_im * dt)
    ab_im = mag * jnp.sin(lam_im * dt)
    den = lam_re * lam_re + lam_im * lam_im
    coef_re = ((ab_re - 1.0) * lam_re + ab_im * lam_im) / den
    coef_im = (ab_im * lam_re - (ab_re - 1.0) * lam_im) / den
    gps = S5_GROUPS // S5_SLABS
    same_group = (jnp.arange(gps * S5_GROUP)[:, None] // S5_GROUP
                  == jnp.arange(gps * S5_STATE)[None, :] // S5_STATE)

    def slab_in(b):
        rows = jnp.transpose(b.reshape(S5_SLABS, gps, S5_STATE, S5_GROUP), (0, 1, 3, 2))
        rows = rows.reshape(S5_SLABS, gps * S5_GROUP, S5_STATE)
        return jnp.where(same_group, jnp.tile(rows, (1, 1, gps)), 0.0).astype(BF16)

    def slab_out(c):
        cols = jnp.transpose(c.reshape(S5_SLABS, gps, S5_GROUP, S5_STATE), (0, 3, 1, 2))
        cols = cols.reshape(S5_SLABS, S5_STATE, gps * S5_GROUP)
        return jnp.where(same_group.T, jnp.tile(cols, (1, gps, 1)), 0.0).astype(BF16)

    flat = lambda a: a.reshape(1, S5_NSTATE)
    c_re, c_im = p['s5_c_re'], p['s5_c_im']
    cc_re = c_re * coef_re[:, None, :] - c_im * coef_im[:, None, :]
    cc_im = c_re * coef_im[:, None, :] + c_im * coef_re[:, None, :]
    return dict(wbr=slab_in(p['s5_b_re']), wbi=slab_in(p['s5_b_im']),
                wcr=slab_out(cc_re), wci=slab_out(cc_im),
                cfr=flat(coef_re), cfi=flat(coef_im), abr=flat(ab_re), abi=flat(ab_im),
                d=p['s5_d'].reshape(1, D_MODEL))


def _lru_body(x_ref, g_ref, win_ref, cw_ref, cb_ref, wrg_ref, brg_ref, wig_ref, big_ref, lam_ref,
              wout_ref, c0_ref, h0_ref, o_ref, cnew_ref, hnew_ref,
              carry_s, h_s, a_s, bx_s, *, B, TM):
    hist = (LRU_CONV - 1) * B

    @pl.when(pl.program_id(0) == 0)
    def _():
        carry_s[...] = c0_ref[...]
        h_s[...] = h0_ref[...]

    x = x_ref[...]
    xn = _rms(x, g_ref[...])
    gy = _bdot(xn, win_ref[...])
    gate_br = jax.nn.gelu(gy[:, :D_RNN])
    ext = jnp.concatenate([carry_s[...], gy[:, D_RNN:]], axis=0)
    u = cb_ref[...] + ext[0:TM] * cw_ref[0:1, :]
    for k in range(1, LRU_CONV):
        u = u + ext[k * B:k * B + TM] * cw_ref[k:k + 1, :]
    carry_s[...] = ext[TM:TM + hist]
    ub = u.astype(BF16)
    rg_parts, ig_parts = [], []
    for nb in range(LRU_BLOCKS):
        blk = ub[:, nb * LRU_BLOCK:(nb + 1) * LRU_BLOCK]
        rg_parts.append(jnp.dot(blk, wrg_ref[nb], preferred_element_type=F32))
        ig_parts.append(jnp.dot(blk, wig_ref[nb], preferred_element_type=F32))
    rg = _sigmoid(jnp.concatenate(rg_parts, axis=1) + brg_ref[...])
    ig = _sigmoid(jnp.concatenate(ig_parts, axis=1) + big_ref[...])
    log_sig = -_softplus(-lam_ref[...])
    log_a = LRU_C * rg * log_sig
    a = jnp.exp(log_a)
    a_s[...] = a
    bx_s[...] = jnp.sqrt(1.0 - a * a) * ig * u

    h_last = h_s[...]
    for t in range(TM // B):
        rows = slice(t * B, (t + 1) * B)
        h_last = a_s[rows, :] * h_last + bx_s[rows, :]
        bx_s[rows, :] = h_last
    h_s[...] = h_last
    o_ref[...] = x + _bdot(bx_s[...] * gate_br, wout_ref[...])
    cnew_ref[...] = carry_s[...]
    hnew_ref[...] = h_last


def _lru(x, g, prm, c0, h0, *, B, TM):
    n = x.shape[0]
    hist = (LRU_CONV - 1) * B
    body = functools.partial(_lru_body, B=B, TM=TM)
    consts = [g, prm['win'], prm['cw'], prm['cb'], prm['wrg'], prm['brg'], prm['wig'], prm['big'],
              prm['lam'], prm['wout'], c0, h0]
    return pl.pallas_call(
        body,
        grid=(n // TM,),
        in_specs=[_row_spec(TM, D_MODEL)] + [_const_spec(a.shape) for a in consts],
        out_specs=[_row_spec(TM, D_MODEL), _const_spec((hist, D_RNN)), _const_spec((B, D_RNN))],
        out_shape=[jax.ShapeDtypeStruct((n, D_MODEL), F32),
                   jax.ShapeDtypeStruct((hist, D_RNN), F32),
                   jax.ShapeDtypeStruct((B, D_RNN), F32)],
        scratch_shapes=[pltpu.VMEM((hist, D_RNN), F32), pltpu.VMEM((B, D_RNN), F32),
                        pltpu.VMEM((TM, D_RNN), F32), pltpu.VMEM((TM, D_RNN), F32)],
        compiler_params=_cparams(),
        name="rglru_mixer",
    )(x, *consts)


def _lru_params(p):
    row = lambda a: a.reshape(1, D_RNN)
    return dict(win=p['lru_w_in'].astype(BF16), cw=p['lru_conv_w'], cb=row(p['lru_conv_b']),
                wrg=p['lru_w_rg'].astype(BF16), brg=row(p['lru_b_rg']),
                wig=p['lru_w_ig'].astype(BF16), big=row(p['lru_b_ig']),
                lam=row(p['lru_lambda']), wout=p['lru_w_out'].astype(BF16))


NCOL = D_MODEL // LANES


def _emit_cols(o_ref, val):
    for j in range(NCOL):
        o_ref[j] = val[:, j * LANES:(j + 1) * LANES]


def _load_cols(ref):
    return jnp.concatenate([ref[j] for j in range(NCOL)], axis=1)


def _rw_pre_body(x_ref, g_ref, mu_ref, wrkv_ref, w0_ref, w1_ref, w2_ref, a0_ref, a1_ref, a2_ref,
                 g1_ref, g2_ref, kk_ref, ka_ref, sel_ref, selt_ref, sh0_ref,
                 r_o, cum_o, lw_o, k_o, v_o, kk_o, b_o, g_o, sh_o, sh_s, *, B, T):
    TM = T * B

    @pl.when(pl.program_id(0) == 0)
    def _():
        sh_s[...] = sh0_ref[...]

    def emit(o_ref, val):
        if T == 1:
            o_ref[...] = val
        else:
            _emit_cols(o_ref, val)

    xn = _rms(x_ref[...], g_ref[...])
    if TM > B:
        prev = jnp.concatenate([sh_s[...], xn[:TM - B]], axis=0)
    else:
        prev = sh_s[...]
    sh_s[...] = xn[TM - B:]
    xx = prev - xn
    mix = lambda n: xn + xx * mu_ref[n:n + 1, :]
    emit(r_o, _bdot(mix(0), wrkv_ref[0]))
    a = _sigmoid(a0_ref[...] + _bdot(_bdot(mix(4), a1_ref[...]), a2_ref[...]))
    k = _bdot(mix(1), wrkv_ref[1])
    kk = k * kk_ref[...]
    n2 = _head_sum(kk * kk, sel_ref[...], selt_ref[...])
    emit(g_o, _bdot(_sigmoid(_bdot(mix(5), g1_ref[...])), g2_ref[...]))
    wl = w0_ref[...] + _bdot(jnp.tanh(_bdot(mix(3), w1_ref[...])), w2_ref[...])
    lw = -jnp.exp(-_softplus(-wl) - 0.5)
    emit(lw_o, lw)
    cum = lw
    sh = B
    while sh < TM:
        cum = cum + jnp.concatenate([jnp.zeros((sh, D_MODEL), F32), cum[:TM - sh]], axis=0)
        sh *= 2
    emit(cum_o, cum)
    emit(v_o, _bdot(mix(2), wrkv_ref[2]))
    kk = kk * lax.rsqrt(jnp.maximum(n2, 1e-24))
    emit(kk_o, kk)
    emit(b_o, kk * a)
    emit(k_o, k * (1.0 + (a - 1.0) * ka_ref[...]))
    sh_o[...] = sh_s[...]


def _cols_spec(tm):
    return pl.BlockSpec((NCOL, tm, LANES), lambda i: (0, i, 0))


def _rw_pre(x, g, prm, sh0, *, B, T):
    n = x.shape[0]
    tm = T * B
    body = functools.partial(_rw_pre_body, B=B, T=T)
    consts = [g, prm['mu'], prm['wrkv'], prm['w0'], prm['w1'], prm['w2'], prm['a0'], prm['a1'],
              prm['a2'], prm['g1'], prm['g2'], prm['k_k'], prm['k_a'], prm['sel'], prm['sel_t'], sh0]
    if T == 1:
        big, big_spec = jax.ShapeDtypeStruct((n, D_MODEL), F32), _row_spec(tm, D_MODEL)
    else:
        big, big_spec = jax.ShapeDtypeStruct((NCOL, n, LANES), F32), _cols_spec(tm)
    return pl.pallas_call(
        body,
        grid=(n // tm,),
        in_specs=[_row_spec(tm, D_MODEL)] + [_const_spec(a.shape) for a in consts],
        out_specs=[big_spec] * 8 + [_const_spec((B, D_MODEL))],
        out_shape=[big] * 8 + [jax.ShapeDtypeStruct((B, D_MODEL), F32)],
        scratch_shapes=[pltpu.VMEM((B, D_MODEL), F32)],
        compiler_params=_cparams(),
        name="rwkv_project",
    )(x, *consts)


def _rw_post_body(x_ref, o_ref, r_ref, k_ref, v_ref, g_ref, rk_ref, lnw_ref, lnb_ref, sel_ref,
                  selt_ref, wo_ref, out_ref, *, T):
    load = (lambda ref: ref[...]) if T == 1 else _load_cols
    o, r, k, v, g = (load(ref) for ref in (o_ref, r_ref, k_ref, v_ref, g_ref))
    hsum = functools.partial(_head_sum, sel=sel_ref[...], sel_t=selt_ref[...])
    mean = hsum(o) * (1.0 / RWKV_HEAD)
    dlt = o - mean
    var = hsum(dlt * dlt) * (1.0 / RWKV_HEAD)
    on = dlt * lax.rsqrt(var + RWKV_GN_EPS) * lnw_ref[...] + lnb_ref[...]
    bonus = hsum(r * k * rk_ref[...]) * v
    out_ref[...] = x_ref[...] + _bdot((on + bonus) * g, wo_ref[...])


def _rw_post(x, o, r, k, v, g, prm, *, B, T):
    n = x.shape[0]
    tm = T * B
    consts = [prm['r_k'], prm['ln_w'], prm['ln_b'], prm['sel'], prm['sel_t'], prm['wo']]
    big_spec = _row_spec(tm, D_MODEL) if T == 1 else _cols_spec(tm)
    return pl.pallas_call(
        functools.partial(_rw_post_body, T=T),
        grid=(n // tm,),
        in_specs=([_row_spec(tm, D_MODEL)] + [big_spec] * 5
                  + [_const_spec(a.shape) for a in consts]),
        out_specs=_row_spec(tm, D_MODEL),
        out_shape=jax.ShapeDtypeStruct((n, D_MODEL), F32),
        compiler_params=_cparams(),
        name="rwkv_output",
    )(x, o, r, k, v, g, *consts)


def _wkv_chunk_body(r_ref, cum_ref, lw_ref, k_ref, v_ref, kk_ref, b_ref, s0_ref, o_ref, sT_ref,
                    S_s, *, B, T):
    @pl.when(pl.program_id(0) == 0)
    def _():
        S_s[...] = s0_ref[...]

    lane = lax.broadcasted_iota(jnp.int32, (T, LANES), 1)
    trow = lax.broadcasted_iota(jnp.int32, (T, LANES), 0)
    first = lane < RWKV_HEAD
    strict = trow > (lane & (T - 1))
    incl = trow >= (lane & (T - 1))
    ri = lax.broadcasted_iota(jnp.int32, (LANES, LANES), 0)
    ci = lax.broadcasted_iota(jnp.int32, (LANES, LANES), 1)
    same_head = (ri < RWKV_HEAD) == (ci < RWKV_HEAD)

    def stack(y):
        return jnp.concatenate([jnp.where(first, y, 0.0), jnp.where(first, 0.0, y)],
                               axis=0).astype(BF16)

    nt = (((1,), (1,)), ((), ()))
    tn = (((0,), (0,)), ((), ()))
    nsteps = T.bit_length() - 1

    def per_group(g, carry):
        chains = [(g * WKV_ROWS + i, hp) for i in range(WKV_ROWS) for hp in range(RWKV_PAIRS)]
        idx = range(len(chains))
        at_chain = [(hp, pl.ds(b, T, stride=B), slice(None)) for b, hp in chains]
        s_old = [S_s[b, hp] for b, hp in chains]
        ks, vs, bbs, ends, e_ends, ps, sps = [], [], [], [], [], [], []
        for c in idx:
            at_c = at_chain[c]
            cm = cum_ref[at_c]
            k = k_ref[at_c]
            bb = b_ref[at_c]
            mid = cm[T // 2 - 1:T // 2, :]
            end = cm[T - 1:T, :]
            at = -kk_ref[at_c] * jnp.exp((cm - lw_ref[at_c]) - mid)
            rt = r_ref[at_c] * jnp.exp(cm - mid)
            e_neg = jnp.exp(mid - cm)
            lhs_f = jnp.concatenate([at, rt], axis=0)
            rhs = jnp.concatenate([stack(bb * e_neg), stack(k * e_neg)], axis=0)
            ps.append(lax.dot_general(lhs_f.astype(BF16), rhs, nt, preferred_element_type=F32))
            sps.append(lax.dot_general((lhs_f * jnp.exp(mid)).astype(BF16), s_old[c].astype(BF16),
                                       nt, preferred_element_type=F32))
            ks.append(k)
            vs.append(v_ref[at_c])
            bbs.append(bb)
            ends.append(end)
            e_ends.append(jnp.exp(end - cm))
        vst = [stack(v) for v in vs]
        ys = [sps[c][0:T] + jnp.dot(jnp.where(strict, ps[c][0:T, 2 * T:4 * T], 0.0).astype(BF16),
                                    vst[c], preferred_element_type=F32) for c in idx]
        ms = [jnp.where(strict, ps[c][0:T, 0:2 * T], 0.0) for c in idx]
        for it in range(nsteps):
            if it + 1 < nsteps:
                both = [jnp.dot(ms[c].astype(BF16),
                                jnp.concatenate([stack(ys[c]), stack(ms[c])], axis=1),
                                preferred_element_type=F32) for c in idx]
                ys = [ys[c] + both[c][:, 0:LANES] for c in idx]
                ms = [both[c][:, LANES:2 * LANES] for c in idx]
            else:
                ys = [ys[c] + jnp.dot(ms[c].astype(BF16), stack(ys[c]), preferred_element_type=F32)
                      for c in idx]
        outs, news = [], []
        for c in idx:
            rbk = jnp.where(jnp.concatenate([incl, incl], axis=1), ps[c][T:2 * T, :], 0.0)
            outs.append(sps[c][T:2 * T]
                        + jnp.dot(rbk.astype(BF16), jnp.concatenate([stack(ys[c]), vst[c]], axis=0),
                                  preferred_element_type=F32))
            upd = lax.dot_general(
                jnp.concatenate([ys[c], vs[c]], axis=0).astype(BF16),
                jnp.concatenate([bbs[c] * e_ends[c], ks[c] * e_ends[c]], axis=0).astype(BF16),
                tn, preferred_element_type=F32)
            news.append(s_old[c] * jnp.exp(ends[c]) + jnp.where(same_head, upd, 0.0))
        for c in idx:
            o_ref[at_chain[c]] = outs[c]
            S_s[chains[c][0], chains[c][1]] = news[c]
        return carry

    lax.fori_loop(0, B // WKV_ROWS, per_group, 0)

    @pl.when(pl.program_id(0) == pl.num_programs(0) - 1)
    def _():
        for b in range(B):
            for hp in range(RWKV_PAIRS):
                blk = S_s[b, hp]
                sT_ref[b, 2 * hp] = blk[0:RWKV_HEAD, 0:RWKV_HEAD]
                sT_ref[b, 2 * hp + 1] = pltpu.roll(blk, RWKV_HEAD, axis=1)[RWKV_HEAD:, 0:RWKV_HEAD]


def _wkv_chunk(r, cum, lw, k, v, kk, bvec, s0, *, B, T):
    n = r.shape[1]
    tm = T * B
    body = functools.partial(_wkv_chunk_body, B=B, T=T)
    st_shape = (B, RWKV_PAIRS, LANES, LANES)
    head_shape = (B, RWKV_HEADS, RWKV_HEAD, RWKV_HEAD)
    return pl.pallas_call(
        body,
        grid=(n // tm,),
        in_specs=[_cols_spec(tm)] * 7 + [_const_spec(st_shape)],
        out_specs=[_cols_spec(tm), _const_spec(head_shape)],
        out_shape=[jax.ShapeDtypeStruct((NCOL, n, LANES), F32),
                   jax.ShapeDtypeStruct(head_shape, F32)],
        scratch_shapes=[pltpu.VMEM(st_shape, F32)],
        compiler_params=_cparams(),
        name="wkv_chunked",
    )(r, cum, lw, k, v, kk, bvec, s0)


def _wkv_step_body(s_ref, r_ref, lw_ref, k_ref, kk_ref, b_ref, v_ref, o_ref, sn_ref):
    w = jnp.exp(lw_ref[...])
    kk, bv, k, r = kk_ref[...], b_ref[...], k_ref[...], r_ref[...]

    def per_value_row(i, carry):
        s = s_ref[i]
        sa = -jnp.sum(s * kk, axis=0, keepdims=True)
        sn = s * w + sa * bv + v_ref[pl.ds(i, 1), :] * k
        sn_ref[i] = sn
        o_ref[pl.ds(i, 1), :] = jnp.sum(sn * r, axis=0, keepdims=True)
        return carry

    lax.fori_loop(0, RWKV_HEAD, per_value_row, 0, unroll=8)


def _wkv_step(s0, r, lw, k, kk, bvec, v):
    bsz = s0.shape[-1]
    vec = lambda t: t.T.reshape(RWKV_HEADS, RWKV_HEAD, bsz)
    s_spec = pl.BlockSpec((None, RWKV_HEAD, RWKV_HEAD, bsz), lambda h: (h, 0, 0, 0))
    v_spec = pl.BlockSpec((None, RWKV_HEAD, bsz), lambda h: (h, 0, 0))
    o, sn = pl.pallas_call(
        _wkv_step_body,
        grid=(RWKV_HEADS,),
        in_specs=[s_spec] + [v_spec] * 6,
        out_specs=[v_spec, s_spec],
        out_shape=[jax.ShapeDtypeStruct((RWKV_HEADS, RWKV_HEAD, bsz), F32),
                   jax.ShapeDtypeStruct(s0.shape, F32)],
        compiler_params=_cparams(),
        name="wkv_step",
    )(s0, vec(r), vec(lw), vec(k), vec(kk), vec(bvec), vec(v))
    return o.reshape(D_MODEL, bsz).T, sn


def _rw_params(p):
    row = lambda a: a.reshape(1, D_MODEL)
    pad_c = lambda w: jnp.pad(w, ((0, 0), (0, LORA_PAD - w.shape[1]))).astype(BF16)
    pad_r = lambda w: jnp.pad(w, ((0, LORA_PAD - w.shape[0]), (0, 0))).astype(BF16)
    sel = (jnp.arange(D_MODEL)[:, None] // RWKV_HEAD == jnp.arange(LANES)[None, :]).astype(BF16)
    return dict(mu=p['rw_mu'], wrkv=p['rw_w_rkv'].astype(BF16), w0=row(p['rw_w0']),
                w1=pad_c(p['rw_w1']), w2=pad_r(p['rw_w2']), a0=row(p['rw_a0']),
                a1=pad_c(p['rw_a1']), a2=pad_r(p['rw_a2']), g1=pad_c(p['rw_g1']),
                g2=pad_r(p['rw_g2']), k_k=row(p['rw_k_k']), k_a=row(p['rw_k_a']),
                r_k=row(p['rw_r_k']), ln_w=row(p['rw_ln_w']), ln_b=row(p['rw_ln_b']),
                wo=p['rw_w_o'].astype(BF16), sel=sel, sel_t=sel.T)


def _pair_states(s):
    bsz = s.shape[0]
    s5 = s.reshape(bsz, RWKV_PAIRS, 2, RWKV_HEAD, RWKV_HEAD)
    eye = jnp.eye(2, dtype=s.dtype)
    bd = jnp.einsum('bpqij,qr->bpqirj', s5, eye)
    return bd.reshape(bsz, RWKV_PAIRS, LANES, LANES)


def _rwkv(x, g, prm, sh0, s0, *, B, L):
    T = min(L, WKV_CHUNK)
    r, cum, lw, k, v, kk, bvec, gt, sh_new = _rw_pre(x, g, prm, sh0, B=B, T=T)
    if L == 1:
        o, s_t = _wkv_step(jnp.transpose(s0, (1, 2, 3, 0)), r, lw, k, kk, bvec, v)
        s_new = jnp.transpose(s_t, (3, 0, 1, 2))
    else:
        o, s_new = _wkv_chunk(r, cum, lw, k, v, kk, bvec, _pair_states(s0), B=B, T=T)
    y = _rw_post(x, o, r, k, v, gt, prm, B=B, T=T)
    return y, sh_new, s_new


def _trunk(x, st, w, *, B, L):
    TM = B * min(L, WKV_CHUNK)
    TM_FFN = B * min(L, FFN_STEPS)
    regroup = L > 1
    if not regroup:
        x = x.reshape(B, D_MODEL)
    row = lambda a: a.reshape(1, -1)
    new = {k: [] for k in ('s5_re', 's5_im', 'rw_wkv', 'rw_shift', 'lru_h', 'lru_conv', 'ffn_conv')}
    for i in range(DEPTH):
        kind, j = i % N_MIXERS, i // N_MIXERS
        g = row(w['norm_mix'][i])
        if kind == 0:
            x, hr, hi = _s5(x, g, w['s5'][j], w['s5_w_glu'], st['s5_re'][j], st['s5_im'][j],
                            layer=j, B=B, TM=TM, batch_major_in=(regroup and i == 0),
                            state_t=(L == 1))
            new['s5_re'].append(hr)
            new['s5_im'].append(hi)
        elif kind == 1:
            x, sh, s = _rwkv(x, g, w['rw'][j], st['rw_shift'][j], st['rw_wkv'][j], B=B, L=L)
            new['rw_shift'].append(sh)
            new['rw_wkv'].append(s)
        else:
            x, cb, hl = _lru(x, g, w['lru'][j], st['lru_conv'][j], st['lru_h'][j], B=B, TM=TM)
            new['lru_conv'].append(cb)
            new['lru_h'].append(hl)
        x, cb = _ffn(x, row(w['norm_ffn'][i]), w['ffn_w_in'], w['ffn_conv_w'][i],
                     row(w['ffn_conv_b'][i]), w['ffn_w_out'], st['ffn_conv'][i],
                     row(w['norm_final']), layer=i, B=B, TM=TM_FFN, final_norm=(i == DEPTH - 1),
                     batch_major_out=(regroup and i == DEPTH - 1))
        new['ffn_conv'].append(cb)
    return x.reshape(B, L, D_MODEL), new


def _stack(parts):
    return parts[0][None] if len(parts) == 1 else jnp.stack(parts)


def _time_major_hist(buf):
    n, bsz, wm1, c = buf.shape
    t = jnp.transpose(buf, (0, 2, 1, 3)).reshape(n, wm1 * bsz, c)
    return [t[j] for j in range(n)]


def _batch_major_hist(rows, bsz):
    t = _stack(rows)
    n, _, c = t.shape
    return jnp.transpose(t.reshape(n, -1, bsz, c), (0, 2, 1, 3))


def _run(x, st, w):
    bsz, length, _ = x.shape
    flat = lambda a: [a[j].reshape(bsz, -1) for j in range(a.shape[0])]
    if length == 1:
        s5_in = lambda a: [jnp.transpose(a[j], (1, 2, 0)).reshape(S5_NSTATE, bsz)
                           for j in range(a.shape[0])]
        s5_out = lambda parts: jnp.transpose(
            _stack(parts).reshape(len(parts), S5_GROUPS, S5_STATE, bsz), (0, 3, 1, 2))
    else:
        s5_in = flat
        s5_out = lambda parts: _stack(parts).reshape(len(parts), bsz, S5_GROUPS, S5_STATE)
    stt = dict(s5_re=s5_in(st['s5_re']), s5_im=s5_in(st['s5_im']),
               rw_wkv=[st['rw_wkv'][j] for j in range(st['rw_wkv'].shape[0])],
               rw_shift=flat(st['rw_shift']), lru_h=flat(st['lru_h']),
               lru_conv=_time_major_hist(st['lru_conv']), ffn_conv=_time_major_hist(st['ffn_conv']))
    y, new = _trunk(x, stt, w, B=bsz, L=length)
    out = dict(
        s5_re=s5_out(new['s5_re']), s5_im=s5_out(new['s5_im']),
        rw_wkv=_stack(new['rw_wkv']), rw_shift=_stack(new['rw_shift']),
        lru_h=_stack(new['lru_h']), lru_conv=_batch_major_hist(new['lru_conv'], bsz),
        ffn_conv=_batch_major_hist(new['ffn_conv'], bsz))
    return y, out


def _prepare_weights(w):
    n_s5 = w['s5_a_re'].shape[0]
    n_rw = w['rw_mu'].shape[0]
    n_lru = w['lru_w_in'].shape[0]
    sub = lambda prefix, j: {k: v[j] for k, v in w.items() if k.startswith(prefix)}
    return dict(
        norm_mix=w['norm_mix'], norm_ffn=w['norm_ffn'], norm_final=w['norm_final'],
        s5=[_s5_params(sub('s5_', j)) for j in range(n_s5)], s5_w_glu=w['s5_w_glu'].astype(BF16),
        rw=[_rw_params(sub('rw_', j)) for j in range(n_rw)],
        lru=[_lru_params(sub('lru_', j)) for j in range(n_lru)],
        ffn_w_in=w['ffn_w_in'].astype(BF16), ffn_conv_w=w['ffn_conv_w'],
        ffn_conv_b=w['ffn_conv_b'], ffn_w_out=w['ffn_w_out'].astype(BF16))


def kernel(x_prompt, x_sample, state_s5_re, state_s5_im, state_rwkv_wkv, state_rwkv_shift, state_lru_h, state_lru_conv, state_ffn_conv, norm_mix, norm_ffn, norm_final, s5_a_re, s5_a_im, s5_log_dt, s5_b_re, s5_b_im, s5_c_re, s5_c_im, s5_d, s5_w_glu, rw_mu, rw_w_rkv, rw_w0, rw_w1, rw_w2, rw_a0, rw_a1, rw_a2, rw_g1, rw_g2, rw_k_k, rw_k_a, rw_r_k, rw_ln_w, rw_ln_b, rw_w_o, lru_w_in, lru_conv_w, lru_conv_b, lru_w_rg, lru_b_rg, lru_w_ig, lru_b_ig, lru_lambda, lru_w_out, ffn_w_in, ffn_conv_w, ffn_conv_b, ffn_w_out):
    w = _prepare_weights(dict(
        norm_mix=norm_mix, norm_ffn=norm_ffn, norm_final=norm_final,
        s5_a_re=s5_a_re, s5_a_im=s5_a_im, s5_log_dt=s5_log_dt, s5_b_re=s5_b_re, s5_b_im=s5_b_im,
        s5_c_re=s5_c_re, s5_c_im=s5_c_im, s5_d=s5_d, s5_w_glu=s5_w_glu,
        rw_mu=rw_mu, rw_w_rkv=rw_w_rkv, rw_w0=rw_w0, rw_w1=rw_w1, rw_w2=rw_w2, rw_a0=rw_a0,
        rw_a1=rw_a1, rw_a2=rw_a2, rw_g1=rw_g1, rw_g2=rw_g2, rw_k_k=rw_k_k, rw_k_a=rw_k_a,
        rw_r_k=rw_r_k, rw_ln_w=rw_ln_w, rw_ln_b=rw_ln_b, rw_w_o=rw_w_o,
        lru_w_in=lru_w_in, lru_conv_w=lru_conv_w, lru_conv_b=lru_conv_b, lru_w_rg=lru_w_rg,
        lru_b_rg=lru_b_rg, lru_w_ig=lru_w_ig, lru_b_ig=lru_b_ig, lru_lambda=lru_lambda,
        lru_w_out=lru_w_out, ffn_w_in=ffn_w_in, ffn_conv_w=ffn_conv_w, ffn_conv_b=ffn_conv_b,
        ffn_w_out=ffn_w_out))
    bsz, dt = x_prompt.shape[0], x_prompt.dtype
    n_s5, n_rw, n_lru = state_s5_re.shape[0], state_rwkv_wkv.shape[0], state_lru_h.shape[0]
    st_prompt = dict(
        s5_re=jnp.zeros((n_s5, bsz, S5_GROUPS, S5_STATE), dt),
        s5_im=jnp.zeros((n_s5, bsz, S5_GROUPS, S5_STATE), dt),
        rw_wkv=jnp.zeros((n_rw, bsz, RWKV_HEADS, RWKV_HEAD, RWKV_HEAD), dt),
        rw_shift=jnp.zeros((n_rw, bsz, D_MODEL), dt),
        lru_h=jnp.zeros((n_lru, bsz, D_RNN), dt),
        lru_conv=jnp.zeros((n_lru, bsz, LRU_CONV - 1, D_RNN), dt),
        ffn_conv=jnp.zeros((DEPTH, bsz, FFN_CONV - 1, D_FF), dt))
    st_sample = dict(s5_re=state_s5_re, s5_im=state_s5_im, rw_wkv=state_rwkv_wkv,
                     rw_shift=state_rwkv_shift, lru_h=state_lru_h, lru_conv=state_lru_conv,
                     ffn_conv=state_ffn_conv)
    y_p, new_p = _run(x_prompt, st_prompt, w)
    y_s, new_s = _run(x_sample, st_sample, w)
    return (y_p, y_s, new_p['s5_re'], new_s['s5_re'], new_p['s5_im'], new_s['s5_im'],
            new_p['rw_wkv'], new_s['rw_wkv'], new_p['rw_shift'], new_s['rw_shift'],
            new_p['lru_h'], new_s['lru_h'], new_p['lru_conv'], new_s['lru_conv'],
            new_p['ffn_conv'], new_s['ffn_conv'])
```

```python
import functools

import jax
import jax.numpy as jnp
from jax import lax
from jax.experimental import pallas as pl
from jax.experimental.pallas import tpu as pltpu

F32 = jnp.float32
BF16 = jnp.bfloat16

D_MODEL = 1024
DEPTH = 4
N_MIXERS = 3
RMS_EPS = 1e-6

S5_GROUP = 16
S5_GROUPS = D_MODEL // S5_GROUP
S5_STATE = 64
S5_SLABS = 8
S5_SLAB_STATE = (S5_GROUPS // S5_SLABS) * S5_STATE
S5_NSTATE = S5_GROUPS * S5_STATE

RWKV_HEAD = 64
RWKV_HEADS = D_MODEL // RWKV_HEAD
RWKV_PAIRS = RWKV_HEADS // 2
RWKV_GN_EPS = 64e-5
LORA_PAD = 128

D_RNN = D_MODEL
LRU_BLOCKS = 4
LRU_BLOCK = D_RNN // LRU_BLOCKS
LRU_C = 8.0
LRU_CONV = 4

D_FF = 2816
FFN_CONV = 3
FFN_STEPS = 64

LANES = 128
WKV_CHUNK = 64
WKV_ROWS = 4
VMEM_LIMIT = 60 * 1024 * 1024


def _cparams():
    return pltpu.CompilerParams(dimension_semantics=("arbitrary",), vmem_limit_bytes=VMEM_LIMIT)


def _const_spec(shape):
    nd = len(shape)
    return pl.BlockSpec(shape, lambda i, _n=nd: (0,) * _n, pipeline_mode=pl.Buffered(1))


def _row_spec(tm, width):
    return pl.BlockSpec((tm, width), lambda i: (i, 0))


def _rms(x, g):
    ms = jnp.mean(x * x, axis=-1, keepdims=True)
    return x * lax.rsqrt(ms + RMS_EPS) * g


def _bdot(a, w):
    return jnp.dot(a.astype(BF16), w, preferred_element_type=F32)


def _softplus(z):
    return jnp.maximum(z, 0.0) + jnp.log1p(jnp.exp(-jnp.abs(z)))


def _sigmoid(x):
    return 0.5 * jnp.tanh(0.5 * x) + 0.5


def _head_sum(x, sel, sel_t):
    return _bdot(_bdot(x, sel), sel_t)


def _load_rows(x_ref, stage, *, B, TM):
    if not stage:
        return x_ref[...]
    T = TM // B
    for j in range(D_MODEL // LANES):
        for b in range(B):
            stage[0][j, pl.ds(b, T, stride=B), :] = x_ref[b, :, j * LANES:(j + 1) * LANES]
    return jnp.concatenate([stage[0][j] for j in range(D_MODEL // LANES)], axis=1)


def _store_rows(o_ref, stage, y, *, B, TM):
    if not stage:
        o_ref[...] = y
        return
    T = TM // B
    for j in range(D_MODEL // LANES):
        stage[0][j] = y[:, j * LANES:(j + 1) * LANES]
        for b in range(B):
            o_ref[b, :, j * LANES:(j + 1) * LANES] = stage[0][j, pl.ds(b, T, stride=B), :]


def _stage_scratch(tm, batch_major):
    return [pltpu.VMEM((D_MODEL // LANES, tm, LANES), F32)] if batch_major else []


def _rows_or_batch_spec(tm, B, batch_major):
    if batch_major:
        return pl.BlockSpec((B, tm // B, D_MODEL), lambda i: (0, i, 0))
    return _row_spec(tm, D_MODEL)


def _ffn_body(x_ref, g_ref, win_ref, cw_ref, cb_ref, wout_ref, c0_ref, gf_ref,
              o_ref, cnew_ref, carry_ref, *stage, B, TM, final_norm):
    hist = (FFN_CONV - 1) * B

    @pl.when(pl.program_id(0) == 0)
    def _():
        carry_ref[...] = c0_ref[...]

    x = x_ref[...]
    hu = _bdot(_rms(x, g_ref[...]), win_ref[...])
    ext = jnp.concatenate([carry_ref[...], hu[:, :D_FF]], axis=0)
    conv = cb_ref[...] + ext[0:TM] * cw_ref[0:1, :]
    for k in range(1, FFN_CONV):
        conv = conv + ext[k * B:k * B + TM] * cw_ref[k:k + 1, :]
    carry_ref[...] = ext[TM:TM + hist]
    y = x + _bdot((conv * _sigmoid(conv)) * hu[:, D_FF:], wout_ref[...])
    if final_norm:
        y = _rms(y, gf_ref[...])
    _store_rows(o_ref, stage, y, B=B, TM=TM)
    cnew_ref[...] = carry_ref[...]


def _layer_spec(shape, layer):
    nd = len(shape) - 1
    return pl.BlockSpec((None,) + tuple(shape[1:]), lambda i, _l=layer, _n=nd: (_l,) + (0,) * _n,
                        pipeline_mode=pl.Buffered(1))


def _ffn(x, g, win_all, cw, cb, wout_all, c0, gf, *, layer, B, TM, final_norm, batch_major_out):
    n = x.shape[0]
    out_rows = (jax.ShapeDtypeStruct((B, n // B, D_MODEL), F32) if batch_major_out
                else jax.ShapeDtypeStruct((n, D_MODEL), F32))
    hist = (FFN_CONV - 1) * B
    body = functools.partial(_ffn_body, B=B, TM=TM, final_norm=final_norm)
    return pl.pallas_call(
        body,
        grid=(n // TM,),
        in_specs=[_row_spec(TM, D_MODEL), _const_spec((1, D_MODEL)), _layer_spec(win_all.shape, layer),
                  _const_spec(cw.shape), _const_spec(cb.shape), _layer_spec(wout_all.shape, layer),
                  _const_spec(c0.shape), _const_spec((1, D_MODEL))],
        out_specs=[_rows_or_batch_spec(TM, B, batch_major_out), _const_spec((hist, D_FF))],
        out_shape=[out_rows, jax.ShapeDtypeStruct((hist, D_FF), F32)],
        scratch_shapes=[pltpu.VMEM((hist, D_FF), F32)] + _stage_scratch(TM, batch_major_out),
        compiler_params=_cparams(),
        name="conv_ffn",
    )(x, g, win_all, cw, cb, wout_all, c0, gf)


def _s5_body(x_ref, g_ref, wbr_ref, wbi_ref, cfr_ref, cfi_ref, abr_ref, abi_ref, wcr_ref, wci_ref,
             d_ref, wglu_ref, h0r_ref, h0i_ref, o_ref, hr_out, hi_out,
             xr_s, xi_s, hr_s, hi_s, y_s, *stage, B, TM, state_t):
    @pl.when(pl.program_id(0) == 0)
    def _():
        cr, ci = cfr_ref[...], cfi_ref[...]
        h0r, h0i = h0r_ref[...], h0i_ref[...]
        if state_t:
            h0r, h0i = h0r.T, h0i.T
        den = cr * cr + ci * ci
        hr_s[...] = (h0r * cr + h0i * ci) / den
        hi_s[...] = (h0i * cr - h0r * ci) / den

    x = _load_rows(x_ref, stage, B=B, TM=TM)
    u = _rms(x, g_ref[...])
    ub = u.astype(BF16)
    steps = TM // B
    for s in range(S5_SLABS):
        us = ub[:, s * LANES:(s + 1) * LANES]
        xr_s[s] = jnp.dot(us, wbr_ref[s], preferred_element_type=F32)
        xi_s[s] = jnp.dot(us, wbi_ref[s], preferred_element_type=F32)
    for s in range(S5_SLABS):
        sl = slice(s * S5_SLAB_STATE, (s + 1) * S5_SLAB_STATE)
        ar = jnp.broadcast_to(abr_ref[:, sl], (B, S5_SLAB_STATE))
        ai = jnp.broadcast_to(abi_ref[:, sl], (B, S5_SLAB_STATE))
        hr, hi = hr_s[:, sl], hi_s[:, sl]
        for t in range(steps):
            rows = slice(t * B, (t + 1) * B)
            hr, hi = ((ar * hr - ai * hi) + xr_s[s, rows, :],
                      (ar * hi + ai * hr) + xi_s[s, rows, :])
            xr_s[s, rows, :] = hr
            xi_s[s, rows, :] = hi
        hr_s[:, sl] = hr
        hi_s[:, sl] = hi
        y_s[:, s * LANES:(s + 1) * LANES] = (
            jnp.dot(xr_s[s].astype(BF16), wcr_ref[s], preferred_element_type=F32)
            - jnp.dot(xi_s[s].astype(BF16), wci_ref[s], preferred_element_type=F32))
    y = y_s[...] + d_ref[...] * u
    z = _bdot(jax.nn.gelu(y), wglu_ref[...])
    o_ref[...] = x + z[:, :D_MODEL] * _sigmoid(z[:, D_MODEL:])
    cr, ci = cfr_ref[...], cfi_ref[...]
    hr_new = cr * hr_s[...] - ci * hi_s[...]
    hi_new = cr * hi_s[...] + ci * hr_s[...]
    hr_out[...] = hr_new.T if state_t else hr_new
    hi_out[...] = hi_new.T if state_t else hi_new


def _s5(x, g, prm, wglu_all, h0r, h0i, *, layer, B, TM, batch_major_in, state_t):
    n = x.shape[0] * x.shape[1] if batch_major_in else x.shape[0]
    body = functools.partial(_s5_body, B=B, TM=TM, state_t=state_t)
    st_shape = (S5_NSTATE, B) if state_t else (B, S5_NSTATE)
    consts = [g, prm['wbr'], prm['wbi'], prm['cfr'], prm['cfi'], prm['abr'], prm['abi'],
              prm['wcr'], prm['wci'], prm['d'], wglu_all, h0r, h0i]
    specs = [_layer_spec(a.shape, layer) if a is wglu_all else _const_spec(a.shape) for a in consts]
    return pl.pallas_call(
        body,
        grid=(n // TM,),
        in_specs=[_rows_or_batch_spec(TM, B, batch_major_in)] + specs,
        out_specs=[_row_spec(TM, D_MODEL), _const_spec(st_shape), _const_spec(st_shape)],
        out_shape=[jax.ShapeDtypeStruct((n, D_MODEL), F32),
                   jax.ShapeDtypeStruct(st_shape, F32), jax.ShapeDtypeStruct(st_shape, F32)],
        scratch_shapes=[pltpu.VMEM((S5_SLABS, TM, S5_SLAB_STATE), F32),
                        pltpu.VMEM((S5_SLABS, TM, S5_SLAB_STATE), F32),
                        pltpu.VMEM((B, S5_NSTATE), F32), pltpu.VMEM((B, S5_NSTATE), F32),
                        pltpu.VMEM((TM, D_MODEL), F32)] + _stage_scratch(TM, batch_major_in),
        compiler_params=_cparams(),
        name="s5_mixer",
    )(x, *consts)


def _s5_params(p):
    lam_re = jnp.minimum(p['s5_a_re'], -1e-4)
    lam_im = p['s5_a_im']
    dt = jnp.exp(p['s5_log_dt'])[:, None]
    mag = jnp.exp(lam_re * dt)
    ab_re = mag * jnp.cos(lam_im * dt)
    ab_im = mag * jnp.sin(lam_im * dt)
    den = lam_re * lam_re + lam_im * lam_im
    coef_re = ((ab_re - 1.0) * lam_re + ab_im * lam_im) / den
    coef_im = (ab_im * lam_re - (ab_re - 1.0) * lam_im) / den
    gps = S5_GROUPS // S5_SLABS
    same_group = (jnp.arange(gps * S5_GROUP)[:, None] // S5_GROUP
                  == jnp.arange(gps * S5_STATE)[None, :] // S5_STATE)

    def slab_in(b):
        rows = jnp.transpose(b.reshape(S5_SLABS, gps, S5_STATE, S5_GROUP), (0, 1, 3, 2))
        rows = rows.reshape(S5_SLABS, gps * S5_GROUP, S5_STATE)
        return jnp.where(same_group, jnp.tile(rows, (1, 1, gps)), 0.0).astype(BF16)

    def slab_out(c):
        cols = jnp.transpose(c.reshape(S5_SLABS, gps, S5_GROUP, S5_STATE), (0, 3, 1, 2))
        cols = cols.reshape(S5_SLABS, S5_STATE, gps * S5_GROUP)
        return jnp.where(same_group.T, jnp.tile(cols, (1, gps, 1)), 0.0).astype(BF16)

    flat = lambda a: a.reshape(1, S5_NSTATE)
    c_re, c_im = p['s5_c_re'], p['s5_c_im']
    cc_re = c_re * coef_re[:, None, :] - c_im * coef_im[:, None, :]
    cc_im = c_re * coef_im[:, None, :] + c_im * coef_re[:, None, :]
    return dict(wbr=slab_in(p['s5_b_re']), wbi=slab_in(p['s5_b_im']),
                wcr=slab_out(cc_re), wci=slab_out(cc_im),
                cfr=flat(coef_re), cfi=flat(coef_im), abr=flat(ab_re), abi=flat(ab_im),
                d=p['s5_d'].reshape(1, D_MODEL))


def _lru_body(x_ref, g_ref, win_ref, cw_ref, cb_ref, wrg_ref, brg_ref, wig_ref, big_ref, lam_ref,
              wout_ref, c0_ref, h0_ref, o_ref, cnew_ref, hnew_ref,
              carry_s, h_s, a_s, bx_s, *, B, TM):
    hist = (LRU_CONV - 1) * B

    @pl.when(pl.program_id(0) == 0)
    def _():
        carry_s[...] = c0_ref[...]
        h_s[...] = h0_ref[...]

    x = x_ref[...]
    xn = _rms(x, g_ref[...])
    gy = _bdot(xn, win_ref[...])
    gate_br = jax.nn.gelu(gy[:, :D_RNN])
    ext = jnp.concatenate([carry_s[...], gy[:, D_RNN:]], axis=0)
    u = cb_ref[...] + ext[0:TM] * cw_ref[0:1, :]
    for k in range(1, LRU_CONV):
        u = u + ext[k * B:k * B + TM] * cw_ref[k:k + 1, :]
    carry_s[...] = ext[TM:TM + hist]
    ub = u.astype(BF16)
    rg_parts, ig_parts = [], []
    for nb in range(LRU_BLOCKS):
        blk = ub[:, nb * LRU_BLOCK:(nb + 1) * LRU_BLOCK]
        rg_parts.append(jnp.dot(blk, wrg_ref[nb], preferred_element_type=F32))
        ig_parts.append(jnp.dot(blk, wig_ref[nb], preferred_element_type=F32))
    rg = _sigmoid(jnp.concatenate(rg_parts, axis=1) + brg_ref[...])
    ig = _sigmoid(jnp.concatenate(ig_parts, axis=1) + big_ref[...])
    log_sig = -_softplus(-lam_ref[...])
    log_a = LRU_C * rg * log_sig
    a = jnp.exp(log_a)
    a_s[...] = a
    bx_s[...] = jnp.sqrt(1.0 - a * a) * ig * u

    h_last = h_s[...]
    for t in range(TM // B):
        rows = slice(t * B, (t + 1) * B)
        h_last = a_s[rows, :] * h_last + bx_s[rows, :]
        bx_s[rows, :] = h_last
    h_s[...] = h_last
    o_ref[...] = x + _bdot(bx_s[...] * gate_br, wout_ref[...])
    cnew_ref[...] = carry_s[...]
    hnew_ref[...] = h_last


def _lru(x, g, prm, c0, h0, *, B, TM):
    n = x.shape[0]
    hist = (LRU_CONV - 1) * B
    body = functools.partial(_lru_body, B=B, TM=TM)
    consts = [g, prm['win'], prm['cw'], prm['cb'], prm['wrg'], prm['brg'], prm['wig'], prm['big'],
              prm['lam'], prm['wout'], c0, h0]
    return pl.pallas_call(
        body,
        grid=(n // TM,),
        in_specs=[_row_spec(TM, D_MODEL)] + [_const_spec(a.shape) for a in consts],
        out_specs=[_row_spec(TM, D_MODEL), _const_spec((hist, D_RNN)), _const_spec((B, D_RNN))],
        out_shape=[jax.ShapeDtypeStruct((n, D_MODEL), F32),
                   jax.ShapeDtypeStruct((hist, D_RNN), F32),
                   jax.ShapeDtypeStruct((B, D_RNN), F32)],
        scratch_shapes=[pltpu.VMEM((hist, D_RNN), F32), pltpu.VMEM((B, D_RNN), F32),
                        pltpu.VMEM((TM, D_RNN), F32), pltpu.VMEM((TM, D_RNN), F32)],
        compiler_params=_cparams(),
        name="rglru_mixer",
    )(x, *consts)


def _lru_params(p):
    row = lambda a: a.reshape(1, D_RNN)
    return dict(win=p['lru_w_in'].astype(BF16), cw=p['lru_conv_w'], cb=row(p['lru_conv_b']),
                wrg=p['lru_w_rg'].astype(BF16), brg=row(p['lru_b_rg']),
                wig=p['lru_w_ig'].astype(BF16), big=row(p['lru_b_ig']),
                lam=row(p['lru_lambda']), wout=p['lru_w_out'].astype(BF16))


NCOL = D_MODEL // LANES


def _emit_cols(o_ref, val):
    for j in range(NCOL):
        o_ref[j] = val[:, j * LANES:(j + 1) * LANES]


def _load_cols(ref):
    return jnp.concatenate([ref[j] for j in range(NCOL)], axis=1)


def _rw_pre_body(x_ref, g_ref, mu_ref, wrkv_ref, w0_ref, w1_ref, w2_ref, a0_ref, a1_ref, a2_ref,
                 g1_ref, g2_ref, kk_ref, ka_ref, sel_ref, selt_ref, sh0_ref,
                 r_o, cum_o, lw_o, k_o, v_o, kk_o, b_o, g_o, sh_o, sh_s, *, B, T):
    TM = T * B

    @pl.when(pl.program_id(0) == 0)
    def _():
        sh_s[...] = sh0_ref[...]

    def emit(o_ref, val):
        if T == 1:
            o_ref[...] = val
        else:
            _emit_cols(o_ref, val)

    xn = _rms(x_ref[...], g_ref[...])
    if TM > B:
        prev = jnp.concatenate([sh_s[...], xn[:TM - B]], axis=0)
    else:
        prev = sh_s[...]
    sh_s[...] = xn[TM - B:]
    xx = prev - xn
    mix = lambda n: xn + xx * mu_ref[n:n + 1, :]
    emit(r_o, _bdot(mix(0), wrkv_ref[0]))
    a = _sigmoid(a0_ref[...] + _bdot(_bdot(mix(4), a1_ref[...]), a2_ref[...]))
    k = _bdot(mix(1), wrkv_ref[1])
    kk = k * kk_ref[...]
    n2 = _head_sum(kk * kk, sel_ref[...], selt_ref[...])
    emit(g_o, _bdot(_sigmoid(_bdot(mix(5), g1_ref[...])), g2_ref[...]))
    wl = w0_ref[...] + _bdot(jnp.tanh(_bdot(mix(3), w1_ref[...])), w2_ref[...])
    lw = -jnp.exp(-_softplus(-wl) - 0.5)
    emit(lw_o, lw)
    cum = lw
    sh = B
    while sh < TM:
        cum = cum + jnp.concatenate([jnp.zeros((sh, D_MODEL), F32), cum[:TM - sh]], axis=0)
        sh *= 2
    emit(cum_o, cum)
    emit(v_o, _bdot(mix(2), wrkv_ref[2]))
    kk = kk * lax.rsqrt(jnp.maximum(n2, 1e-24))
    emit(kk_o, kk)
    emit(b_o, kk * a)
    emit(k_o, k * (1.0 + (a - 1.0) * ka_ref[...]))
    sh_o[...] = sh_s[...]


def _cols_spec(tm):
    return pl.BlockSpec((NCOL, tm, LANES), lambda i: (0, i, 0))


def _rw_pre(x, g, prm, sh0, *, B, T):
    n = x.shape[0]
    tm = T * B
    body = functools.partial(_rw_pre_body, B=B, T=T)
    consts = [g, prm['mu'], prm['wrkv'], prm['w0'], prm['w1'], prm['w2'], prm['a0'], prm['a1'],
              prm['a2'], prm['g1'], prm['g2'], prm['k_k'], prm['k_a'], prm['sel'], prm['sel_t'], sh0]
    if T == 1:
        big, big_spec = jax.ShapeDtypeStruct((n, D_MODEL), F32), _row_spec(tm, D_MODEL)
    else:
        big, big_spec = jax.ShapeDtypeStruct((NCOL, n, LANES), F32), _cols_spec(tm)
    return pl.pallas_call(
        body,
        grid=(n // tm,),
        in_specs=[_row_spec(tm, D_MODEL)] + [_const_spec(a.shape) for a in consts],
        out_specs=[big_spec] * 8 + [_const_spec((B, D_MODEL))],
        out_shape=[big] * 8 + [jax.ShapeDtypeStruct((B, D_MODEL), F32)],
        scratch_shapes=[pltpu.VMEM((B, D_MODEL), F32)],
        compiler_params=_cparams(),
        name="rwkv_project",
    )(x, *consts)


def _rw_post_body(x_ref, o_ref, r_ref, k_ref, v_ref, g_ref, rk_ref, lnw_ref, lnb_ref, sel_ref,
                  selt_ref, wo_ref, out_ref, *, T):
    load = (lambda ref: ref[...]) if T == 1 else _load_cols
    o, r, k, v, g = (load(ref) for ref in (o_ref, r_ref, k_ref, v_ref, g_ref))
    hsum = functools.partial(_head_sum, sel=sel_ref[...], sel_t=selt_ref[...])
    mean = hsum(o) * (1.0 / RWKV_HEAD)
    dlt = o - mean
    var = hsum(dlt * dlt) * (1.0 / RWKV_HEAD)
    on = dlt * lax.rsqrt(var + RWKV_GN_EPS) * lnw_ref[...] + lnb_ref[...]
    bonus = hsum(r * k * rk_ref[...]) * v
    out_ref[...] = x_ref[...] + _bdot((on + bonus) * g, wo_ref[...])


def _rw_post_streamed(x, o, r, k, v, g, consts, *, tm):
    n = x.shape[0]
    deep = pl.Buffered(3)
    row_in = pl.BlockSpec((tm, D_MODEL), lambda i: (i, 0), pipeline_mode=deep)
    cols_in = pl.BlockSpec((NCOL, tm, LANES), lambda i: (0, i, 0), pipeline_mode=deep)

    def outer(x_h, o_h, r_h, k_h, v_h, g_h, rk, lnw, lnb, sel, selt, wo, out_h):
        def step(x_ref, o_ref, r_ref, k_ref, v_ref, g_ref, out_ref):
            _rw_post_body(x_ref, o_ref, r_ref, k_ref, v_ref, g_ref, rk, lnw, lnb, sel, selt, wo,
                          out_ref, T=tm)
        pltpu.emit_pipeline(step, grid=(n // tm,), in_specs=[row_in] + [cols_in] * 5,
                            out_specs=[_row_spec(tm, D_MODEL)])(x_h, o_h, r_h, k_h, v_h, g_h, out_h)

    hbm = pl.BlockSpec(memory_space=pl.ANY)
    whole = pl.BlockSpec(memory_space=pltpu.VMEM)
    return pl.pallas_call(
        outer,
        in_specs=[hbm] * 6 + [whole] * len(consts),
        out_specs=hbm,
        out_shape=jax.ShapeDtypeStruct((n, D_MODEL), F32),
        compiler_params=pltpu.CompilerParams(vmem_limit_bytes=VMEM_LIMIT),
        name="rwkv_output",
    )(x, o, r, k, v, g, *consts)


def _rw_post(x, o, r, k, v, g, prm, *, B, T):
    n = x.shape[0]
    tm = T * B
    consts = [prm['r_k'], prm['ln_w'], prm['ln_b'], prm['sel'], prm['sel_t'], prm['wo']]
    if T > 1:
        return _rw_post_streamed(x, o, r, k, v, g, consts, tm=tm)
    big_spec = _row_spec(tm, D_MODEL) if T == 1 else _cols_spec(tm)
    return pl.pallas_call(
        functools.partial(_rw_post_body, T=T),
        grid=(n // tm,),
        in_specs=([_row_spec(tm, D_MODEL)] + [big_spec] * 5
                  + [_const_spec(a.shape) for a in consts]),
        out_specs=_row_spec(tm, D_MODEL),
        out_shape=jax.ShapeDtypeStruct((n, D_MODEL), F32),
        compiler_params=_cparams(),
        name="rwkv_output",
    )(x, o, r, k, v, g, *consts)


def _wkv_chunk_body(r_ref, cum_ref, lw_ref, k_ref, v_ref, kk_ref, b_ref, s0_ref, o_ref, sT_ref,
                    S_s, *, B, T):
    @pl.when(pl.program_id(0) == 0)
    def _():
        S_s[...] = s0_ref[...]

    lane = lax.broadcasted_iota(jnp.int32, (T, LANES), 1)
    trow = lax.broadcasted_iota(jnp.int32, (T, LANES), 0)
    first = lane < RWKV_HEAD
    strict = trow > (lane & (T - 1))
    incl = trow >= (lane & (T - 1))
    ri = lax.broadcasted_iota(jnp.int32, (LANES, LANES), 0)
    ci = lax.broadcasted_iota(jnp.int32, (LANES, LANES), 1)
    same_head = (ri < RWKV_HEAD) == (ci < RWKV_HEAD)

    def stack(y):
        return jnp.concatenate([jnp.where(first, y, 0.0), jnp.where(first, 0.0, y)],
                               axis=0).astype(BF16)

    nt = (((1,), (1,)), ((), ()))
    tn = (((0,), (0,)), ((), ()))
    nsteps = T.bit_length() - 1

    def per_group(g, carry):
        chains = [(g * WKV_ROWS + i, hp) for i in range(WKV_ROWS) for hp in range(RWKV_PAIRS)]
        idx = range(len(chains))
        at_chain = [(hp, pl.ds(b, T, stride=B), slice(None)) for b, hp in chains]
        s_old = [S_s[b, hp] for b, hp in chains]
        ks, vs, bbs, ends, e_ends, ps, sps = [], [], [], [], [], [], []
        for c in idx:
            at_c = at_chain[c]
            cm = cum_ref[at_c]
            k = k_ref[at_c]
            bb = b_ref[at_c]
            mid = cm[T // 2 - 1:T // 2, :]
            end = cm[T - 1:T, :]
            at = -kk_ref[at_c] * jnp.exp((cm - lw_ref[at_c]) - mid)
            rt = r_ref[at_c] * jnp.exp(cm - mid)
            e_neg = jnp.exp(mid - cm)
            lhs_f = jnp.concatenate([at, rt], axis=0)
            rhs = jnp.concatenate([stack(bb * e_neg), stack(k * e_neg)], axis=0)
            ps.append(lax.dot_general(lhs_f.astype(BF16), rhs, nt, preferred_element_type=F32))
            sps.append(lax.dot_general((lhs_f * jnp.exp(mid)).astype(BF16), s_old[c].astype(BF16),
                                       nt, preferred_element_type=F32))
            ks.append(k)
            vs.append(v_ref[at_c])
            bbs.append(bb)
            ends.append(end)
            e_ends.append(jnp.exp(end - cm))
        vst = [stack(v) for v in vs]
        ys = [sps[c][0:T] + jnp.dot(jnp.where(strict, ps[c][0:T, 2 * T:4 * T], 0.0).astype(BF16),
                                    vst[c], preferred_element_type=F32) for c in idx]
        ms = [jnp.where(strict, ps[c][0:T, 0:2 * T], 0.0) for c in idx]
        for it in range(nsteps):
            if it + 1 < nsteps:
                both = [jnp.dot(ms[c].astype(BF16),
                                jnp.concatenate([stack(ys[c]), stack(ms[c])], axis=1),
                                preferred_element_type=F32) for c in idx]
                ys = [ys[c] + both[c][:, 0:LANES] for c in idx]
                ms = [both[c][:, LANES:2 * LANES] for c in idx]
            else:
                ys = [ys[c] + jnp.dot(ms[c].astype(BF16), stack(ys[c]), preferred_element_type=F32)
                      for c in idx]
        outs, news = [], []
        for c in idx:
            rbk = jnp.where(jnp.concatenate([incl, incl], axis=1), ps[c][T:2 * T, :], 0.0)
            outs.append(sps[c][T:2 * T]
                        + jnp.dot(rbk.astype(BF16), jnp.concatenate([stack(ys[c]), vst[c]], axis=0),
                                  preferred_element_type=F32))
            upd = lax.dot_general(
                jnp.concatenate([ys[c], vs[c]], axis=0).astype(BF16),
                jnp.concatenate([bbs[c] * e_ends[c], ks[c] * e_ends[c]], axis=0).astype(BF16),
                tn, preferred_element_type=F32)
            news.append(s_old[c] * jnp.exp(ends[c]) + jnp.where(same_head, upd, 0.0))
        for c in idx:
            o_ref[at_chain[c]] = outs[c]
            S_s[chains[c][0], chains[c][1]] = news[c]
        return carry

    lax.fori_loop(0, B // WKV_ROWS, per_group, 0)

    @pl.when(pl.program_id(0) == pl.num_programs(0) - 1)
    def _():
        for b in range(B):
            for hp in range(RWKV_PAIRS):
                blk = S_s[b, hp]
                sT_ref[b, 2 * hp] = blk[0:RWKV_HEAD, 0:RWKV_HEAD]
                sT_ref[b, 2 * hp + 1] = pltpu.roll(blk, RWKV_HEAD, axis=1)[RWKV_HEAD:, 0:RWKV_HEAD]


def _wkv_chunk(r, cum, lw, k, v, kk, bvec, s0, *, B, T):
    n = r.shape[1]
    tm = T * B
    body = functools.partial(_wkv_chunk_body, B=B, T=T)
    st_shape = (B, RWKV_PAIRS, LANES, LANES)
    head_shape = (B, RWKV_HEADS, RWKV_HEAD, RWKV_HEAD)
    return pl.pallas_call(
        body,
        grid=(n // tm,),
        in_specs=[_cols_spec(tm)] * 7 + [_const_spec(st_shape)],
        out_specs=[_cols_spec(tm), _const_spec(head_shape)],
        out_shape=[jax.ShapeDtypeStruct((NCOL, n, LANES), F32),
                   jax.ShapeDtypeStruct(head_shape, F32)],
        scratch_shapes=[pltpu.VMEM(st_shape, F32)],
        compiler_params=_cparams(),
        name="wkv_chunked",
    )(r, cum, lw, k, v, kk, bvec, s0)


def _wkv_step_body(s_ref, r_ref, lw_ref, k_ref, kk_ref, b_ref, v_ref, o_ref, sn_ref):
    w = jnp.exp(lw_ref[...])
    kk, bv, k, r = kk_ref[...], b_ref[...], k_ref[...], r_ref[...]

    def per_value_row(i, carry):
        s = s_ref[i]
        sa = -jnp.sum(s * kk, axis=0, keepdims=True)
        sn = s * w + sa * bv + v_ref[pl.ds(i, 1), :] * k
        sn_ref[i] = sn
        o_ref[pl.ds(i, 1), :] = jnp.sum(sn * r, axis=0, keepdims=True)
        return carry

    lax.fori_loop(0, RWKV_HEAD, per_value_row, 0, unroll=8)


def _wkv_step(s0, r, lw, k, kk, bvec, v):
    bsz = s0.shape[-1]
    vec = lambda t: t.T.reshape(RWKV_HEADS, RWKV_HEAD, bsz)
    s_spec = pl.BlockSpec((None, RWKV_HEAD, RWKV_HEAD, bsz), lambda h: (h, 0, 0, 0))
    v_spec = pl.BlockSpec((None, RWKV_HEAD, bsz), lambda h: (h, 0, 0))
    o, sn = pl.pallas_call(
        _wkv_step_body,
        grid=(RWKV_HEADS,),
        in_specs=[s_spec] + [v_spec] * 6,
        out_specs=[v_spec, s_spec],
        out_shape=[jax.ShapeDtypeStruct((RWKV_HEADS, RWKV_HEAD, bsz), F32),
                   jax.ShapeDtypeStruct(s0.shape, F32)],
        compiler_params=_cparams(),
        name="wkv_step",
    )(s0, vec(r), vec(lw), vec(k), vec(kk), vec(bvec), vec(v))
    return o.reshape(D_MODEL, bsz).T, sn


def _rw_params(p):
    row = lambda a: a.reshape(1, D_MODEL)
    pad_c = lambda w: jnp.pad(w, ((0, 0), (0, LORA_PAD - w.shape[1]))).astype(BF16)
    pad_r = lambda w: jnp.pad(w, ((0, LORA_PAD - w.shape[0]), (0, 0))).astype(BF16)
    sel = (jnp.arange(D_MODEL)[:, None] // RWKV_HEAD == jnp.arange(LANES)[None, :]).astype(BF16)
    return dict(mu=p['rw_mu'], wrkv=p['rw_w_rkv'].astype(BF16), w0=row(p['rw_w0']),
                w1=pad_c(p['rw_w1']), w2=pad_r(p['rw_w2']), a0=row(p['rw_a0']),
                a1=pad_c(p['rw_a1']), a2=pad_r(p['rw_a2']), g1=pad_c(p['rw_g1']),
                g2=pad_r(p['rw_g2']), k_k=row(p['rw_k_k']), k_a=row(p['rw_k_a']),
                r_k=row(p['rw_r_k']), ln_w=row(p['rw_ln_w']), ln_b=row(p['rw_ln_b']),
                wo=p['rw_w_o'].astype(BF16), sel=sel, sel_t=sel.T)


def _pair_states(s):
    bsz = s.shape[0]
    s5 = s.reshape(bsz, RWKV_PAIRS, 2, RWKV_HEAD, RWKV_HEAD)
    eye = jnp.eye(2, dtype=s.dtype)
    bd = jnp.einsum('bpqij,qr->bpqirj', s5, eye)
    return bd.reshape(bsz, RWKV_PAIRS, LANES, LANES)


def _rwkv(x, g, prm, sh0, s0, *, B, L):
    T = min(L, WKV_CHUNK)
    r, cum, lw, k, v, kk, bvec, gt, sh_new = _rw_pre(x, g, prm, sh0, B=B, T=T)
    if L == 1:
        o, s_t = _wkv_step(jnp.transpose(s0, (1, 2, 3, 0)), r, lw, k, kk, bvec, v)
        s_new = jnp.transpose(s_t, (3, 0, 1, 2))
    else:
        o, s_new = _wkv_chunk(r, cum, lw, k, v, kk, bvec, _pair_states(s0), B=B, T=T)
    y = _rw_post(x, o, r, k, v, gt, prm, B=B, T=T)
    return y, sh_new, s_new


def _trunk(x, st, w, *, B, L):
    TM = B * min(L, WKV_CHUNK)
    TM_FFN = B * min(L, FFN_STEPS)
    regroup = L > 1
    if not regroup:
        x = x.reshape(B, D_MODEL)
    row = lambda a: a.reshape(1, -1)
    new = {k: [] for k in ('s5_re', 's5_im', 'rw_wkv', 'rw_shift', 'lru_h', 'lru_conv', 'ffn_conv')}
    for i in range(DEPTH):
        kind, j = i % N_MIXERS, i // N_MIXERS
        g = row(w['norm_mix'][i])
        if kind == 0:
            x, hr, hi = _s5(x, g, w['s5'][j], w['s5_w_glu'], st['s5_re'][j], st['s5_im'][j],
                            layer=j, B=B, TM=TM, batch_major_in=(regroup and i == 0),
                            state_t=(L == 1))
            new['s5_re'].append(hr)
            new['s5_im'].append(hi)
        elif kind == 1:
            x, sh, s = _rwkv(x, g, w['rw'][j], st['rw_shift'][j], st['rw_wkv'][j], B=B, L=L)
            new['rw_shift'].append(sh)
            new['rw_wkv'].append(s)
        else:
            x, cb, hl = _lru(x, g, w['lru'][j], st['lru_conv'][j], st['lru_h'][j], B=B, TM=TM)
            new['lru_conv'].append(cb)
            new['lru_h'].append(hl)
        x, cb = _ffn(x, row(w['norm_ffn'][i]), w['ffn_w_in'], w['ffn_conv_w'][i],
                     row(w['ffn_conv_b'][i]), w['ffn_w_out'], st['ffn_conv'][i],
                     row(w['norm_final']), layer=i, B=B, TM=TM_FFN, final_norm=(i == DEPTH - 1),
                     batch_major_out=(regroup and i == DEPTH - 1))
        new['ffn_conv'].append(cb)
    return x.reshape(B, L, D_MODEL), new


def _stack(parts):
    return parts[0][None] if len(parts) == 1 else jnp.stack(parts)


def _time_major_hist(buf):
    n, bsz, wm1, c = buf.shape
    t = jnp.transpose(buf, (0, 2, 1, 3)).reshape(n, wm1 * bsz, c)
    return [t[j] for j in range(n)]


def _batch_major_hist(rows, bsz):
    t = _stack(rows)
    n, _, c = t.shape
    return jnp.transpose(t.reshape(n, -1, bsz, c), (0, 2, 1, 3))


def _run(x, st, w):
    bsz, length, _ = x.shape
    flat = lambda a: [a[j].reshape(bsz, -1) for j in range(a.shape[0])]
    if length == 1:
        s5_in = lambda a: [jnp.transpose(a[j], (1, 2, 0)).reshape(S5_NSTATE, bsz)
                           for j in range(a.shape[0])]
        s5_out = lambda parts: jnp.transpose(
            _stack(parts).reshape(len(parts), S5_GROUPS, S5_STATE, bsz), (0, 3, 1, 2))
    else:
        s5_in = flat
        s5_out = lambda parts: _stack(parts).reshape(len(parts), bsz, S5_GROUPS, S5_STATE)
    stt = dict(s5_re=s5_in(st['s5_re']), s5_im=s5_in(st['s5_im']),
               rw_wkv=[st['rw_wkv'][j] for j in range(st['rw_wkv'].shape[0])],
               rw_shift=flat(st['rw_shift']), lru_h=flat(st['lru_h']),
               lru_conv=_time_major_hist(st['lru_conv']), ffn_conv=_time_major_hist(st['ffn_conv']))
    y, new = _trunk(x, stt, w, B=bsz, L=length)
    out = dict(
        s5_re=s5_out(new['s5_re']), s5_im=s5_out(new['s5_im']),
        rw_wkv=_stack(new['rw_wkv']), rw_shift=_stack(new['rw_shift']),
        lru_h=_stack(new['lru_h']), lru_conv=_batch_major_hist(new['lru_conv'], bsz),
        ffn_conv=_batch_major_hist(new['ffn_conv'], bsz))
    return y, out


def _prepare_weights(w):
    n_s5 = w['s5_a_re'].shape[0]
    n_rw = w['rw_mu'].shape[0]
    n_lru = w['lru_w_in'].shape[0]
    sub = lambda prefix, j: {k: v[j] for k, v in w.items() if k.startswith(prefix)}
    return dict(
        norm_mix=w['norm_mix'], norm_ffn=w['norm_ffn'], norm_final=w['norm_final'],
        s5=[_s5_params(sub('s5_', j)) for j in range(n_s5)], s5_w_glu=w['s5_w_glu'].astype(BF16),
        rw=[_rw_params(sub('rw_', j)) for j in range(n_rw)],
        lru=[_lru_params(sub('lru_', j)) for j in range(n_lru)],
        ffn_w_in=w['ffn_w_in'].astype(BF16), ffn_conv_w=w['ffn_conv_w'],
        ffn_conv_b=w['ffn_conv_b'], ffn_w_out=w['ffn_w_out'].astype(BF16))


def kernel(x_prompt, x_sample, state_s5_re, state_s5_im, state_rwkv_wkv, state_rwkv_shift, state_lru_h, state_lru_conv, state_ffn_conv, norm_mix, norm_ffn, norm_final, s5_a_re, s5_a_im, s5_log_dt, s5_b_re, s5_b_im, s5_c_re, s5_c_im, s5_d, s5_w_glu, rw_mu, rw_w_rkv, rw_w0, rw_w1, rw_w2, rw_a0, rw_a1, rw_a2, rw_g1, rw_g2, rw_k_k, rw_k_a, rw_r_k, rw_ln_w, rw_ln_b, rw_w_o, lru_w_in, lru_conv_w, lru_conv_b, lru_w_rg, lru_b_rg, lru_w_ig, lru_b_ig, lru_lambda, lru_w_out, ffn_w_in, ffn_conv_w, ffn_conv_b, ffn_w_out):
    w = _prepare_weights(dict(
        norm_mix=norm_mix, norm_ffn=norm_ffn, norm_final=norm_final,
        s5_a_re=s5_a_re, s5_a_im=s5_a_im, s5_log_dt=s5_log_dt, s5_b_re=s5_b_re, s5_b_im=s5_b_im,
        s5_c_re=s5_c_re, s5_c_im=s5_c_im, s5_d=s5_d, s5_w_glu=s5_w_glu,
        rw_mu=rw_mu, rw_w_rkv=rw_w_rkv, rw_w0=rw_w0, rw_w1=rw_w1, rw_w2=rw_w2, rw_a0=rw_a0,
        rw_a1=rw_a1, rw_a2=rw_a2, rw_g1=rw_g1, rw_g2=rw_g2, rw_k_k=rw_k_k, rw_k_a=rw_k_a,
        rw_r_k=rw_r_k, rw_ln_w=rw_ln_w, rw_ln_b=rw_ln_b, rw_w_o=rw_w_o,
        lru_w_in=lru_w_in, lru_conv_w=lru_conv_w, lru_conv_b=lru_conv_b, lru_w_rg=lru_w_rg,
        lru_b_rg=lru_b_rg, lru_w_ig=lru_w_ig, lru_b_ig=lru_b_ig, lru_lambda=lru_lambda,
        lru_w_out=lru_w_out, ffn_w_in=ffn_w_in, ffn_conv_w=ffn_conv_w, ffn_conv_b=ffn_conv_b,
        ffn_w_out=ffn_w_out))
    bsz, dt = x_prompt.shape[0], x_prompt.dtype
    n_s5, n_rw, n_lru = state_s5_re.shape[0], state_rwkv_wkv.shape[0], state_lru_h.shape[0]
    st_prompt = dict(
        s5_re=jnp.zeros((n_s5, bsz, S5_GROUPS, S5_STATE), dt),
        s5_im=jnp.zeros((n_s5, bsz, S5_GROUPS, S5_STATE), dt),
        rw_wkv=jnp.zeros((n_rw, bsz, RWKV_HEADS, RWKV_HEAD, RWKV_HEAD), dt),
        rw_shift=jnp.zeros((n_rw, bsz, D_MODEL), dt),
        lru_h=jnp.zeros((n_lru, bsz, D_RNN), dt),
        lru_conv=jnp.zeros((n_lru, bsz, LRU_CONV - 1, D_RNN), dt),
        ffn_conv=jnp.zeros((DEPTH, bsz, FFN_CONV - 1, D_FF), dt))
    st_sample = dict(s5_re=state_s5_re, s5_im=state_s5_im, rw_wkv=state_rwkv_wkv,
                     rw_shift=state_rwkv_shift, lru_h=state_lru_h, lru_conv=state_lru_conv,
                     ffn_conv=state_ffn_conv)
    y_p, new_p = _run(x_prompt, st_prompt, w)
    y_s, new_s = _run(x_sample, st_sample, w)
    return (y_p, y_s, new_p['s5_re'], new_s['s5_re'], new_p['s5_im'], new_s['s5_im'],
            new_p['rw_wkv'], new_s['rw_wkv'], new_p['rw_shift'], new_s['rw_shift'],
            new_p['lru_h'], new_s['lru_h'], new_p['lru_conv'], new_s['lru_conv'],
            new_p['ffn_conv'], new_s['ffn_conv'])
```
